```python
import math
import jax, jax.numpy as jnp
from jax import lax
import numpy as np

D_MODEL = 2048
BATCH = 4
SEQ = 2048
DEPTH = 2
DEC_BATCH = 8
DEC_SEQ = 32
PAST_LEN = 1024

CHUNK = 64
Q_BLOCK = 128
FOX_HEADS = 8
FOX_DH = 128
FOX_W = FOX_HEADS * FOX_DH
ATTN_SCALE = FOX_DH ** -0.5
GM_GROUPS = 4
GM_CH = 128
GM_W = GM_GROUPS * GM_CH
GM_CHUNK = 128
SSD_HEADS = 8
SSD_P = 64
SSD_W = SSD_HEADS * SSD_P
SSD_N = 128
SSD_G = 2
SSD_CONV = 4
SSD_CONV_CH = SSD_W + 2 * SSD_G * SSD_N
D_MIX = FOX_W + GM_W + SSD_W
D_FF = 5632
N_IN = 3 * FOX_W + FOX_HEADS + 2 * GM_W + SSD_W + SSD_CONV_CH + SSD_HEADS
EPS = 1e-6

kernel_name = 'hybrid_fox_gmlp_ssd_stream_step'


def rmsnorm(x, w):
    xf = x.astype(jnp.float32)
    y = xf * lax.rsqrt(jnp.mean(xf * xf, axis=-1, keepdims=True) + EPS)
    return (y * w.astype(jnp.float32)).astype(x.dtype)


def layernorm(x, w, b):
    xf = x.astype(jnp.float32)
    mu = jnp.mean(xf, axis=-1, keepdims=True)
    var = jnp.mean(jnp.square(xf - mu), axis=-1, keepdims=True)
    y = (xf - mu) * lax.rsqrt(var + EPS)
    return (y * w.astype(jnp.float32) + b.astype(jnp.float32)).astype(x.dtype)


def swiglu(h, wg, wu, wd):
    a = jnp.einsum('bld,df->blf', h, wg)
    g = jnp.einsum('bld,df->blf', h, wu)
    return jnp.einsum('blf,fd->bld', jax.nn.silu(a) * g, wd)


def split_in(h, w_in):
    proj = jnp.einsum('bld,dn->bln', h, w_in)
    sizes = [FOX_W, FOX_W, FOX_W, FOX_HEADS, GM_W, GM_W, SSD_W, SSD_CONV_CH, SSD_HEADS]
    return jnp.split(proj, np.cumsum(sizes)[:-1].tolist(), axis=-1)


def fox_prompt(q, k, v, logf):
    bsz, s_len = q.shape[0], q.shape[1]
    c = jnp.cumsum(logf, axis=1)
    c_key = jnp.transpose(c, (0, 2, 1))[:, :, None, :]
    k_pos = jnp.arange(s_len)

    def one_block(i):
        start = i * Q_BLOCK
        qb = lax.dynamic_slice_in_dim(q, start, Q_BLOCK, axis=1)
        cb = jnp.transpose(lax.dynamic_slice_in_dim(c, start, Q_BLOCK, axis=1), (0, 2, 1))[..., None]
        s = jnp.einsum('bqhd,bkhd->bhqk', qb, k).astype(jnp.float32) * ATTN_SCALE + cb - c_key
        q_pos = start + jnp.arange(Q_BLOCK)
        s = jnp.where(k_pos[None, :] <= q_pos[:, None], s, -jnp.inf)
        p = jax.nn.softmax(s, axis=-1).astype(v.dtype)
        return jnp.einsum('bhqk,bkhd->bqhd', p, v)

    o = lax.map(one_block, jnp.arange(s_len // Q_BLOCK))
    return jnp.transpose(o, (1, 0, 2, 3, 4)).reshape(bsz, s_len, FOX_W)


def fox_sample(q, k, v, logf, ck, cv, clogf):
    bsz, L = q.shape[0], q.shape[1]
    P = ck.shape[1]
    keys = jnp.concatenate([ck, k], axis=1)
    vals = jnp.concatenate([cv, v], axis=1)
    c = jnp.cumsum(jnp.concatenate([clogf.astype(jnp.float32), logf], axis=1), axis=1)
    cq = jnp.transpose(c[:, P:], (0, 2, 1))[..., None]
    ckey = jnp.transpose(c, (0, 2, 1))[:, :, None, :]
    s = jnp.einsum('bqhd,bkhd->bhqk', q, keys).astype(jnp.float32) * ATTN_SCALE + cq - ckey
    q_pos = P + jnp.arange(L)
    k_pos = jnp.arange(P + L)
    s = jnp.where(k_pos[None, :] <= q_pos[:, None], s, -jnp.inf)
    p = jax.nn.softmax(s, axis=-1).astype(vals.dtype)
    return jnp.einsum('bhqk,bkhd->bqhd', p, vals).reshape(bsz, L, FOX_W)


def gm_mix(u, v, ws, bs):
    bsz, L = v.shape[0], v.shape[1]
    lc = GM_CHUNK if L % GM_CHUNK == 0 else L
    wm = jnp.tril(ws[:, :lc, :lc])
    vc = v.reshape(bsz, L // lc, lc, GM_GROUPS, GM_CH)
    sp = jnp.einsum('gts,bcsgd->bctgd', wm, vc) + jnp.transpose(bs[:, :lc])[None, None, :, :, None]
    return u * sp.reshape(bsz, L, GM_W).astype(u.dtype)


def ssd_scan(xh, dt, A, Bm, Cm, s0, q_len):
    bsz, L = xh.shape[0], xh.shape[1]
    nc = L // q_len
    x_c = xh.reshape(bsz, nc, q_len, SSD_HEADS, SSD_P)
    dt_c = dt.reshape(bsz, nc, q_len, SSD_HEADS)
    b_c = Bm.reshape(bsz, nc, q_len, SSD_HEADS, SSD_N)
    c_c = Cm.reshape(bsz, nc, q_len, SSD_HEADS, SSD_N)
    acum = jnp.cumsum(dt_c * A, axis=2)
    diff = acum[:, :, :, None, :] - acum[:, :, None, :, :]
    causal = jnp.tril(jnp.ones((q_len, q_len), dtype=bool))
    lmat = jnp.exp(jnp.where(causal[None, None, :, :, None], diff, -jnp.inf))
    scores = jnp.einsum('bcthn,bcshn->bctsh', c_c, b_c) * lmat * dt_c[:, :, None, :, :]
    y_intra = jnp.einsum('bctsh,bcshp->bcthp', scores, x_c)
    decay_end = jnp.exp(acum[:, :, -1:, :] - acum) * dt_c
    chunk_states = jnp.einsum('bcsh,bcshn,bcshp->bchpn', decay_end, b_c, x_c)
    chunk_decay = jnp.exp(acum[:, :, -1, :])

    def step(s, inp):
        st, dec = inp
        return s * dec[:, :, None, None] + st, s

    s_final, s_in = lax.scan(step, s0, (jnp.moveaxis(chunk_states, 1, 0), jnp.moveaxis(chunk_decay, 1, 0)))
    s_in = jnp.moveaxis(s_in, 0, 1)
    y_inter = jnp.einsum('bcthn,bchpn->bcthp', c_c, s_in) * jnp.exp(acum)[..., None]
    return (y_intra + y_inter).reshape(bsz, L, SSD_HEADS, SSD_P), s_final


def ssd_mixer(z, xbc, dtr, conv_prev, s0, conv_w, conv_b, dt_bias, a_log, d_skip, norm_w):
    bsz, L = xbc.shape[0], xbc.shape[1]
    xpad = jnp.concatenate([conv_prev.astype(xbc.dtype), xbc], axis=1)
    acc = conv_b
    for i in range(SSD_CONV):
        acc = acc + xpad[:, i:i + L] * conv_w[i]
    act = jax.nn.silu(acc)
    new_conv = xpad[:, L:]
    xs, bm, cm = jnp.split(act, [SSD_W, SSD_W + SSD_G * SSD_N], axis=-1)
    xh = xs.reshape(bsz, L, SSD_HEADS, SSD_P).astype(jnp.float32)
    rep = SSD_HEADS // SSD_G
    bm = jnp.repeat(bm.reshape(bsz, L, SSD_G, SSD_N), rep, axis=2).astype(jnp.float32)
    cm = jnp.repeat(cm.reshape(bsz, L, SSD_G, SSD_N), rep, axis=2).astype(jnp.float32)
    dt = jax.nn.softplus(dtr.astype(jnp.float32) + dt_bias.astype(jnp.float32))
    A = -jnp.exp(a_log.astype(jnp.float32))
    q_len = CHUNK if L % CHUNK == 0 else L
    y, s_final = ssd_scan(xh, dt, A, bm, cm, s0.astype(jnp.float32), q_len)
    y = (y + d_skip.astype(jnp.float32)[:, None] * xh).reshape(bsz, L, SSD_W)
    y = rmsnorm(y * jax.nn.silu(z.astype(jnp.float32)), norm_w)
    return y.astype(z.dtype), new_conv, s_final.astype(z.dtype)


def mixer(h, l, p, cache):
    q, k, v, f_raw, gu, gv, z, xbc, dtr = split_in(h, p['w_in'][l])
    bsz, L = h.shape[0], h.shape[1]
    q = q.reshape(bsz, L, FOX_HEADS, FOX_DH)
    k = k.reshape(bsz, L, FOX_HEADS, FOX_DH)
    v = v.reshape(bsz, L, FOX_HEADS, FOX_DH)
    logf = jax.nn.log_sigmoid(f_raw.astype(jnp.float32) + p['fox_fb'][l].astype(jnp.float32))
    u = jax.nn.gelu(gu)
    gm_v = layernorm(jax.nn.gelu(gv), p['gm_ln_w'][l], p['gm_ln_b'][l])
    gm_o = gm_mix(u, gm_v, p['gm_ws'][l], p['gm_bs'][l])
    if cache is None:
        fox_o = fox_prompt(q, k, v, logf)
        conv_prev = jnp.zeros((bsz, SSD_CONV - 1, SSD_CONV_CH), xbc.dtype)
        ssm_prev = jnp.zeros((bsz, SSD_HEADS, SSD_P, SSD_N), jnp.float32)
    else:
        ck, cv, clf, cconv, cssm = cache
        fox_o = fox_sample(q, k, v, logf, ck[l], cv[l], clf[l])
        conv_prev, ssm_prev = cconv[l], cssm[l]
    ssd_o, new_conv, new_ssm = ssd_mixer(z, xbc, dtr, conv_prev, ssm_prev, p['ssd_conv_w'][l], p['ssd_conv_b'][l],
                                         p['ssd_dt_bias'][l], p['ssd_a_log'][l], p['ssd_d'][l], p['ssd_norm_w'][l])
    mixed = jnp.concatenate([fox_o, gm_o, ssd_o], axis=-1)
    o = jnp.einsum('bln,nd->bld', mixed, p['w_out'][l])
    if cache is None:
        return o, (k, v, logf, new_conv, new_ssm)
    return o, (k, v, logf, gm_v, new_conv, new_ssm)


def run_trunk(x, p, cache):
    states = []
    for l in range(DEPTH):
        nw = p['norm_w'][l]
        h = rmsnorm(x, nw[0])
        x = x + 0.5 * rmsnorm(swiglu(h, p['w_ffn_gate'][l, 0], p['w_ffn_up'][l, 0], p['w_ffn_down'][l, 0]), nw[1])
        o, st = mixer(rmsnorm(x, nw[2]), l, p, cache)
        x = x + rmsnorm(o, nw[3])
        h = rmsnorm(x, nw[4])
        x = x + 0.5 * rmsnorm(swiglu(h, p['w_ffn_gate'][l, 1], p['w_ffn_up'][l, 1], p['w_ffn_down'][l, 1]), nw[5])
        states.append(st)
    return x, [jnp.stack(t) for t in zip(*states)]


def setup_inputs(seed: int = 0) -> dict:
    key = jax.random.key(seed)
    ks = jax.random.split(key, 24)
    f32 = jnp.float32

    def nrm(k, shape, scale):
        return scale * jax.random.normal(k, shape, f32)

    dt0 = jnp.exp(jax.random.uniform(ks[19], (DEPTH, SSD_HEADS), f32, math.log(1e-3), math.log(1e-1)))
    return {
        'x_prompt': nrm(ks[0], (BATCH, SEQ, D_MODEL), 1.0),
        'x_sample': nrm(ks[1], (DEC_BATCH, DEC_SEQ, D_MODEL), 1.0),
        'cache_fox_k': nrm(ks[2], (DEPTH, DEC_BATCH, PAST_LEN, FOX_HEADS, FOX_DH), 1.0),
        'cache_fox_v': nrm(ks[3], (DEPTH, DEC_BATCH, PAST_LEN, FOX_HEADS, FOX_DH), 1.0),
        'cache_fox_logf': jax.nn.log_sigmoid(1.0 + nrm(ks[4], (DEPTH, DEC_BATCH, PAST_LEN, FOX_HEADS), 1.0)),
        'state_ssd_conv': nrm(ks[5], (DEPTH, DEC_BATCH, SSD_CONV - 1, SSD_CONV_CH), 1.0),
        'state_ssd': nrm(ks[6], (DEPTH, DEC_BATCH, SSD_HEADS, SSD_P, SSD_N), 0.5),
        'norm_w': 1.0 + nrm(ks[7], (DEPTH, 6, D_MODEL), 0.05),
        'w_ffn_gate': nrm(ks[8], (DEPTH, 2, D_MODEL, D_FF), D_MODEL ** -0.5),
        'w_ffn_up': nrm(ks[9], (DEPTH, 2, D_MODEL, D_FF), D_MODEL ** -0.5),
        'w_ffn_down': nrm(ks[10], (DEPTH, 2, D_FF, D_MODEL), D_FF ** -0.5),
        'w_in': nrm(ks[11], (DEPTH, D_MODEL, N_IN), D_MODEL ** -0.5),
        'fox_fb': 1.0 + nrm(ks[12], (DEPTH, FOX_HEADS), 0.5),
        'gm_ln_w': 1.0 + nrm(ks[13], (DEPTH, GM_W), 0.05),
        'gm_ln_b': nrm(ks[14], (DEPTH, GM_W), 0.02),
        'gm_ws': nrm(ks[15], (DEPTH, GM_GROUPS, GM_CHUNK, GM_CHUNK), GM_CHUNK ** -0.5),
        'gm_bs': 1.0 + nrm(ks[16], (DEPTH, GM_GROUPS, GM_CHUNK), 0.1),
        'ssd_conv_w': nrm(ks[17], (DEPTH, SSD_CONV, SSD_CONV_CH), SSD_CONV ** -0.5),
        'ssd_conv_b': nrm(ks[18], (DEPTH, SSD_CONV_CH), 0.02),
        'ssd_dt_bias': dt0 + jnp.log(-jnp.expm1(-dt0)),
        'ssd_a_log': jnp.log(jax.random.uniform(ks[20], (DEPTH, SSD_HEADS), f32, 1.0, 16.0)),
        'ssd_d': 1.0 + nrm(ks[21], (DEPTH, SSD_HEADS), 0.1),
        'ssd_norm_w': 1.0 + nrm(ks[22], (DEPTH, SSD_W), 0.05),
        'w_out': nrm(ks[23], (DEPTH, D_MIX, D_MODEL), D_MIX ** -0.5),
    }


def reference(x_prompt, x_sample, cache_fox_k, cache_fox_v, cache_fox_logf, state_ssd_conv, state_ssd,
              norm_w, w_ffn_gate, w_ffn_up, w_ffn_down, w_in, fox_fb, gm_ln_w, gm_ln_b, gm_ws, gm_bs,
              ssd_conv_w, ssd_conv_b, ssd_dt_bias, ssd_a_log, ssd_d, ssd_norm_w, w_out):
    p = {'norm_w': norm_w, 'w_ffn_gate': w_ffn_gate, 'w_ffn_up': w_ffn_up, 'w_ffn_down': w_ffn_down,
         'w_in': w_in, 'fox_fb': fox_fb, 'gm_ln_w': gm_ln_w, 'gm_ln_b': gm_ln_b, 'gm_ws': gm_ws, 'gm_bs': gm_bs,
         'ssd_conv_w': ssd_conv_w, 'ssd_conv_b': ssd_conv_b, 'ssd_dt_bias': ssd_dt_bias, 'ssd_a_log': ssd_a_log,
         'ssd_d': ssd_d, 'ssd_norm_w': ssd_norm_w, 'w_out': w_out}
    y_prompt, (p_k, p_v, p_lf, p_conv, p_ssm) = run_trunk(x_prompt, p, None)
    y_sample, (s_k, s_v, s_lf, s_gm, s_conv, s_ssm) = run_trunk(
        x_sample, p, (cache_fox_k, cache_fox_v, cache_fox_logf, state_ssd_conv, state_ssd))
    return (y_prompt, y_sample, p_k, p_v, p_lf, p_conv, p_ssm, s_k, s_v, s_lf, s_gm, s_conv, s_ssm)
```

```python
import functools

import jax
import jax.numpy as jnp
from jax import lax
from jax.experimental import pallas as pl
from jax.experimental.pallas import tpu as pltpu

F32 = jnp.float32
BF16 = jnp.bfloat16
EPS = 1e-6
LANES = 128
SSD_CHUNK = 64
GM_CHUNK = 128
VMEM_LIMIT = 56 * 1024 * 1024

MISC_W = LANES


def _cparams(sem):
    return pltpu.CompilerParams(dimension_semantics=sem, vmem_limit_bytes=VMEM_LIMIT)


def _rmsnorm(x, w):
    return x * lax.rsqrt(jnp.mean(x * x, axis=-1, keepdims=True) + EPS) * w


def _softplus(x):
    return jnp.maximum(x, 0.0) + jnp.log1p(jnp.exp(-jnp.abs(x)))


def _dot(a, b):
    return jnp.dot(a, b, preferred_element_type=F32)


def _dot_nt(a, b):
    return lax.dot_general(a, b, (((1,), (1,)), ((), ())), preferred_element_type=F32)


def _dot_tn(a, b):
    return lax.dot_general(a, b, (((0,), (0,)), ((), ())), preferred_element_type=F32)


def _row_to_col(row):
    n = row.shape[1]
    eye = lax.broadcasted_iota(jnp.int32, (n, n), 0) == lax.broadcasted_iota(jnp.int32, (n, n), 1)
    return jnp.sum(jnp.where(eye, jnp.broadcast_to(row, (n, n)), 0.0), axis=1, keepdims=True)


def _lane_cumsum(x, seg):
    lane = lax.broadcasted_iota(jnp.int32, x.shape, 1) & (seg - 1)
    k = 1
    while k < seg:
        x = x + jnp.where(lane >= k, pltpu.roll(x, k, axis=1), 0.0)
        k *= 2
    return x


def _flat_cumsum(x):
    x = _lane_cumsum(x, LANES)
    tot = x[:, LANES - 1:LANES]
    rows = []
    carry = jnp.zeros((1, 1), F32)
    for r in range(x.shape[0]):
        rows.append(x[r:r + 1, :] + carry)
        carry = carry + tot[r:r + 1, :]
    return jnp.concatenate(rows, axis=0)


def _ffn_body(x_ref, nwa_ref, nwb_ref, wg_ref, wu_ref, wd_ref, o_ref, h_ref):
    f = pl.program_id(1)

    @pl.when(f == 0)
    def _():
        h_ref[...] = _rmsnorm(x_ref[...], nwa_ref[...]).astype(BF16)

    h = h_ref[...]
    a = _dot(h, wg_ref[...])
    g = _dot(h, wu_ref[...])
    act = (a * jax.nn.sigmoid(a) * g).astype(BF16)
    part = _dot(act, wd_ref[...])

    @pl.when(f == 0)
    def _():
        o_ref[...] = part

    @pl.when(f > 0)
    def _():
        o_ref[...] += part

    @pl.when(f == pl.num_programs(1) - 1)
    def _():
        o_ref[...] = x_ref[...] + 0.5 * _rmsnorm(o_ref[...], nwb_ref[...])


def _ffn(x, norm_w3, wg, wu, wd, l, j, tm, tf):
    m, d = x.shape
    dff = wg.shape[-1]
    return pl.pallas_call(
        _ffn_body,
        grid=(m // tm, dff // tf),
        in_specs=[
            pl.BlockSpec((tm, d), lambda i, f: (i, 0)),
            pl.BlockSpec((None, 1, d), lambda i, f: (l * 6 + 3 * j + j, 0, 0)),
            pl.BlockSpec((None, 1, d), lambda i, f: (l * 6 + 3 * j + j + 1, 0, 0)),
            pl.BlockSpec((None, None, d, tf), lambda i, f: (l, j, 0, f)),
            pl.BlockSpec((None, None, d, tf), lambda i, f: (l, j, 0, f)),
            pl.BlockSpec((None, None, tf, d), lambda i, f: (l, j, f, 0)),
        ],
        out_specs=pl.BlockSpec((tm, d), lambda i, f: (i, 0)),
        out_shape=jax.ShapeDtypeStruct((m, d), F32),
        scratch_shapes=[pltpu.VMEM((tm, d), BF16)],
        compiler_params=_cparams(("parallel", "arbitrary")),
        name=f"ffn_{l}_{j}",
    )(x, norm_w3, norm_w3, wg, wu, wd)


def _inproj_body(x_ref, nw_ref, w_ref, bias_ref, alog_ref, o_ref, h_ref, *, misc_off):
    j = pl.program_id(1)

    @pl.when(j == 0)
    def _():
        h_ref[...] = _rmsnorm(x_ref[...], nw_ref[...]).astype(BF16)

    acc = _dot(h_ref[...], w_ref[...])
    o_ref[...] = acc

    @pl.when(j == pl.num_programs(1) - 1)
    def _():
        raw = acc[:, misc_off:misc_off + MISC_W] + bias_ref[...]
        lane = lax.broadcasted_iota(jnp.int32, raw.shape, 1)
        sp = _softplus(raw)
        logf = -_softplus(-raw)
        neg_a = -jnp.exp(alog_ref[...])
        val = jnp.where(lane < 8, logf, jnp.where(lane < 16, sp, jnp.where(lane < 24, sp * neg_a, 0.0)))
        o_ref[:, misc_off:misc_off + MISC_W] = val


def _inproj(x, norm_w3, w_r, bias_vec, alog_vec, l, tm, tn, misc_off):
    m, d = x.shape
    npad = w_r.shape[-1]
    return pl.pallas_call(
        functools.partial(_inproj_body, misc_off=misc_off),
        grid=(m // tm, npad // tn),
        in_specs=[
            pl.BlockSpec((tm, d), lambda i, j: (i, 0)),
            pl.BlockSpec((None, 1, d), lambda i, j: (l * 6 + 2, 0, 0)),
            pl.BlockSpec((None, d, tn), lambda i, j: (l, 0, j)),
            pl.BlockSpec((None, 1, MISC_W), lambda i, j: (l, 0, 0)),
            pl.BlockSpec((None, 1, MISC_W), lambda i, j: (l, 0, 0)),
        ],
        out_specs=pl.BlockSpec((tm, tn), lambda i, j: (i, j)),
        out_shape=jax.ShapeDtypeStruct((m, npad), F32),
        scratch_shapes=[pltpu.VMEM((tm, d), BF16)],
        compiler_params=_cparams(("parallel", "arbitrary")),
        name=f"inproj_{l}",
    )(x, norm_w3, w_r, bias_vec, alog_vec)


def _rows_as_lane_vector(c_ref, start, count):
    rows = c_ref[pl.ds(start, count), :]
    return jnp.concatenate([rows[r:r + 1, :] for r in range(count)], axis=1)


def _fox_prompt_body(lf_ref, q_ref, k_ref, v_ref, o_ref, c_scr, kb_scr, vb_scr, *, tq, tk, scale):
    qi = pl.program_id(2)

    @pl.when(qi == 0)
    def _():
        c_scr[...] = _flat_cumsum(lf_ref[...])
        kb_scr[...] = k_ref[...].astype(BF16)
        vb_scr[...] = v_ref[...].astype(BF16)

    q = q_ref[...].astype(BF16)
    cq_col = _row_to_col(_rows_as_lane_vector(c_scr, qi * (tq // LANES), tq // LANES))
    q_pos = qi * tq + lax.broadcasted_iota(jnp.int32, (tq, tk), 0)
    k_iota = lax.broadcasted_iota(jnp.int32, (tq, tk), 1)

    def body(kj, carry):
        m, l, acc = carry
        kb = kb_scr[pl.ds(pl.multiple_of(kj * tk, tk), tk), :]
        vb = vb_scr[pl.ds(pl.multiple_of(kj * tk, tk), tk), :]
        ck_row = _rows_as_lane_vector(c_scr, kj * (tk // LANES), tk // LANES)
        s = _dot_nt(q, kb) * scale + cq_col - ck_row
        s = jnp.where(kj * tk + k_iota <= q_pos, s, -jnp.inf)
        m_new = jnp.maximum(m, jnp.max(s, axis=1, keepdims=True))
        p = jnp.exp(s - m_new)
        alpha = jnp.exp(m - m_new)
        l = alpha * l + jnp.sum(p, axis=1, keepdims=True)
        acc = alpha * acc + _dot(p.astype(BF16), vb)
        return m_new, l, acc

    dh = q.shape[1]
    init = (jnp.full((tq, 1), -jnp.inf, F32), jnp.zeros((tq, 1), F32), jnp.zeros((tq, dh), F32))
    n_blocks = (qi * tq + tq + tk - 1) // tk
    _, l, acc = lax.fori_loop(0, n_blocks, body, init)
    o_ref[...] = (acc / l).astype(o_ref.dtype)


def _fox_prompt(proj, lf_rows, bsz, seq, heads, dh, col_q, col_k, col_v, tq, tk):
    nq = seq // tq
    n = seq // LANES
    return pl.pallas_call(
        functools.partial(_fox_prompt_body, tq=tq, tk=tk, scale=dh ** -0.5),
        grid=(bsz, heads, nq),
        in_specs=[
            pl.BlockSpec((None, None, n, LANES), lambda b, h, i: (b, h, 0, 0)),
            pl.BlockSpec((tq, dh), lambda b, h, i: (b * nq + i, col_q + h)),
            pl.BlockSpec((seq, dh), lambda b, h, i: (b, col_k + h)),
            pl.BlockSpec((seq, dh), lambda b, h, i: (b, col_v + h)),
        ],
        out_specs=pl.BlockSpec((tq, dh), lambda b, h, i: (b * nq + i, h)),
        out_shape=jax.ShapeDtypeStruct((bsz * seq, heads * dh), BF16),
        scratch_shapes=[pltpu.VMEM((n, LANES), F32), pltpu.VMEM((seq, dh), BF16), pltpu.VMEM((seq, dh), BF16)],
        compiler_params=_cparams(("parallel", "parallel", "arbitrary")),
        name="fox_prompt",
    )(lf_rows, proj, proj, proj)


def _fox_sample_body(lf_ref, q_ref, k_ref, v_ref, ck_ref, cv_ref, o_ref, *, past, new, scale):
    c = _flat_cumsum(lf_ref[...])
    n_past = past // LANES
    c_past = jnp.concatenate([c[r:r + 1, :] for r in range(n_past)], axis=1)
    c_new = c[n_past:n_past + 1, 0:new]
    cq_col = _row_to_col(c_new)
    q = q_ref[...].astype(BF16)
    s_past = _dot_nt(q, ck_ref[...].astype(BF16)) * scale + cq_col - c_past
    s_new = _dot_nt(q, k_ref[...].astype(BF16)) * scale + cq_col - c_new
    causal = lax.broadcasted_iota(jnp.int32, (new, new), 1) <= lax.broadcasted_iota(jnp.int32, (new, new), 0)
    s_new = jnp.where(causal, s_new, -jnp.inf)
    m = jnp.maximum(jnp.max(s_past, axis=1, keepdims=True), jnp.max(s_new, axis=1, keepdims=True))
    p_past = jnp.exp(s_past - m)
    p_new = jnp.exp(s_new - m)
    denom = jnp.sum(p_past, axis=1, keepdims=True) + jnp.sum(p_new, axis=1, keepdims=True)
    acc = _dot(p_past.astype(BF16), cv_ref[...].astype(BF16)) + _dot(p_new.astype(BF16), v_ref[...].astype(BF16))
    o_ref[...] = (acc / denom).astype(o_ref.dtype)


def _fox_sample(proj, lf_rows, cache_k, cache_v, l, row0, bsz, new, heads, dh, col_q, col_k, col_v):
    past = cache_k.shape[2]
    n = lf_rows.shape[2]
    rb0 = row0 // new
    return pl.pallas_call(
        functools.partial(_fox_sample_body, past=past, new=new, scale=dh ** -0.5),
        grid=(bsz, heads),
        in_specs=[
            pl.BlockSpec((None, None, n, LANES), lambda b, h: (b, h, 0, 0)),
            pl.BlockSpec((new, dh), lambda b, h: (rb0 + b, col_q + h)),
            pl.BlockSpec((new, dh), lambda b, h: (rb0 + b, col_k + h)),
            pl.BlockSpec((new, dh), lambda b, h: (rb0 + b, col_v + h)),
            pl.BlockSpec((None, None, past, dh), lambda b, h: (l, b, 0, h)),
            pl.BlockSpec((None, None, past, dh), lambda b, h: (l, b, 0, h)),
        ],
        out_specs=pl.BlockSpec((new, dh), lambda b, h: (b, h)),
        out_shape=jax.ShapeDtypeStruct((bsz * new, heads * dh), BF16),
        compiler_params=_cparams(("parallel", "parallel")),
        name="fox_sample",
    )(lf_rows, proj, proj, proj, cache_k, cache_v)


def _gmlp_body(gu_ref, gv_ref, lnw_ref, lnb_ref, wp_ref, ws_ref, bp_ref, bs_ref, o_ref, v_ref, *,
               n_prompt_tiles, groups, gch):
    i = pl.program_id(0)
    is_sample = i >= n_prompt_tiles
    u = jax.nn.gelu(gu_ref[...])
    gv = jax.nn.gelu(gv_ref[...])
    mu = jnp.mean(gv, axis=-1, keepdims=True)
    var = jnp.mean(jnp.square(gv - mu), axis=-1, keepdims=True)
    v = (gv - mu) * lax.rsqrt(var + EPS) * lnw_ref[...] + lnb_ref[...]

    @pl.when(is_sample)
    def _():
        v_ref[...] = v

    tm = u.shape[0]
    tril = lax.broadcasted_iota(jnp.int32, (GM_CHUNK, GM_CHUNK), 1) <= lax.broadcasted_iota(
        jnp.int32, (GM_CHUNK, GM_CHUNK), 0)
    bias_all = jnp.where(is_sample, bs_ref[...], bp_ref[...])
    for g in range(groups):
        wm = jnp.where(is_sample, ws_ref[g], wp_ref[g])
        wm = jnp.where(tril, wm, 0.0).astype(BF16)
        bias = bias_all[:, g:g + 1]
        for c in range(tm // GM_CHUNK):
            rows = slice(c * GM_CHUNK, (c + 1) * GM_CHUNK)
            cols = slice(g * gch, (g + 1) * gch)
            sp = _dot(wm, v[rows, cols].astype(BF16)) + bias
            o_ref[rows, cols] = (u[rows, cols] * sp).astype(o_ref.dtype)


def _gmlp(proj, lnw, lnb, w_prompt, w_sample, b_prompt, b_sample, l, n_prompt_rows, groups, gch,
          col_u, col_v, tm):
    m = proj.shape[0]
    gw = groups * gch
    npt = n_prompt_rows // tm
    return pl.pallas_call(
        functools.partial(_gmlp_body, n_prompt_tiles=npt, groups=groups, gch=gch),
        grid=(m // tm,),
        in_specs=[
            pl.BlockSpec((tm, gw), lambda i: (i, col_u)),
            pl.BlockSpec((tm, gw), lambda i: (i, col_v)),
            pl.BlockSpec((None, 1, gw), lambda i: (l, 0, 0)),
            pl.BlockSpec((None, 1, gw), lambda i: (l, 0, 0)),
            pl.BlockSpec((None, groups, GM_CHUNK, GM_CHUNK), lambda i: (l, 0, 0, 0)),
            pl.BlockSpec((None, groups, GM_CHUNK, GM_CHUNK), lambda i: (l, 0, 0, 0)),
            pl.BlockSpec((None, GM_CHUNK, groups), lambda i: (l, 0, 0)),
            pl.BlockSpec((None, GM_CHUNK, groups), lambda i: (l, 0, 0)),
        ],
        out_specs=[
            pl.BlockSpec((tm, gw), lambda i: (i, 0)),
            pl.BlockSpec((tm, gw), lambda i: (jnp.maximum(i - npt, 0), 0)),
        ],
        out_shape=[jax.ShapeDtypeStruct((m, gw), BF16), jax.ShapeDtypeStruct((m - n_prompt_rows, gw), F32)],
        compiler_params=_cparams(("arbitrary",)),
        name="gmlp",
    )(proj, proj, lnw, lnb, w_prompt, w_sample, b_prompt, b_sample)


def _ssd_body(xbc_ref, z_ref, misc_ref, dtrow_ref, arow_ref, prev_ref, s0_ref, cw_ref, cb_ref, dvec_ref,
              nw_ref, o_ref, conv_ref, st_ref, xp_scr, s_scr, y_scr, *, lt, q, heads, hp, ngroups, sn, conv_k):
    t = pl.program_id(1)
    nt = pl.num_programs(1)
    xw = heads * hp
    pad = 8

    @pl.when(t == 0)
    def _():
        xp_scr[0:pad, :] = prev_ref[...]
        s_scr[...] = s0_ref[...]

    xbc = xbc_ref[...]
    xp_scr[pad:pad + lt, :] = xbc
    acc = cb_ref[...]
    for i in range(conv_k):
        off = pad - (conv_k - 1) + i
        acc = acc + xp_scr[off:off + lt, :] * cw_ref[i:i + 1, :]
    xp_scr[0:pad, :] = xbc[lt - pad:lt, :]
    act = acc * jax.nn.sigmoid(acc)

    @pl.when(t == nt - 1)
    def _():
        conv_ref[...] = xbc[lt - (conv_k - 1):lt, :]

    misc = misc_ref[...]
    dt_rows = dtrow_ref[...]
    ltp = dt_rows.shape[1]
    a_rows = jnp.concatenate(
        [_lane_cumsum(arow_ref[:, j * LANES:(j + 1) * LANES], q) for j in range(ltp // LANES)], axis=1)
    tril = lax.broadcasted_iota(jnp.int32, (q, q), 1) <= lax.broadcasted_iota(jnp.int32, (q, q), 0)
    rep = heads // ngroups
    for c in range(lt // q):
        r0 = c * q
        bmat = [act[r0:r0 + q, xw + g * sn: xw + (g + 1) * sn].astype(BF16) for g in range(ngroups)]
        cmat = [act[r0:r0 + q, xw + (ngroups + g) * sn: xw + (ngroups + g + 1) * sn].astype(BF16)
                for g in range(ngroups)]
        gmat = [_dot_nt(cmat[g], bmat[g]) for g in range(ngroups)]
        for h in range(heads):
            g = h // rep
            a_row = a_rows[h:h + 1, r0:r0 + q]
            dt_row = dt_rows[h:h + 1, r0:r0 + q]
            a_col = _row_to_col(a_row)
            dt_col = misc[r0:r0 + q, 8 + h:9 + h]
            a_last = a_row[:, q - 1:q]
            lmat = jnp.exp(jnp.where(tril, a_col - a_row, -jnp.inf))
            scores = gmat[g] * lmat * dt_row
            xh = act[r0:r0 + q, h * hp:(h + 1) * hp]
            y = _dot(scores.astype(BF16), xh.astype(BF16))
            decay_end = jnp.exp(a_last - a_col) * dt_col
            chunk_state = _dot_tn((xh * decay_end).astype(BF16), bmat[g])
            s_in = s_scr[h]
            y = y + _dot_nt(cmat[g], s_in.astype(BF16)) * jnp.exp(a_col)
            s_scr[h] = s_in * jnp.exp(a_last) + chunk_state
            y_scr[r0:r0 + q, h * hp:(h + 1) * hp] = y + dvec_ref[:, h * hp:(h + 1) * hp] * xh

    zz = z_ref[...]
    o_ref[...] = _rmsnorm(y_scr[...] * (zz * jax.nn.sigmoid(zz)), nw_ref[...]).astype(o_ref.dtype)

    @pl.when(t == nt - 1)
    def _():
        st_ref[...] = s_scr[...]


def _ssd(proj, dt_rows, a_rows, conv_prev, s0, conv_w, conv_b, dvec, norm_w, l, row0, bsz, seq, lt, q,
         heads, hp, ngroups, sn, col_xbc, col_z, col_misc):
    nt = seq // lt
    xw = heads * hp
    cch = xw + 2 * ngroups * sn
    conv_k = conv_w.shape[1]
    rb0 = row0 // lt
    ltp = max(lt, LANES)
    body = functools.partial(_ssd_body, lt=lt, q=q, heads=heads, hp=hp, ngroups=ngroups, sn=sn, conv_k=conv_k)
    return pl.pallas_call(
        body,
        grid=(bsz, nt),
        in_specs=[
            pl.BlockSpec((lt, cch), lambda b, t: (rb0 + b * nt + t, col_xbc)),
            pl.BlockSpec((lt, xw), lambda b, t: (rb0 + b * nt + t, col_z)),
            pl.BlockSpec((lt, MISC_W), lambda b, t: (rb0 + b * nt + t, col_misc)),
            pl.BlockSpec((None, heads, ltp), lambda b, t: (b, 0, t)),
            pl.BlockSpec((None, heads, ltp), lambda b, t: (b, 0, t)),
            pl.BlockSpec((None, 8, cch), lambda b, t: (b, 0, 0)),
            pl.BlockSpec((None, heads, hp, sn), lambda b, t: (b, 0, 0, 0)),
            pl.BlockSpec((None, conv_k, cch), lambda b, t: (l, 0, 0)),
            pl.BlockSpec((None, 1, cch), lambda b, t: (l, 0, 0)),
            pl.BlockSpec((None, 1, xw), lambda b, t: (l, 0, 0)),
            pl.BlockSpec((None, 1, xw), lambda b, t: (l, 0, 0)),
        ],
        out_specs=[
            pl.BlockSpec((lt, xw), lambda b, t: (b * nt + t, 0)),
            pl.BlockSpec((None, conv_k - 1, cch), lambda b, t: (b, 0, 0)),
            pl.BlockSpec((None, heads, hp, sn), lambda b, t: (b, 0, 0, 0)),
        ],
        out_shape=[
            jax.ShapeDtypeStruct((bsz * seq, xw), BF16),
            jax.ShapeDtypeStruct((bsz, conv_k - 1, cch), F32),
            jax.ShapeDtypeStruct((bsz, heads, hp, sn), F32),
        ],
        scratch_shapes=[pltpu.VMEM((lt + 8, cch), F32), pltpu.VMEM((heads, hp, sn), F32),
                        pltpu.VMEM((lt, xw), F32)],
        compiler_params=_cparams(("parallel", "arbitrary")),
        name="ssd",
    )(proj, proj, proj, dt_rows, a_rows, conv_prev, s0, conv_w, conv_b, dvec, norm_w)


def _outproj_body(fox_ref, gm_ref, ssd_ref, x_ref, nw_ref, w_ref, o_ref, *, fw, gw):
    o = _dot(fox_ref[...], w_ref[0:fw, :])
    o = o + _dot(gm_ref[...], w_ref[fw:fw + gw, :])
    o = o + _dot(ssd_ref[...], w_ref[fw + gw:, :])
    o_ref[...] = x_ref[...] + _rmsnorm(o, nw_ref[...])


def _outproj(fox_o, gm_o, ssd_o, x, norm_w3, w_out, l, tm):
    m, d = x.shape
    fw, gw, sw = fox_o.shape[1], gm_o.shape[1], ssd_o.shape[1]
    return pl.pallas_call(
        functools.partial(_outproj_body, fw=fw, gw=gw),
        grid=(m // tm,),
        in_specs=[
            pl.BlockSpec((tm, fw), lambda i: (i, 0)),
            pl.BlockSpec((tm, gw), lambda i: (i, 0)),
            pl.BlockSpec((tm, sw), lambda i: (i, 0)),
            pl.BlockSpec((tm, d), lambda i: (i, 0)),
            pl.BlockSpec((None, 1, d), lambda i: (l * 6 + 3, 0, 0)),
            pl.BlockSpec((None, fw + gw + sw, d), lambda i: (l, 0, 0)),
        ],
        out_specs=pl.BlockSpec((tm, d), lambda i: (i, 0)),
        out_shape=jax.ShapeDtypeStruct((m, d), F32),
        compiler_params=_cparams(("parallel",)),
        name=f"outproj_{l}",
    )(fox_o, gm_o, ssd_o, x, norm_w3, w_out)


def _head_rows(vals, bsz, seq):
    return jnp.transpose(vals.reshape(bsz, seq, vals.shape[-1]), (0, 2, 1))


def _pad_lanes(rows, width):
    return jnp.pad(rows, ((0, 0), (0, 0), (0, width - rows.shape[-1])))


def kernel(x_prompt, x_sample, cache_fox_k, cache_fox_v, cache_fox_logf, state_ssd_conv, state_ssd, norm_w,
           w_ffn_gate, w_ffn_up, w_ffn_down, w_in, fox_fb, gm_ln_w, gm_ln_b, gm_ws, gm_bs, ssd_conv_w,
           ssd_conv_b, ssd_dt_bias, ssd_a_log, ssd_d, ssd_norm_w, w_out):
    bsz, seq, d = x_prompt.shape
    dbsz, dseq, _ = x_sample.shape
    depth, _, past, heads, dh = cache_fox_k.shape
    fox_w = heads * dh
    groups, gchunk = gm_ws.shape[1], gm_ws.shape[2]
    gm_w = gm_ln_w.shape[1]
    gch = gm_w // groups
    _, _, sheads, hp, sn = state_ssd.shape
    ssd_w = sheads * hp
    cch = ssd_conv_w.shape[2]
    ngroups = (cch - ssd_w) // (2 * sn)
    conv_k = ssd_conv_w.shape[1]
    n_p, n_s = bsz * seq, dbsz * dseq
    m = n_p + n_s
    assert gchunk == GM_CHUNK and heads == 8 and sheads == 8 and dh == LANES
    assert seq % GM_CHUNK == 0 and GM_CHUNK % dseq == 0 and dseq < SSD_CHUNK and past % LANES == 0

    wg, wu, wd = (w.astype(BF16) for w in (w_ffn_gate, w_ffn_up, w_ffn_down))
    wo = w_out.astype(BF16)
    sizes = [fox_w, fox_w, fox_w, heads, gm_w, gm_w, ssd_w, cch, sheads]
    offs = [0]
    for s in sizes:
        offs.append(offs[-1] + s)
    wq, wk, wv, wf, wgu, wgv, wz, wxbc, wdt = (w_in[:, :, offs[i]:offs[i + 1]] for i in range(9))
    tn = 1024
    used = 3 * fox_w + 2 * gm_w + cch + ssd_w + MISC_W
    npad = -(-used // tn) * tn
    zeros_misc = jnp.zeros((depth, d, MISC_W - heads - 2 * sheads), F32)
    zeros_tail = jnp.zeros((depth, d, npad - used), F32)
    w_r = jnp.concatenate([wq, wk, wv, wgu, wgv, wxbc, wz, wf, wdt, wdt, zeros_misc, zeros_tail],
                          axis=-1).astype(BF16)
    col_q, col_k, col_v = 0, fox_w // dh, 2 * fox_w // dh
    col_u, col_gv = 3 * fox_w // gm_w, 3 * fox_w // gm_w + 1
    off_xbc = 3 * fox_w + 2 * gm_w
    off_z = off_xbc + cch
    off_misc = off_z + ssd_w
    assert off_xbc % cch == 0 and off_z % ssd_w == 0 and off_misc % MISC_W == 0
    misc_in_tile = off_misc % tn
    assert off_misc // tn == npad // tn - 1 and misc_in_tile + MISC_W <= tn

    lane_pad = jnp.zeros((depth, MISC_W - heads - 2 * sheads), F32)
    bias_vec = jnp.concatenate([fox_fb, ssd_dt_bias, ssd_dt_bias, lane_pad], axis=-1)[:, None, :]
    alog_vec = jnp.concatenate([jnp.zeros((depth, heads + sheads), F32), ssd_a_log, lane_pad], axis=-1)[:, None, :]
    norm_w3 = norm_w.reshape(depth * 6, 1, d)
    lnw3, lnb3 = gm_ln_w[:, None, :], gm_ln_b[:, None, :]
    reps = GM_CHUNK // dseq
    eye = jnp.eye(reps, dtype=F32)
    ws_s = jnp.einsum('ab,lgts->lgatbs', eye, gm_ws[:, :, :dseq, :dseq]).reshape(depth, groups, GM_CHUNK, GM_CHUNK)
    bs_p = jnp.transpose(gm_bs, (0, 2, 1))
    bs_s = jnp.tile(jnp.transpose(gm_bs[:, :, :dseq], (0, 2, 1)), (1, reps, 1))
    conv_b3 = ssd_conv_b[:, None, :]
    dvec = jnp.repeat(ssd_d, hp, axis=-1)[:, None, :]
    snw3 = ssd_norm_w[:, None, :]
    cache_k = cache_fox_k.reshape(depth, dbsz, past, fox_w)
    cache_v = cache_fox_v.reshape(depth, dbsz, past, fox_w)
    conv_prev_p = jnp.zeros((bsz, 8, cch), F32)
    conv_prev_s = jnp.pad(state_ssd_conv, ((0, 0), (0, 0), (8 - (conv_k - 1), 0), (0, 0)))
    s0_p = jnp.zeros((bsz, sheads, hp, sn), F32)

    x = jnp.concatenate([x_prompt.reshape(n_p, d), x_sample.reshape(n_s, d)], axis=0)
    tm_big = m // 8 if m % 64 == 0 and (m // 8) % 16 == 0 else 256
    tm_ffn = m // 12 if m % 12 == 0 and (m // 12) % 16 == 0 else 256
    tm_out = m // 16 if m % 16 == 0 and (m // 16) % 16 == 0 else 256
    lf_len = 2 * past
    assert past + dseq <= lf_len

    outs = {k: [] for k in ("pk", "pv", "plf", "pconv", "pst", "sk", "sv", "slf", "sgm", "sconv", "sst")}
    for l in range(depth):
        x = _ffn(x, norm_w3, wg, wu, wd, l, 0, tm_ffn, 512)
        proj = _inproj(x, norm_w3, w_r, bias_vec, alog_vec, l, tm_big, tn, misc_in_tile)
        misc = proj[:, off_misc:off_misc + heads + 2 * sheads]
        logf_p, logf_s = misc[:n_p, :heads], misc[n_p:, :heads]
        lf_rows_p = _head_rows(logf_p, bsz, seq).reshape(bsz, heads, seq // LANES, LANES)
        lf_all = jnp.concatenate([cache_fox_logf[l], logf_s.reshape(dbsz, dseq, heads),
                                  jnp.zeros((dbsz, lf_len - past - dseq, heads), F32)], axis=1)
        lf_rows_s = jnp.transpose(lf_all, (0, 2, 1)).reshape(dbsz, heads, lf_len // LANES, LANES)
        dt_rows_p = _head_rows(misc[:n_p, heads:heads + sheads], bsz, seq)
        a_rows_p = _head_rows(misc[:n_p, heads + sheads:], bsz, seq)
        dt_rows_s = _pad_lanes(_head_rows(misc[n_p:, heads:heads + sheads], dbsz, dseq), LANES)
        a_rows_s = _pad_lanes(_head_rows(misc[n_p:, heads + sheads:], dbsz, dseq), LANES)

        fox_p = _fox_prompt(proj, lf_rows_p, bsz, seq, heads, dh, col_q, col_k, col_v, 256, 256)
        fox_s = _fox_sample(proj, lf_rows_s, cache_k, cache_v, l, n_p, dbsz, dseq, heads, dh, col_q, col_k, col_v)
        gm_o, gm_v_s = _gmlp(proj, lnw3, lnb3, gm_ws, ws_s, bs_p, bs_s, l, n_p, groups, gch, col_u, col_gv, 256)
        ssd_p, conv_p, st_p = _ssd(proj, dt_rows_p, a_rows_p, conv_prev_p, s0_p, ssd_conv_w, conv_b3, dvec, snw3,
                                   l, 0, bsz, seq, 256, SSD_CHUNK, sheads, hp, ngroups, sn,
                                   off_xbc // cch, off_z // ssd_w, off_misc // MISC_W)
        ssd_s, conv_s, st_s = _ssd(proj, dt_rows_s, a_rows_s, conv_prev_s[l], state_ssd[l], ssd_conv_w, conv_b3,
                                   dvec, snw3, l, n_p, dbsz, dseq, dseq, dseq, sheads, hp, ngroups, sn,
                                   off_xbc // cch, off_z // ssd_w, off_misc // MISC_W)
        fox_o = jnp.concatenate([fox_p, fox_s], axis=0)
        ssd_o = jnp.concatenate([ssd_p, ssd_s], axis=0)
        x = _outproj(fox_o, gm_o, ssd_o, x, norm_w3, wo, l, tm_out)
        x = _ffn(x, norm_w3, wg, wu, wd, l, 1, tm_ffn, 512)

        k_all = proj[:, fox_w:2 * fox_w]
        v_all = proj[:, 2 * fox_w:3 * fox_w]
        outs["pk"].append(k_all[:n_p].reshape(bsz, seq, heads, dh))
        outs["pv"].append(v_all[:n_p].reshape(bsz, seq, heads, dh))
        outs["plf"].append(logf_p.reshape(bsz, seq, heads))
        outs["pconv"].append(conv_p)
        outs["pst"].append(st_p)
        outs["sk"].append(k_all[n_p:].reshape(dbsz, dseq, heads, dh))
        outs["sv"].append(v_all[n_p:].reshape(dbsz, dseq, heads, dh))
        outs["slf"].append(logf_s.reshape(dbsz, dseq, heads))
        outs["sgm"].append(gm_v_s.reshape(dbsz, dseq, gm_w))
        outs["sconv"].append(conv_s)
        outs["sst"].append(st_s)

    st = {k: jnp.stack(v) for k, v in outs.items()}
    y_prompt = x[:n_p].reshape(bsz, seq, d)
    y_sample = x[n_p:].reshape(dbsz, dseq, d)
    return (y_prompt, y_sample, st["pk"], st["pv"], st["plf"], st["pconv"], st["pst"],
            st["sk"], st["sv"], st["slf"], st["sgm"], st["sconv"], st["sst"])
```

```python
import functools

import jax
import jax.numpy as jnp
from jax import lax
from jax.experimental import pallas as pl
from jax.experimental.pallas import tpu as pltpu

F32 = jnp.float32
BF16 = jnp.bfloat16
EPS = 1e-6
LANES = 128
SSD_CHUNK = 64
GM_CHUNK = 128
VMEM_LIMIT = 56 * 1024 * 1024

MISC_W = LANES


def _cparams(sem):
    return pltpu.CompilerParams(dimension_semantics=sem, vmem_limit_bytes=VMEM_LIMIT)


def _rmsnorm(x, w):
    return x * lax.rsqrt(jnp.mean(x * x, axis=-1, keepdims=True) + EPS) * w


def _softplus(x):
    return jnp.maximum(x, 0.0) + jnp.log1p(jnp.exp(-jnp.abs(x)))


def _dot(a, b):
    return jnp.dot(a, b, preferred_element_type=F32)


def _dot_nt(a, b):
    return lax.dot_general(a, b, (((1,), (1,)), ((), ())), preferred_element_type=F32)


def _dot_tn(a, b):
    return lax.dot_general(a, b, (((0,), (0,)), ((), ())), preferred_element_type=F32)


def _row_to_col(row):
    n = row.shape[1]
    eye = lax.broadcasted_iota(jnp.int32, (n, n), 0) == lax.broadcasted_iota(jnp.int32, (n, n), 1)
    return jnp.sum(jnp.where(eye, jnp.broadcast_to(row, (n, n)), 0.0), axis=1, keepdims=True)


def _lane_cumsum(x, seg):
    lane = lax.broadcasted_iota(jnp.int32, x.shape, 1) & (seg - 1)
    k = 1
    while k < seg:
        x = x + jnp.where(lane >= k, pltpu.roll(x, k, axis=1), 0.0)
        k *= 2
    return x


def _flat_cumsum(x):
    x = _lane_cumsum(x, LANES)
    tot = x[:, LANES - 1:LANES]
    rows = []
    carry = jnp.zeros((1, 1), F32)
    for r in range(x.shape[0]):
        rows.append(x[r:r + 1, :] + carry)
        carry = carry + tot[r:r + 1, :]
    return jnp.concatenate(rows, axis=0)


def _ffn_body(x_ref, nwa_ref, nwb_ref, wg_ref, wu_ref, wd_ref, o_ref, h_ref):
    f = pl.program_id(1)

    @pl.when(f == 0)
    def _():
        h_ref[...] = _rmsnorm(x_ref[...], nwa_ref[...]).astype(BF16)
        o_ref[...] = jnp.zeros_like(o_ref)

    h = h_ref[...]
    a = _dot(h, wg_ref[...])
    g = _dot(h, wu_ref[...])
    act = (a * jax.nn.sigmoid(a) * g).astype(BF16)
    o_ref[...] += _dot(act, wd_ref[...])

    @pl.when(f == pl.num_programs(1) - 1)
    def _():
        o_ref[...] = x_ref[...] + 0.5 * _rmsnorm(o_ref[...], nwb_ref[...])


def _ffn(x, norm_w3, wg, wu, wd, l, j, tm, tf):
    m, d = x.shape
    dff = wg.shape[-1]
    return pl.pallas_call(
        _ffn_body,
        grid=(m // tm, dff // tf),
        in_specs=[
            pl.BlockSpec((tm, d), lambda i, f: (i, 0)),
            pl.BlockSpec((None, 1, d), lambda i, f: (l * 6 + 3 * j + j, 0, 0)),
            pl.BlockSpec((None, 1, d), lambda i, f: (l * 6 + 3 * j + j + 1, 0, 0)),
            pl.BlockSpec((None, None, d, tf), lambda i, f: (l, j, 0, f)),
            pl.BlockSpec((None, None, d, tf), lambda i, f: (l, j, 0, f)),
            pl.BlockSpec((None, None, tf, d), lambda i, f: (l, j, f, 0)),
        ],
        out_specs=pl.BlockSpec((tm, d), lambda i, f: (i, 0)),
        out_shape=jax.ShapeDtypeStruct((m, d), F32),
        scratch_shapes=[pltpu.VMEM((tm, d), BF16)],
        compiler_params=_cparams(("parallel", "arbitrary")),
        name=f"ffn_{l}_{j}",
    )(x, norm_w3, norm_w3, wg, wu, wd)


def _inproj_body(x_ref, nw_ref, w_ref, bias_ref, alog_ref, o_ref, h_ref, *, misc_off):
    j = pl.program_id(1)

    @pl.when(j == 0)
    def _():
        h_ref[...] = _rmsnorm(x_ref[...], nw_ref[...]).astype(BF16)

    acc = _dot(h_ref[...], w_ref[...])
    o_ref[...] = acc

    @pl.when(j == pl.num_programs(1) - 1)
    def _():
        raw = acc[:, misc_off:misc_off + MISC_W] + bias_ref[...]
        lane = lax.broadcasted_iota(jnp.int32, raw.shape, 1)
        sp = _softplus(raw)
        logf = -_softplus(-raw)
        neg_a = -jnp.exp(alog_ref[...])
        val = jnp.where(lane < 8, logf, jnp.where(lane < 16, sp, jnp.where(lane < 24, sp * neg_a, 0.0)))
        o_ref[:, misc_off:misc_off + MISC_W] = val


def _inproj(x, norm_w3, w_r, bias_vec, alog_vec, l, tm, tn, misc_off):
    m, d = x.shape
    npad = w_r.shape[-1]
    return pl.pallas_call(
        functools.partial(_inproj_body, misc_off=misc_off),
        grid=(m // tm, npad // tn),
        in_specs=[
            pl.BlockSpec((tm, d), lambda i, j: (i, 0)),
            pl.BlockSpec((None, 1, d), lambda i, j: (l * 6 + 2, 0, 0)),
            pl.BlockSpec((None, d, tn), lambda i, j: (l, 0, j)),
            pl.BlockSpec((None, 1, MISC_W), lambda i, j: (l, 0, 0)),
            pl.BlockSpec((None, 1, MISC_W), lambda i, j: (l, 0, 0)),
        ],
        out_specs=pl.BlockSpec((tm, tn), lambda i, j: (i, j)),
        out_shape=jax.ShapeDtypeStruct((m, npad), F32),
        scratch_shapes=[pltpu.VMEM((tm, d), BF16)],
        compiler_params=_cparams(("parallel", "arbitrary")),
        name=f"inproj_{l}",
    )(x, norm_w3, w_r, bias_vec, alog_vec)


def _rows_as_lane_vector(c_ref, start, count):
    rows = c_ref[pl.ds(start, count), :]
    return jnp.concatenate([rows[r:r + 1, :] for r in range(count)], axis=1)


def _split3(x):
    hi = x.astype(BF16).astype(F32)
    r = x - hi
    mid = r.astype(BF16).astype(F32)
    return hi, mid, r - mid


def _fox_prompt_body(lf_ref, q_ref, k_ref, v_ref, *rest, tq, scale, aliased):
    o_ref, ko_ref, vo_ref, c_scr, qx_scr, kx_scr, vb_scr, s_scr = rest[2 * aliased:]
    seq, dh = q_ref.shape
    nq = seq // tq
    sub = tq // LANES
    log2e = 1.4426950408889634
    c_scr[...] = _flat_cumsum(lf_ref[...]) * log2e
    ko_ref[...] = k_ref[...]
    vo_ref[...] = v_ref[...]
    vb_scr[...] = v_ref[...].astype(BF16)
    lane = lax.broadcasted_iota(jnp.int32, (tq, LANES), 1)
    for i in range(nq):
        rows = slice(i * tq, (i + 1) * tq)
        hi, mid, lo = _split3(_row_to_col(_rows_as_lane_vector(c_scr, i * sub, sub)))
        q_ext = jnp.where(lane == 0, hi, jnp.where(lane == 1, mid, jnp.where(lane == 2, lo,
                          jnp.where(lane < 6, 1.0, 0.0))))
        k_ext = jnp.where(lane < 3, 1.0, jnp.where(lane == 3, -hi, jnp.where(lane == 4, -mid,
                          jnp.where(lane == 5, -lo, 0.0))))
        qx_scr[rows, 0:dh] = (q_ref[rows, :] * (scale * log2e)).astype(BF16)
        qx_scr[rows, dh:dh + LANES] = q_ext.astype(BF16)
        kx_scr[rows, 0:dh] = k_ref[rows, :].astype(BF16)
        kx_scr[rows, dh:dh + LANES] = k_ext.astype(BF16)

    causal = lax.broadcasted_iota(jnp.int32, (tq, tq), 1) <= lax.broadcasted_iota(jnp.int32, (tq, tq), 0)
    for i in range(nq):
        qx = qx_scr[i * tq:(i + 1) * tq, :]
        m_acc = jnp.full((tq, LANES), -jnp.inf, F32)
        for j in range(i + 1):
            s = _dot_nt(qx, kx_scr[j * tq:(j + 1) * tq, :])
            if j == i:
                s = jnp.where(causal, s, -jnp.inf)
            s_scr[:, j * tq:(j + 1) * tq] = s
            for u in range(sub):
                m_acc = jnp.maximum(m_acc, s[:, u * LANES:(u + 1) * LANES])
        m = jnp.max(m_acc, axis=1, keepdims=True)
        l_acc = jnp.zeros((tq, LANES), F32)
        acc = jnp.zeros((tq, dh), F32)
        for j in range(i + 1):
            p = jnp.exp2(s_scr[:, j * tq:(j + 1) * tq] - m)
            for u in range(sub):
                l_acc = l_acc + p[:, u * LANES:(u + 1) * LANES]
            acc = acc + _dot(p.astype(BF16), vb_scr[j * tq:(j + 1) * tq, :])
        o_ref[i * tq:(i + 1) * tq, :] = (acc / jnp.sum(l_acc, axis=1, keepdims=True)).astype(o_ref.dtype)


def _fox_prompt(proj, lf_rows, kv_prev, l, depth, bsz, seq, heads, dh, col_q, col_k, col_v, tq):
    n = seq // LANES
    aliased = kv_prev is not None
    kv_shape = jax.ShapeDtypeStruct((depth, bsz * seq, heads * dh), F32)
    kv_spec = pl.BlockSpec((None, seq, dh), lambda b, h: (l, b, h))
    in_specs = [
        pl.BlockSpec((None, None, n, LANES), lambda b, h: (b, h, 0, 0)),
        pl.BlockSpec((seq, dh), lambda b, h: (b, col_q + h)),
        pl.BlockSpec((seq, dh), lambda b, h: (b, col_k + h)),
        pl.BlockSpec((seq, dh), lambda b, h: (b, col_v + h)),
    ]
    args = [lf_rows, proj, proj, proj]
    if aliased:
        in_specs += [pl.BlockSpec(memory_space=pl.ANY)] * 2
        args += list(kv_prev)
    return pl.pallas_call(
        functools.partial(_fox_prompt_body, tq=tq, scale=dh ** -0.5, aliased=aliased),
        grid=(bsz, heads),
        in_specs=in_specs,
        out_specs=[pl.BlockSpec((seq, dh), lambda b, h: (b, h)), kv_spec, kv_spec],
        out_shape=[jax.ShapeDtypeStruct((bsz * seq, heads * dh), BF16), kv_shape, kv_shape],
        scratch_shapes=[pltpu.VMEM((n, LANES), F32), pltpu.VMEM((seq, dh + LANES), BF16),
                        pltpu.VMEM((seq, dh + LANES), BF16), pltpu.VMEM((seq, dh), BF16),
                        pltpu.VMEM((tq, seq), F32)],
        input_output_aliases={4: 1, 5: 2} if aliased else {},
        compiler_params=_cparams(("parallel", "parallel")),
        name="fox_prompt",
    )(*args)


def _fox_sample_body(lf_ref, q_ref, k_ref, v_ref, ck_ref, cv_ref, *rest, past, new, heads, dh, scale, aliased):
    o_ref, ko_ref, vo_ref = rest[2 * aliased:]
    ko_ref[...] = k_ref[...]
    vo_ref[...] = v_ref[...]
    n_past = past // LANES
    causal = lax.broadcasted_iota(jnp.int32, (new, new), 1) <= lax.broadcasted_iota(jnp.int32, (new, new), 0)
    for h in range(heads):
        cols = slice(h * dh, (h + 1) * dh)
        c = _flat_cumsum(lf_ref[h])
        c_past = jnp.concatenate([c[r:r + 1, :] for r in range(n_past)], axis=1)
        c_new = c[n_past:n_past + 1, 0:new]
        cq_col = _row_to_col(c_new)
        q = q_ref[:, cols].astype(BF16)
        s_past = _dot_nt(q, ck_ref[:, cols].astype(BF16)) * scale + cq_col - c_past
        s_new = _dot_nt(q, k_ref[:, cols].astype(BF16)) * scale + cq_col - c_new
        s_new = jnp.where(causal, s_new, -jnp.inf)
        m = jnp.maximum(jnp.max(s_past, axis=1, keepdims=True), jnp.max(s_new, axis=1, keepdims=True))
        p_past = jnp.exp(s_past - m)
        p_new = jnp.exp(s_new - m)
        denom = jnp.sum(p_past, axis=1, keepdims=True) + jnp.sum(p_new, axis=1, keepdims=True)
        acc = _dot(p_past.astype(BF16), cv_ref[:, cols].astype(BF16)) + _dot(
            p_new.astype(BF16), v_ref[:, cols].astype(BF16))
        o_ref[:, cols] = (acc / denom).astype(o_ref.dtype)


def _fox_sample(proj, lf_rows, cache_k, cache_v, kv_prev, l, row0, bsz, new, heads, dh, col_q, col_k, col_v):
    depth, _, past, fw = cache_k.shape
    n = lf_rows.shape[2]
    rb0 = row0 // new
    aliased = kv_prev is not None
    kv_shape = jax.ShapeDtypeStruct((depth, bsz * new, fw), F32)
    kv_spec = pl.BlockSpec((None, new, fw), lambda b: (l, b, 0))
    in_specs = [
        pl.BlockSpec((None, heads, n, LANES), lambda b: (b, 0, 0, 0)),
        pl.BlockSpec((new, fw), lambda b: (rb0 + b, col_q)),
        pl.BlockSpec((new, fw), lambda b: (rb0 + b, col_k)),
        pl.BlockSpec((new, fw), lambda b: (rb0 + b, col_v)),
        pl.BlockSpec((None, None, past, fw), lambda b: (l, b, 0, 0)),
        pl.BlockSpec((None, None, past, fw), lambda b: (l, b, 0, 0)),
    ]
    args = [lf_rows, proj, proj, proj, cache_k, cache_v]
    if aliased:
        in_specs += [pl.BlockSpec(memory_space=pl.ANY)] * 2
        args += list(kv_prev)
    return pl.pallas_call(
        functools.partial(_fox_sample_body, past=past, new=new, heads=heads, dh=dh, scale=dh ** -0.5,
                          aliased=aliased),
        grid=(bsz,),
        in_specs=in_specs,
        out_specs=[pl.BlockSpec((new, fw), lambda b: (b, 0)), kv_spec, kv_spec],
        out_shape=[jax.ShapeDtypeStruct((bsz * new, fw), BF16), kv_shape, kv_shape],
        input_output_aliases={6: 1, 7: 2} if aliased else {},
        compiler_params=_cparams(("parallel",)),
        name="fox_sample",
    )(*args)


def _gmlp_body(gu_ref, gv_ref, lnw_ref, lnb_ref, wp_ref, ws_ref, bp_ref, bs_ref, o_ref, v_ref, *,
               n_prompt_tiles, groups, gch):
    i = pl.program_id(0)
    is_sample = i >= n_prompt_tiles
    u = jax.nn.gelu(gu_ref[...])
    gv = jax.nn.gelu(gv_ref[...])
    mu = jnp.mean(gv, axis=-1, keepdims=True)
    var = jnp.mean(jnp.square(gv - mu), axis=-1, keepdims=True)
    v = (gv - mu) * lax.rsqrt(var + EPS) * lnw_ref[...] + lnb_ref[...]

    @pl.when(is_sample)
    def _():
        v_ref[...] = v

    tm = u.shape[0]
    tril = lax.broadcasted_iota(jnp.int32, (GM_CHUNK, GM_CHUNK), 1) <= lax.broadcasted_iota(
        jnp.int32, (GM_CHUNK, GM_CHUNK), 0)
    bias_all = jnp.where(is_sample, bs_ref[...], bp_ref[...])
    for g in range(groups):
        wm = jnp.where(is_sample, ws_ref[g], wp_ref[g])
        wm = jnp.where(tril, wm, 0.0).astype(BF16)
        bias = bias_all[:, g:g + 1]
        for c in range(tm // GM_CHUNK):
            rows = slice(c * GM_CHUNK, (c + 1) * GM_CHUNK)
            cols = slice(g * gch, (g + 1) * gch)
            sp = _dot(wm, v[rows, cols].astype(BF16)) + bias
            o_ref[rows, cols] = (u[rows, cols] * sp).astype(o_ref.dtype)


def _gmlp(proj, lnw, lnb, w_prompt, w_sample, b_prompt, b_sample, l, n_prompt_rows, groups, gch,
          col_u, col_v, tm):
    m = proj.shape[0]
    gw = groups * gch
    npt = n_prompt_rows // tm
    return pl.pallas_call(
        functools.partial(_gmlp_body, n_prompt_tiles=npt, groups=groups, gch=gch),
        grid=(m // tm,),
        in_specs=[
            pl.BlockSpec((tm, gw), lambda i: (i, col_u)),
            pl.BlockSpec((tm, gw), lambda i: (i, col_v)),
            pl.BlockSpec((None, 1, gw), lambda i: (l, 0, 0)),
            pl.BlockSpec((None, 1, gw), lambda i: (l, 0, 0)),
            pl.BlockSpec((None, groups, GM_CHUNK, GM_CHUNK), lambda i: (l, 0, 0, 0)),
            pl.BlockSpec((None, groups, GM_CHUNK, GM_CHUNK), lambda i: (l, 0, 0, 0)),
            pl.BlockSpec((None, GM_CHUNK, groups), lambda i: (l, 0, 0)),
            pl.BlockSpec((None, GM_CHUNK, groups), lambda i: (l, 0, 0)),
        ],
        out_specs=[
            pl.BlockSpec((tm, gw), lambda i: (i, 0)),
            pl.BlockSpec((tm, gw), lambda i: (jnp.maximum(i - npt, 0), 0)),
        ],
        out_shape=[jax.ShapeDtypeStruct((m, gw), BF16), jax.ShapeDtypeStruct((m - n_prompt_rows, gw), F32)],
        compiler_params=_cparams(("arbitrary",)),
        name="gmlp",
    )(proj, proj, lnw, lnb, w_prompt, w_sample, b_prompt, b_sample)


def _ssd_body(xbc_ref, z_ref, misc_ref, dtrow_ref, arow_ref, prev_ref, s0_ref, cw_ref, cb_ref, dvec_ref,
              nw_ref, o_ref, conv_ref, st_ref, xp_scr, s_scr, y_scr, *, lt, q, heads, hp, ngroups, sn, conv_k):
    t = pl.program_id(1)
    nt = pl.num_programs(1)
    xw = heads * hp
    pad = 8

    @pl.when(t == 0)
    def _():
        xp_scr[0:pad, :] = prev_ref[...]
        s_scr[...] = s0_ref[...]

    xbc = xbc_ref[...]
    xp_scr[pad:pad + lt, :] = xbc
    acc = cb_ref[...]
    for i in range(conv_k):
        off = pad - (conv_k - 1) + i
        acc = acc + xp_scr[off:off + lt, :] * cw_ref[i:i + 1, :]
    xp_scr[0:pad, :] = xbc[lt - pad:lt, :]
    act = acc * jax.nn.sigmoid(acc)

    @pl.when(t == nt - 1)
    def _():
        conv_ref[...] = xbc[lt - (conv_k - 1):lt, :]

    misc = misc_ref[...]
    dt_rows = dtrow_ref[...]
    ltp = dt_rows.shape[1]
    a_rows = jnp.concatenate(
        [_lane_cumsum(arow_ref[:, j * LANES:(j + 1) * LANES], q) for j in range(ltp // LANES)], axis=1)
    tril = lax.broadcasted_iota(jnp.int32, (q, q), 1) <= lax.broadcasted_iota(jnp.int32, (q, q), 0)
    rep = heads // ngroups
    for c in range(lt // q):
        r0 = c * q
        bmat = [act[r0:r0 + q, xw + g * sn: xw + (g + 1) * sn].astype(BF16) for g in range(ngroups)]
        cmat = [act[r0:r0 + q, xw + (ngroups + g) * sn: xw + (ngroups + g + 1) * sn].astype(BF16)
                for g in range(ngroups)]
        gmat = [_dot_nt(cmat[g], bmat[g]) for g in range(ngroups)]
        for h in range(heads):
            g = h // rep
            a_row = a_rows[h:h + 1, r0:r0 + q]
            dt_row = dt_rows[h:h + 1, r0:r0 + q]
            a_col = _row_to_col(a_row)
            dt_col = misc[r0:r0 + q, 8 + h:9 + h]
            a_last = a_row[:, q - 1:q]
            lmat = jnp.exp(jnp.where(tril, a_col - a_row, -jnp.inf))
            scores = gmat[g] * lmat * dt_row
            xh = act[r0:r0 + q, h * hp:(h + 1) * hp]
            y = _dot(scores.astype(BF16), xh.astype(BF16))
            decay_end = jnp.exp(a_last - a_col) * dt_col
            chunk_state = _dot_tn((xh * decay_end).astype(BF16), bmat[g])
            s_in = s_scr[h]
            y = y + _dot_nt(cmat[g], s_in.astype(BF16)) * jnp.exp(a_col)
            s_scr[h] = s_in * jnp.exp(a_last) + chunk_state
            y_scr[r0:r0 + q, h * hp:(h + 1) * hp] = y + dvec_ref[:, h * hp:(h + 1) * hp] * xh

    zz = z_ref[...]
    o_ref[...] = _rmsnorm(y_scr[...] * (zz * jax.nn.sigmoid(zz)), nw_ref[...]).astype(o_ref.dtype)

    @pl.when(t == nt - 1)
    def _():
        st_ref[...] = s_scr[...]


def _ssd(proj, dt_rows, a_rows, conv_prev, s0, conv_w, conv_b, dvec, norm_w, l, row0, bsz, seq, lt, q,
         heads, hp, ngroups, sn, col_xbc, col_z, col_misc):
    nt = seq // lt
    xw = heads * hp
    cch = xw + 2 * ngroups * sn
    conv_k = conv_w.shape[1]
    rb0 = row0 // lt
    ltp = max(lt, LANES)
    body = functools.partial(_ssd_body, lt=lt, q=q, heads=heads, hp=hp, ngroups=ngroups, sn=sn, conv_k=conv_k)
    return pl.pallas_call(
        body,
        grid=(bsz, nt),
        in_specs=[
            pl.BlockSpec((lt, cch), lambda b, t: (rb0 + b * nt + t, col_xbc)),
            pl.BlockSpec((lt, xw), lambda b, t: (rb0 + b * nt + t, col_z)),
            pl.BlockSpec((lt, MISC_W), lambda b, t: (rb0 + b * nt + t, col_misc)),
            pl.BlockSpec((None, heads, ltp), lambda b, t: (b, 0, t)),
            pl.BlockSpec((None, heads, ltp), lambda b, t: (b, 0, t)),
            pl.BlockSpec((None, 8, cch), lambda b, t: (b, 0, 0)),
            pl.BlockSpec((None, heads, hp, sn), lambda b, t: (b, 0, 0, 0)),
            pl.BlockSpec((None, conv_k, cch), lambda b, t: (l, 0, 0)),
            pl.BlockSpec((None, 1, cch), lambda b, t: (l, 0, 0)),
            pl.BlockSpec((None, 1, xw), lambda b, t: (l, 0, 0)),
            pl.BlockSpec((None, 1, xw), lambda b, t: (l, 0, 0)),
        ],
        out_specs=[
            pl.BlockSpec((lt, xw), lambda b, t: (b * nt + t, 0)),
            pl.BlockSpec((None, conv_k - 1, cch), lambda b, t: (b, 0, 0)),
            pl.BlockSpec((None, heads, hp, sn), lambda b, t: (b, 0, 0, 0)),
        ],
        out_shape=[
            jax.ShapeDtypeStruct((bsz * seq, xw), BF16),
            jax.ShapeDtypeStruct((bsz, conv_k - 1, cch), F32),
            jax.ShapeDtypeStruct((bsz, heads, hp, sn), F32),
        ],
        scratch_shapes=[pltpu.VMEM((lt + 8, cch), F32), pltpu.VMEM((heads, hp, sn), F32),
                        pltpu.VMEM((lt, xw), F32)],
        compiler_params=_cparams(("parallel", "arbitrary")),
        name="ssd",
    )(proj, proj, proj, dt_rows, a_rows, conv_prev, s0, conv_w, conv_b, dvec, norm_w)


def _outproj_body(fox_ref, gm_ref, ssd_ref, x_ref, nw_ref, w_ref, o_ref, *, fw, gw):
    o = _dot(fox_ref[...], w_ref[0:fw, :])
    o = o + _dot(gm_ref[...], w_ref[fw:fw + gw, :])
    o = o + _dot(ssd_ref[...], w_ref[fw + gw:, :])
    o_ref[...] = x_ref[...] + _rmsnorm(o, nw_ref[...])


def _outproj(fox_o, gm_o, ssd_o, x, norm_w3, w_out, l, tm):
    m, d = x.shape
    fw, gw, sw = fox_o.shape[1], gm_o.shape[1], ssd_o.shape[1]
    return pl.pallas_call(
        functools.partial(_outproj_body, fw=fw, gw=gw),
        grid=(m // tm,),
        in_specs=[
            pl.BlockSpec((tm, fw), lambda i: (i, 0)),
            pl.BlockSpec((tm, gw), lambda i: (i, 0)),
            pl.BlockSpec((tm, sw), lambda i: (i, 0)),
            pl.BlockSpec((tm, d), lambda i: (i, 0)),
            pl.BlockSpec((None, 1, d), lambda i: (l * 6 + 3, 0, 0)),
            pl.BlockSpec((None, fw + gw + sw, d), lambda i: (l, 0, 0)),
        ],
        out_specs=pl.BlockSpec((tm, d), lambda i: (i, 0)),
        out_shape=jax.ShapeDtypeStruct((m, d), F32),
        compiler_params=_cparams(("parallel",)),
        name=f"outproj_{l}",
    )(fox_o, gm_o, ssd_o, x, norm_w3, w_out)


def _head_rows(vals, bsz, seq):
    return jnp.transpose(vals.reshape(bsz, seq, vals.shape[-1]), (0, 2, 1))


def _pad_lanes(rows, width):
    return jnp.pad(rows, ((0, 0), (0, 0), (0, width - rows.shape[-1])))


def kernel(x_prompt, x_sample, cache_fox_k, cache_fox_v, cache_fox_logf, state_ssd_conv, state_ssd, norm_w,
           w_ffn_gate, w_ffn_up, w_ffn_down, w_in, fox_fb, gm_ln_w, gm_ln_b, gm_ws, gm_bs, ssd_conv_w,
           ssd_conv_b, ssd_dt_bias, ssd_a_log, ssd_d, ssd_norm_w, w_out):
    bsz, seq, d = x_prompt.shape
    dbsz, dseq, _ = x_sample.shape
    depth, _, past, heads, dh = cache_fox_k.shape
    fox_w = heads * dh
    groups, gchunk = gm_ws.shape[1], gm_ws.shape[2]
    gm_w = gm_ln_w.shape[1]
    gch = gm_w // groups
    _, _, sheads, hp, sn = state_ssd.shape
    ssd_w = sheads * hp
    cch = ssd_conv_w.shape[2]
    ngroups = (cch - ssd_w) // (2 * sn)
    conv_k = ssd_conv_w.shape[1]
    n_p, n_s = bsz * seq, dbsz * dseq
    m = n_p + n_s
    assert gchunk == GM_CHUNK and heads == 8 and sheads == 8 and dh == LANES
    assert seq % GM_CHUNK == 0 and GM_CHUNK % dseq == 0 and dseq < SSD_CHUNK and past % LANES == 0

    wg, wu, wd = (w.astype(BF16) for w in (w_ffn_gate, w_ffn_up, w_ffn_down))
    wo = w_out.astype(BF16)
    sizes = [fox_w, fox_w, fox_w, heads, gm_w, gm_w, ssd_w, cch, sheads]
    offs = [0]
    for s in sizes:
        offs.append(offs[-1] + s)
    wq, wk, wv, wf, wgu, wgv, wz, wxbc, wdt = (w_in[:, :, offs[i]:offs[i + 1]] for i in range(9))
    tn = 1024
    used = 3 * fox_w + 2 * gm_w + cch + ssd_w + MISC_W
    npad = -(-used // tn) * tn
    zeros_misc = jnp.zeros((depth, d, MISC_W - heads - 2 * sheads), F32)
    zeros_tail = jnp.zeros((depth, d, npad - used), F32)
    w_r = jnp.concatenate([wq, wk, wv, wgu, wgv, wxbc, wz, wf, wdt, wdt, zeros_misc, zeros_tail],
                          axis=-1).astype(BF16)
    col_q, col_k, col_v = 0, fox_w // dh, 2 * fox_w // dh
    col_u, col_gv = 3 * fox_w // gm_w, 3 * fox_w // gm_w + 1
    off_xbc = 3 * fox_w + 2 * gm_w
    off_z = off_xbc + cch
    off_misc = off_z + ssd_w
    assert off_xbc % cch == 0 and off_z % ssd_w == 0 and off_misc % MISC_W == 0
    misc_in_tile = off_misc % tn
    assert off_misc // tn == npad // tn - 1 and misc_in_tile + MISC_W <= tn

    lane_pad = jnp.zeros((depth, MISC_W - heads - 2 * sheads), F32)
    bias_vec = jnp.concatenate([fox_fb, ssd_dt_bias, ssd_dt_bias, lane_pad], axis=-1)[:, None, :]
    alog_vec = jnp.concatenate([jnp.zeros((depth, heads + sheads), F32), ssd_a_log, lane_pad], axis=-1)[:, None, :]
    norm_w3 = norm_w.reshape(depth * 6, 1, d)
    lnw3, lnb3 = gm_ln_w[:, None, :], gm_ln_b[:, None, :]
    reps = GM_CHUNK // dseq
    eye = jnp.eye(reps, dtype=F32)
    ws_s = jnp.einsum('ab,lgts->lgatbs', eye, gm_ws[:, :, :dseq, :dseq]).reshape(depth, groups, GM_CHUNK, GM_CHUNK)
    bs_p = jnp.transpose(gm_bs, (0, 2, 1))
    bs_s = jnp.tile(jnp.transpose(gm_bs[:, :, :dseq], (0, 2, 1)), (1, reps, 1))
    conv_b3 = ssd_conv_b[:, None, :]
    dvec = jnp.repeat(ssd_d, hp, axis=-1)[:, None, :]
    snw3 = ssd_norm_w[:, None, :]
    cache_k = cache_fox_k.reshape(depth, dbsz, past, fox_w)
    cache_v = cache_fox_v.reshape(depth, dbsz, past, fox_w)
    conv_prev_p = jnp.zeros((bsz, 8, cch), F32)
    conv_prev_s = jnp.pad(state_ssd_conv, ((0, 0), (0, 0), (8 - (conv_k - 1), 0), (0, 0)))
    s0_p = jnp.zeros((bsz, sheads, hp, sn), F32)

    x = jnp.concatenate([x_prompt.reshape(n_p, d), x_sample.reshape(n_s, d)], axis=0)
    tm_big = m // 8 if m % 64 == 0 and (m // 8) % 16 == 0 else 256
    tm_ffn = m // 12 if m % 12 == 0 and (m // 12) % 16 == 0 else 256
    tm_out = m // 16 if m % 16 == 0 and (m // 16) % 16 == 0 else 256
    lf_len = 2 * past
    assert past + dseq <= lf_len

    outs = {k: [] for k in ("plf", "pconv", "pst", "slf", "sgm", "sconv", "sst")}
    kv_p = kv_s = None
    for l in range(depth):
        x = _ffn(x, norm_w3, wg, wu, wd, l, 0, tm_ffn, 512)
        proj = _inproj(x, norm_w3, w_r, bias_vec, alog_vec, l, tm_big, tn, misc_in_tile)
        misc = proj[:, off_misc:off_misc + heads + 2 * sheads]
        logf_p, logf_s = misc[:n_p, :heads], misc[n_p:, :heads]
        lf_rows_p = _head_rows(logf_p, bsz, seq).reshape(bsz, heads, seq // LANES, LANES)
        lf_all = jnp.concatenate([cache_fox_logf[l], logf_s.reshape(dbsz, dseq, heads),
                                  jnp.zeros((dbsz, lf_len - past - dseq, heads), F32)], axis=1)
        lf_rows_s = jnp.transpose(lf_all, (0, 2, 1)).reshape(dbsz, heads, lf_len // LANES, LANES)
        dt_rows_p = _head_rows(misc[:n_p, heads:heads + sheads], bsz, seq)
        a_rows_p = _head_rows(misc[:n_p, heads + sheads:], bsz, seq)
        dt_rows_s = _pad_lanes(_head_rows(misc[n_p:, heads:heads + sheads], dbsz, dseq), LANES)
        a_rows_s = _pad_lanes(_head_rows(misc[n_p:, heads + sheads:], dbsz, dseq), LANES)

        fox_p, *kv_p = _fox_prompt(proj, lf_rows_p, kv_p, l, depth, bsz, seq, heads, dh, col_q, col_k, col_v, 256)
        fox_s, *kv_s = _fox_sample(proj, lf_rows_s, cache_k, cache_v, kv_s, l, n_p, dbsz, dseq, heads, dh, 0, 1, 2)
        gm_o, gm_v_s = _gmlp(proj, lnw3, lnb3, gm_ws, ws_s, bs_p, bs_s, l, n_p, groups, gch, col_u, col_gv, 256)
        ssd_p, conv_p, st_p = _ssd(proj, dt_rows_p, a_rows_p, conv_prev_p, s0_p, ssd_conv_w, conv_b3, dvec, snw3,
                                   l, 0, bsz, seq, 256, SSD_CHUNK, sheads, hp, ngroups, sn,
                                   off_xbc // cch, off_z // ssd_w, off_misc // MISC_W)
        ssd_s, conv_s, st_s = _ssd(proj, dt_rows_s, a_rows_s, conv_prev_s[l], state_ssd[l], ssd_conv_w, conv_b3,
                                   dvec, snw3, l, n_p, dbsz, dseq, dseq, dseq, sheads, hp, ngroups, sn,
                                   off_xbc // cch, off_z // ssd_w, off_misc // MISC_W)
        fox_o = jnp.concatenate([fox_p, fox_s], axis=0)
        ssd_o = jnp.concatenate([ssd_p, ssd_s], axis=0)
        x = _outproj(fox_o, gm_o, ssd_o, x, norm_w3, wo, l, tm_out)
        x = _ffn(x, norm_w3, wg, wu, wd, l, 1, tm_ffn, 512)

        outs["plf"].append(logf_p.reshape(bsz, seq, heads))
        outs["pconv"].append(conv_p)
        outs["pst"].append(st_p)
        outs["slf"].append(logf_s.reshape(dbsz, dseq, heads))
        outs["sgm"].append(gm_v_s.reshape(dbsz, dseq, gm_w))
        outs["sconv"].append(conv_s)
        outs["sst"].append(st_s)

    st = {k: jnp.stack(v) for k, v in outs.items()}
    y_prompt = x[:n_p].reshape(bsz, seq, d)
    y_sample = x[n_p:].reshape(dbsz, dseq, d)
    p_k, p_v = (a.reshape(depth, bsz, seq, heads, dh) for a in kv_p)
    s_k, s_v = (a.reshape(depth, dbsz, dseq, heads, dh) for a in kv_s)
    return (y_prompt, y_sample, p_k, p_v, st["plf"], st["pconv"], st["pst"],
            s_k, s_v, st["slf"], st["sgm"], st["sconv"], st["sst"])
```

```python
import functools

import jax
import jax.numpy as jnp
from jax import lax
from jax.experimental import pallas as pl
from jax.experimental.pallas import tpu as pltpu

F32 = jnp.float32
BF16 = jnp.bfloat16
EPS = 1e-6
LANES = 128
SSD_CHUNK = 64
GM_CHUNK = 128
VMEM_LIMIT = 56 * 1024 * 1024

MISC_W = LANES


def _cparams(sem):
    return pltpu.CompilerParams(dimension_semantics=sem, vmem_limit_bytes=VMEM_LIMIT)


def _rmsnorm(x, w):
    return x * lax.rsqrt(jnp.mean(x * x, axis=-1, keepdims=True) + EPS) * w


def _softplus(x):
    return jnp.maximum(x, 0.0) + jnp.log1p(jnp.exp(-jnp.abs(x)))


def _dot(a, b):
    return jnp.dot(a, b, preferred_element_type=F32)


def _dot_nt(a, b):
    return lax.dot_general(a, b, (((1,), (1,)), ((), ())), preferred_element_type=F32)


def _dot_tn(a, b):
    return lax.dot_general(a, b, (((0,), (0,)), ((), ())), preferred_element_type=F32)


def _row_to_col(row):
    n = row.shape[1]
    eye = lax.broadcasted_iota(jnp.int32, (n, n), 0) == lax.broadcasted_iota(jnp.int32, (n, n), 1)
    return jnp.sum(jnp.where(eye, jnp.broadcast_to(row, (n, n)), 0.0), axis=1, keepdims=True)


def _lane_cumsum(x, seg):
    lane = lax.broadcasted_iota(jnp.int32, x.shape, 1) & (seg - 1)
    k = 1
    while k < seg:
        x = x + jnp.where(lane >= k, pltpu.roll(x, k, axis=1), 0.0)
        k *= 2
    return x


def _flat_cumsum(x):
    x = _lane_cumsum(x, LANES)
    tot = x[:, LANES - 1:LANES]
    rows = []
    carry = jnp.zeros((1, 1), F32)
    for r in range(x.shape[0]):
        rows.append(x[r:r + 1, :] + carry)
        carry = carry + tot[r:r + 1, :]
    return jnp.concatenate(rows, axis=0)


def _ffn_body(x_ref, nwa_ref, nwb_ref, wg_ref, wu_ref, wd_ref, o_ref, h_ref):
    f = pl.program_id(1)

    @pl.when(f == 0)
    def _():
        h_ref[...] = _rmsnorm(x_ref[...], nwa_ref[...]).astype(BF16)
        o_ref[...] = jnp.zeros_like(o_ref)

    h = h_ref[...]
    a = _dot(h, wg_ref[...].astype(BF16))
    g = _dot(h, wu_ref[...].astype(BF16))
    act = (a * jax.nn.sigmoid(a) * g).astype(BF16)
    o_ref[...] += _dot(act, wd_ref[...].astype(BF16))

    @pl.when(f == pl.num_programs(1) - 1)
    def _():
        o_ref[...] = x_ref[...] + 0.5 * _rmsnorm(o_ref[...], nwb_ref[...])


def _ffn(x, norm_w3, wg, wu, wd, l, j, tm, tf):
    m, d = x.shape
    dff = wg.shape[-1]
    return pl.pallas_call(
        _ffn_body,
        grid=(m // tm, dff // tf),
        in_specs=[
            pl.BlockSpec((tm, d), lambda i, f: (i, 0), pipeline_mode=pl.Buffered(1)),
            pl.BlockSpec((None, 1, d), lambda i, f: (l * 6 + 3 * j + j, 0, 0)),
            pl.BlockSpec((None, 1, d), lambda i, f: (l * 6 + 3 * j + j + 1, 0, 0)),
            pl.BlockSpec((None, None, d, tf), lambda i, f: (l, j, 0, f)),
            pl.BlockSpec((None, None, d, tf), lambda i, f: (l, j, 0, f)),
            pl.BlockSpec((None, None, tf, d), lambda i, f: (l, j, f, 0)),
        ],
        out_specs=pl.BlockSpec((tm, d), lambda i, f: (i, 0)),
        out_shape=jax.ShapeDtypeStruct((m, d), F32),
        scratch_shapes=[pltpu.VMEM((tm, d), BF16)],
        compiler_params=_cparams(("parallel", "arbitrary")),
        name=f"ffn_{l}_{j}",
    )(x, norm_w3, norm_w3, wg, wu, wd)


def _inproj_body(x_ref, nw_ref, w_ref, bias_ref, alog_ref, o_ref, h_ref, *, misc_off):
    j = pl.program_id(1)

    @pl.when(j == 0)
    def _():
        h_ref[...] = _rmsnorm(x_ref[...], nw_ref[...]).astype(BF16)

    acc = _dot(h_ref[...], w_ref[...])
    o_ref[...] = acc

    @pl.when(j == pl.num_programs(1) - 1)
    def _():
        raw = acc[:, misc_off:misc_off + MISC_W] + bias_ref[...]
        lane = lax.broadcasted_iota(jnp.int32, raw.shape, 1)
        sp = _softplus(raw)
        logf = -_softplus(-raw)
        neg_a = -jnp.exp(alog_ref[...])
        val = jnp.where(lane < 8, logf, jnp.where(lane < 16, sp, jnp.where(lane < 24, sp * neg_a, 0.0)))
        o_ref[:, misc_off:misc_off + MISC_W] = val


def _inproj(x, norm_w3, w_r, bias_vec, alog_vec, l, tm, tn, misc_off):
    m, d = x.shape
    npad = w_r.shape[-1]
    return pl.pallas_call(
        functools.partial(_inproj_body, misc_off=misc_off),
        grid=(m // tm, npad // tn),
        in_specs=[
            pl.BlockSpec((tm, d), lambda i, j: (i, 0)),
            pl.BlockSpec((None, 1, d), lambda i, j: (l * 6 + 2, 0, 0)),
            pl.BlockSpec((None, d, tn), lambda i, j: (l, 0, j)),
            pl.BlockSpec((None, 1, MISC_W), lambda i, j: (l, 0, 0)),
            pl.BlockSpec((None, 1, MISC_W), lambda i, j: (l, 0, 0)),
        ],
        out_specs=pl.BlockSpec((tm, tn), lambda i, j: (i, j)),
        out_shape=jax.ShapeDtypeStruct((m, npad), F32),
        scratch_shapes=[pltpu.VMEM((tm, d), BF16)],
        compiler_params=_cparams(("parallel", "arbitrary")),
        name=f"inproj_{l}",
    )(x, norm_w3, w_r, bias_vec, alog_vec)


def _rows_as_lane_vector(c_ref, start, count):
    rows = c_ref[pl.ds(start, count), :]
    return jnp.concatenate([rows[r:r + 1, :] for r in range(count)], axis=1)


def _split3(x):
    hi = x.astype(BF16).astype(F32)
    r = x - hi
    mid = r.astype(BF16).astype(F32)
    return hi, mid, r - mid


def _fox_prompt_body(lf_ref, q_ref, k_ref, v_ref, *rest, tq, scale, aliased):
    o_ref, ko_ref, vo_ref, c_scr, qx_scr, kx_scr, vb_scr, s_scr = rest[2 * aliased:]
    seq, dh = q_ref.shape
    nq = seq // tq
    sub = tq // LANES
    log2e = 1.4426950408889634
    c_scr[...] = _flat_cumsum(lf_ref[...]) * log2e
    ko_ref[...] = k_ref[...]
    vo_ref[...] = v_ref[...]
    vb_scr[...] = v_ref[...].astype(BF16)
    lane = lax.broadcasted_iota(jnp.int32, (tq, LANES), 1)
    for i in range(nq):
        rows = slice(i * tq, (i + 1) * tq)
        hi, mid, lo = _split3(_row_to_col(_rows_as_lane_vector(c_scr, i * sub, sub)))
        q_ext = jnp.where(lane == 0, hi, jnp.where(lane == 1, mid, jnp.where(lane == 2, lo,
                          jnp.where(lane < 6, 1.0, 0.0))))
        k_ext = jnp.where(lane < 3, 1.0, jnp.where(lane == 3, -hi, jnp.where(lane == 4, -mid,
                          jnp.where(lane == 5, -lo, 0.0))))
        qx_scr[rows, 0:dh] = (q_ref[rows, :] * (scale * log2e)).astype(BF16)
        qx_scr[rows, dh:dh + LANES] = q_ext.astype(BF16)
        kx_scr[rows, 0:dh] = k_ref[rows, :].astype(BF16)
        kx_scr[rows, dh:dh + LANES] = k_ext.astype(BF16)

    causal = lax.broadcasted_iota(jnp.int32, (tq, tq), 1) <= lax.broadcasted_iota(jnp.int32, (tq, tq), 0)
    for i in range(nq):
        qx = qx_scr[i * tq:(i + 1) * tq, :]
        m_acc = jnp.full((tq, LANES), -jnp.inf, F32)
        for j in range(i + 1):
            s = _dot_nt(qx, kx_scr[j * tq:(j + 1) * tq, :])
            if j == i:
                s = jnp.where(causal, s, -jnp.inf)
            s_scr[:, j * tq:(j + 1) * tq] = s
            for u in range(sub):
                m_acc = jnp.maximum(m_acc, s[:, u * LANES:(u + 1) * LANES])
        m = jnp.max(m_acc, axis=1, keepdims=True)
        l_acc = jnp.zeros((tq, LANES), F32)
        acc = jnp.zeros((tq, dh), F32)
        for j in range(i + 1):
            p = jnp.exp2(s_scr[:, j * tq:(j + 1) * tq] - m)
            for u in range(sub):
                l_acc = l_acc + p[:, u * LANES:(u + 1) * LANES]
            acc = acc + _dot(p.astype(BF16), vb_scr[j * tq:(j + 1) * tq, :])
        o_ref[i * tq:(i + 1) * tq, :] = (acc / jnp.sum(l_acc, axis=1, keepdims=True)).astype(o_ref.dtype)


def _fox_prompt(proj, lf_rows, kv_prev, l, depth, bsz, seq, heads, dh, col_q, col_k, col_v, tq):
    n = seq // LANES
    aliased = kv_prev is not None
    kv_shape = jax.ShapeDtypeStruct((depth, bsz * seq, heads * dh), F32)
    kv_spec = pl.BlockSpec((None, seq, dh), lambda b, h: (l, b, h))
    in_specs = [
        pl.BlockSpec((None, None, n, LANES), lambda b, h: (b, h, 0, 0)),
        pl.BlockSpec((seq, dh), lambda b, h: (b, col_q + h)),
        pl.BlockSpec((seq, dh), lambda b, h: (b, col_k + h)),
        pl.BlockSpec((seq, dh), lambda b, h: (b, col_v + h)),
    ]
    args = [lf_rows, proj, proj, proj]
    if aliased:
        in_specs += [pl.BlockSpec(memory_space=pl.ANY)] * 2
        args += list(kv_prev)
    return pl.pallas_call(
        functools.partial(_fox_prompt_body, tq=tq, scale=dh ** -0.5, aliased=aliased),
        grid=(bsz, heads),
        in_specs=in_specs,
        out_specs=[pl.BlockSpec((seq, dh), lambda b, h: (b, h)), kv_spec, kv_spec],
        out_shape=[jax.ShapeDtypeStruct((bsz * seq, heads * dh), BF16), kv_shape, kv_shape],
        scratch_shapes=[pltpu.VMEM((n, LANES), F32), pltpu.VMEM((seq, dh + LANES), BF16),
                        pltpu.VMEM((seq, dh + LANES), BF16), pltpu.VMEM((seq, dh), BF16),
                        pltpu.VMEM((tq, seq), F32)],
        input_output_aliases={4: 1, 5: 2} if aliased else {},
        compiler_params=_cparams(("parallel", "parallel")),
        name="fox_prompt",
    )(*args)


def _fox_sample_body(lf_ref, q_ref, k_ref, v_ref, ck_ref, cv_ref, *rest, past, new, heads, dh, scale, aliased):
    o_ref, ko_ref, vo_ref = rest[2 * aliased:]
    ko_ref[...] = k_ref[...]
    vo_ref[...] = v_ref[...]
    n_past = past // LANES
    causal = lax.broadcasted_iota(jnp.int32, (new, new), 1) <= lax.broadcasted_iota(jnp.int32, (new, new), 0)
    for h in range(heads):
        cols = slice(h * dh, (h + 1) * dh)
        c = _flat_cumsum(lf_ref[h])
        c_past = jnp.concatenate([c[r:r + 1, :] for r in range(n_past)], axis=1)
        c_new = c[n_past:n_past + 1, 0:new]
        cq_col = _row_to_col(c_new)
        q = q_ref[:, cols].astype(BF16)
        s_past = _dot_nt(q, ck_ref[:, cols].astype(BF16)) * scale + cq_col - c_past
        s_new = _dot_nt(q, k_ref[:, cols].astype(BF16)) * scale + cq_col - c_new
        s_new = jnp.where(causal, s_new, -jnp.inf)
        m = jnp.maximum(jnp.max(s_past, axis=1, keepdims=True), jnp.max(s_new, axis=1, keepdims=True))
        p_past = jnp.exp(s_past - m)
        p_new = jnp.exp(s_new - m)
        denom = jnp.sum(p_past, axis=1, keepdims=True) + jnp.sum(p_new, axis=1, keepdims=True)
        acc = _dot(p_past.astype(BF16), cv_ref[:, cols].astype(BF16)) + _dot(
            p_new.astype(BF16), v_ref[:, cols].astype(BF16))
        o_ref[:, cols] = (acc / denom).astype(o_ref.dtype)


def _fox_sample(proj, lf_rows, cache_k, cache_v, kv_prev, l, row0, bsz, new, heads, dh, col_q, col_k, col_v):
    depth, _, past, fw = cache_k.shape
    n = lf_rows.shape[2]
    rb0 = row0 // new
    aliased = kv_prev is not None
    kv_shape = jax.ShapeDtypeStruct((depth, bsz * new, fw), F32)
    kv_spec = pl.BlockSpec((None, new, fw), lambda b: (l, b, 0))
    in_specs = [
        pl.BlockSpec((None, heads, n, LANES), lambda b: (b, 0, 0, 0)),
        pl.BlockSpec((new, fw), lambda b: (rb0 + b, col_q)),
        pl.BlockSpec((new, fw), lambda b: (rb0 + b, col_k)),
        pl.BlockSpec((new, fw), lambda b: (rb0 + b, col_v)),
        pl.BlockSpec((None, None, past, fw), lambda b: (l, b, 0, 0)),
        pl.BlockSpec((None, None, past, fw), lambda b: (l, b, 0, 0)),
    ]
    args = [lf_rows, proj, proj, proj, cache_k, cache_v]
    if aliased:
        in_specs += [pl.BlockSpec(memory_space=pl.ANY)] * 2
        args += list(kv_prev)
    return pl.pallas_call(
        functools.partial(_fox_sample_body, past=past, new=new, heads=heads, dh=dh, scale=dh ** -0.5,
                          aliased=aliased),
        grid=(bsz,),
        in_specs=in_specs,
        out_specs=[pl.BlockSpec((new, fw), lambda b: (b, 0)), kv_spec, kv_spec],
        out_shape=[jax.ShapeDtypeStruct((bsz * new, fw), BF16), kv_shape, kv_shape],
        input_output_aliases={6: 1, 7: 2} if aliased else {},
        compiler_params=_cparams(("parallel",)),
        name="fox_sample",
    )(*args)


def _gmlp_body(gu_ref, gv_ref, lnw_ref, lnb_ref, wp_ref, ws_ref, bp_ref, bs_ref, o_ref, v_ref, *,
               n_prompt_tiles, groups, gch):
    i = pl.program_id(0)
    is_sample = i >= n_prompt_tiles
    u = jax.nn.gelu(gu_ref[...])
    gv = jax.nn.gelu(gv_ref[...])
    mu = jnp.mean(gv, axis=-1, keepdims=True)
    var = jnp.mean(jnp.square(gv - mu), axis=-1, keepdims=True)
    v = (gv - mu) * lax.rsqrt(var + EPS) * lnw_ref[...] + lnb_ref[...]

    @pl.when(is_sample)
    def _():
        v_ref[...] = v

    tm = u.shape[0]
    tril = lax.broadcasted_iota(jnp.int32, (GM_CHUNK, GM_CHUNK), 1) <= lax.broadcasted_iota(
        jnp.int32, (GM_CHUNK, GM_CHUNK), 0)
    bias_all = jnp.where(is_sample, bs_ref[...], bp_ref[...])
    for g in range(groups):
        wm = jnp.where(is_sample, ws_ref[g], wp_ref[g])
        wm = jnp.where(tril, wm, 0.0).astype(BF16)
        bias = bias_all[:, g:g + 1]
        for c in range(tm // GM_CHUNK):
            rows = slice(c * GM_CHUNK, (c + 1) * GM_CHUNK)
            cols = slice(g * gch, (g + 1) * gch)
            sp = _dot(wm, v[rows, cols].astype(BF16)) + bias
            o_ref[rows, cols] = (u[rows, cols] * sp).astype(o_ref.dtype)


def _gmlp(proj, lnw, lnb, w_prompt, w_sample, b_prompt, b_sample, l, n_prompt_rows, groups, gch,
          col_u, col_v, tm):
    m = proj.shape[0]
    gw = groups * gch
    npt = n_prompt_rows // tm
    return pl.pallas_call(
        functools.partial(_gmlp_body, n_prompt_tiles=npt, groups=groups, gch=gch),
        grid=(m // tm,),
        in_specs=[
            pl.BlockSpec((tm, gw), lambda i: (i, col_u)),
            pl.BlockSpec((tm, gw), lambda i: (i, col_v)),
            pl.BlockSpec((None, 1, gw), lambda i: (l, 0, 0)),
            pl.BlockSpec((None, 1, gw), lambda i: (l, 0, 0)),
            pl.BlockSpec((None, groups, GM_CHUNK, GM_CHUNK), lambda i: (l, 0, 0, 0)),
            pl.BlockSpec((None, groups, GM_CHUNK, GM_CHUNK), lambda i: (l, 0, 0, 0)),
            pl.BlockSpec((None, GM_CHUNK, groups), lambda i: (l, 0, 0)),
            pl.BlockSpec((None, GM_CHUNK, groups), lambda i: (l, 0, 0)),
        ],
        out_specs=[
            pl.BlockSpec((tm, gw), lambda i: (i, 0)),
            pl.BlockSpec((tm, gw), lambda i: (jnp.maximum(i - npt, 0), 0)),
        ],
        out_shape=[jax.ShapeDtypeStruct((m, gw), BF16), jax.ShapeDtypeStruct((m - n_prompt_rows, gw), F32)],
        compiler_params=_cparams(("arbitrary",)),
        name="gmlp",
    )(proj, proj, lnw, lnb, w_prompt, w_sample, b_prompt, b_sample)


def _ssd_body(xbc_ref, z_ref, misc_ref, dtrow_ref, arow_ref, prev_ref, s0_ref, cw_ref, cb_ref, dvec_ref,
              nw_ref, o_ref, conv_ref, st_ref, xp_scr, s_scr, y_scr, *, lt, q, heads, hp, ngroups, sn, conv_k):
    t = pl.program_id(1)
    nt = pl.num_programs(1)
    xw = heads * hp
    pad = 8

    @pl.when(t == 0)
    def _():
        xp_scr[0:pad, :] = prev_ref[...]
        s_scr[...] = s0_ref[...]

    xbc = xbc_ref[...]
    xp_scr[pad:pad + lt, :] = xbc
    acc = cb_ref[...]
    for i in range(conv_k):
        off = pad - (conv_k - 1) + i
        acc = acc + xp_scr[off:off + lt, :] * cw_ref[i:i + 1, :]
    xp_scr[0:pad, :] = xbc[lt - pad:lt, :]
    act = acc * jax.nn.sigmoid(acc)

    @pl.when(t == nt - 1)
    def _():
        conv_ref[...] = xbc[lt - (conv_k - 1):lt, :]

    dt_rows = dtrow_ref[...]
    ltp = dt_rows.shape[1]
    a_rows = jnp.concatenate(
        [_lane_cumsum(arow_ref[:, j * LANES:(j + 1) * LANES], q) for j in range(ltp // LANES)], axis=1)
    row = lax.broadcasted_iota(jnp.int32, (lt, lt), 0)
    col = lax.broadcasted_iota(jnp.int32, (lt, lt), 1)
    tri_all = jnp.where((col <= row) & (col >= (row & -q)), 1.0, 0.0).astype(BF16)
    a_cols = sum(_dot(tri_all, part.astype(BF16)) for part in _split3(misc_ref[...]))
    x_all = act[:, 0:xw]
    if lt < LANES:
        x_all = jnp.concatenate([x_all, jnp.zeros((LANES - lt, xw), F32)], axis=0)
    x_t = x_all.T
    tril = lax.broadcasted_iota(jnp.int32, (q, q), 1) <= lax.broadcasted_iota(jnp.int32, (q, q), 0)
    rep = heads // ngroups
    for c in range(lt // q):
        r0 = c * q
        bmat = [act[r0:r0 + q, xw + g * sn: xw + (g + 1) * sn].astype(BF16) for g in range(ngroups)]
        cmat = [act[r0:r0 + q, xw + (ngroups + g) * sn: xw + (ngroups + g + 1) * sn].astype(BF16)
                for g in range(ngroups)]
        gmat = [_dot_nt(cmat[g], bmat[g]) for g in range(ngroups)]
        for h in range(heads):
            g = h // rep
            a_row = a_rows[h:h + 1, r0:r0 + q]
            dt_row = dt_rows[h:h + 1, r0:r0 + q]
            a_col = a_cols[r0:r0 + q, 16 + h:17 + h]
            a_last = a_row[:, q - 1:q]
            lmat = jnp.exp(jnp.where(tril, a_col - a_row, -jnp.inf))
            scores = gmat[g] * lmat * dt_row
            xh = act[r0:r0 + q, h * hp:(h + 1) * hp]
            y = _dot(scores.astype(BF16), xh.astype(BF16))
            decay_end = jnp.exp(a_last - a_row) * dt_row
            xd_t = (x_t[h * hp:(h + 1) * hp, r0:r0 + q] * decay_end).astype(BF16)
            chunk_state = _dot(xd_t, bmat[g])
            s_in = s_scr[h]
            y = y + _dot_nt(cmat[g], s_in.astype(BF16)) * jnp.exp(a_col)
            s_scr[h] = s_in * jnp.exp(a_last) + chunk_state
            y_scr[r0:r0 + q, h * hp:(h + 1) * hp] = y + dvec_ref[:, h * hp:(h + 1) * hp] * xh

    zz = z_ref[...]
    o_ref[...] = _rmsnorm(y_scr[...] * (zz * jax.nn.sigmoid(zz)), nw_ref[...]).astype(o_ref.dtype)

    @pl.when(t == nt - 1)
    def _():
        st_ref[...] = s_scr[...]


def _ssd(proj, dt_rows, a_rows, conv_prev, s0, conv_w, conv_b, dvec, norm_w, l, row0, bsz, seq, lt, q,
         heads, hp, ngroups, sn, col_xbc, col_z, col_misc):
    nt = seq // lt
    xw = heads * hp
    cch = xw + 2 * ngroups * sn
    conv_k = conv_w.shape[1]
    rb0 = row0 // lt
    ltp = max(lt, LANES)
    body = functools.partial(_ssd_body, lt=lt, q=q, heads=heads, hp=hp, ngroups=ngroups, sn=sn, conv_k=conv_k)
    return pl.pallas_call(
        body,
        grid=(bsz, nt),
        in_specs=[
            pl.BlockSpec((lt, cch), lambda b, t: (rb0 + b * nt + t, col_xbc)),
            pl.BlockSpec((lt, xw), lambda b, t: (rb0 + b * nt + t, col_z)),
            pl.BlockSpec((lt, MISC_W), lambda b, t: (rb0 + b * nt + t, col_misc)),
            pl.BlockSpec((None, heads, ltp), lambda b, t: (b, 0, t)),
            pl.BlockSpec((None, heads, ltp), lambda b, t: (b, 0, t)),
            pl.BlockSpec((None, 8, cch), lambda b, t: (b, 0, 0)),
            pl.BlockSpec((None, heads, hp, sn), lambda b, t: (b, 0, 0, 0)),
            pl.BlockSpec((None, conv_k, cch), lambda b, t: (l, 0, 0)),
            pl.BlockSpec((None, 1, cch), lambda b, t: (l, 0, 0)),
            pl.BlockSpec((None, 1, xw), lambda b, t: (l, 0, 0)),
            pl.BlockSpec((None, 1, xw), lambda b, t: (l, 0, 0)),
        ],
        out_specs=[
            pl.BlockSpec((lt, xw), lambda b, t: (b * nt + t, 0)),
            pl.BlockSpec((None, conv_k - 1, cch), lambda b, t: (b, 0, 0)),
            pl.BlockSpec((None, heads, hp, sn), lambda b, t: (b, 0, 0, 0)),
        ],
        out_shape=[
            jax.ShapeDtypeStruct((bsz * seq, xw), BF16),
            jax.ShapeDtypeStruct((bsz, conv_k - 1, cch), F32),
            jax.ShapeDtypeStruct((bsz, heads, hp, sn), F32),
        ],
        scratch_shapes=[pltpu.VMEM((lt + 8, cch), F32), pltpu.VMEM((heads, hp, sn), F32),
                        pltpu.VMEM((lt, xw), F32)],
        compiler_params=_cparams(("parallel", "arbitrary")),
        name="ssd",
    )(proj, proj, proj, dt_rows, a_rows, conv_prev, s0, conv_w, conv_b, dvec, norm_w)


def _outproj_body(fox_ref, gm_ref, ssd_ref, x_ref, nw_ref, w_ref, o_ref, *, fw, gw):
    o = _dot(fox_ref[...], w_ref[0:fw, :])
    o = o + _dot(gm_ref[...], w_ref[fw:fw + gw, :])
    o = o + _dot(ssd_ref[...], w_ref[fw + gw:, :])
    o_ref[...] = x_ref[...] + _rmsnorm(o, nw_ref[...])


def _outproj(fox_o, gm_o, ssd_o, x, norm_w3, w_out, l, tm):
    m, d = x.shape
    fw, gw, sw = fox_o.shape[1], gm_o.shape[1], ssd_o.shape[1]
    return pl.pallas_call(
        functools.partial(_outproj_body, fw=fw, gw=gw),
        grid=(m // tm,),
        in_specs=[
            pl.BlockSpec((tm, fw), lambda i: (i, 0)),
            pl.BlockSpec((tm, gw), lambda i: (i, 0)),
            pl.BlockSpec((tm, sw), lambda i: (i, 0)),
            pl.BlockSpec((tm, d), lambda i: (i, 0)),
            pl.BlockSpec((None, 1, d), lambda i: (l * 6 + 3, 0, 0)),
            pl.BlockSpec((None, fw + gw + sw, d), lambda i: (l, 0, 0)),
        ],
        out_specs=pl.BlockSpec((tm, d), lambda i: (i, 0)),
        out_shape=jax.ShapeDtypeStruct((m, d), F32),
        compiler_params=_cparams(("parallel",)),
        name=f"outproj_{l}",
    )(fox_o, gm_o, ssd_o, x, norm_w3, w_out)


def _head_rows(vals, bsz, seq):
    return jnp.transpose(vals.reshape(bsz, seq, vals.shape[-1]), (0, 2, 1))


def _pad_lanes(rows, width):
    return jnp.pad(rows, ((0, 0), (0, 0), (0, width - rows.shape[-1])))


def kernel(x_prompt, x_sample, cache_fox_k, cache_fox_v, cache_fox_logf, state_ssd_conv, state_ssd, norm_w,
           w_ffn_gate, w_ffn_up, w_ffn_down, w_in, fox_fb, gm_ln_w, gm_ln_b, gm_ws, gm_bs, ssd_conv_w,
           ssd_conv_b, ssd_dt_bias, ssd_a_log, ssd_d, ssd_norm_w, w_out):
    bsz, seq, d = x_prompt.shape
    dbsz, dseq, _ = x_sample.shape
    depth, _, past, heads, dh = cache_fox_k.shape
    fox_w = heads * dh
    groups, gchunk = gm_ws.shape[1], gm_ws.shape[2]
    gm_w = gm_ln_w.shape[1]
    gch = gm_w // groups
    _, _, sheads, hp, sn = state_ssd.shape
    ssd_w = sheads * hp
    cch = ssd_conv_w.shape[2]
    ngroups = (cch - ssd_w) // (2 * sn)
    conv_k = ssd_conv_w.shape[1]
    n_p, n_s = bsz * seq, dbsz * dseq
    m = n_p + n_s
    assert gchunk == GM_CHUNK and heads == 8 and sheads == 8 and dh == LANES
    assert seq % GM_CHUNK == 0 and GM_CHUNK % dseq == 0 and dseq < SSD_CHUNK and past % LANES == 0

    wg, wu, wd = w_ffn_gate, w_ffn_up, w_ffn_down
    wo = w_out.astype(BF16)
    sizes = [fox_w, fox_w, fox_w, heads, gm_w, gm_w, ssd_w, cch, sheads]
    offs = [0]
    for s in sizes:
        offs.append(offs[-1] + s)
    wq, wk, wv, wf, wgu, wgv, wz, wxbc, wdt = (w_in[:, :, offs[i]:offs[i + 1]] for i in range(9))
    tn = 1024
    used = 3 * fox_w + 2 * gm_w + cch + ssd_w + MISC_W
    npad = -(-used // tn) * tn
    zeros_misc = jnp.zeros((depth, d, MISC_W - heads - 2 * sheads), F32)
    zeros_tail = jnp.zeros((depth, d, npad - used), F32)
    w_r = jnp.concatenate([wq, wk, wv, wgu, wgv, wxbc, wz, wf, wdt, wdt, zeros_misc, zeros_tail],
                          axis=-1).astype(BF16)
    col_q, col_k, col_v = 0, fox_w // dh, 2 * fox_w // dh
    col_u, col_gv = 3 * fox_w // gm_w, 3 * fox_w // gm_w + 1
    off_xbc = 3 * fox_w + 2 * gm_w
    off_z = off_xbc + cch
    off_misc = off_z + ssd_w
    assert off_xbc % cch == 0 and off_z % ssd_w == 0 and off_misc % MISC_W == 0
    misc_in_tile = off_misc % tn
    assert off_misc // tn == npad // tn - 1 and misc_in_tile + MISC_W <= tn

    lane_pad = jnp.zeros((depth, MISC_W - heads - 2 * sheads), F32)
    bias_vec = jnp.concatenate([fox_fb, ssd_dt_bias, ssd_dt_bias, lane_pad], axis=-1)[:, None, :]
    alog_vec = jnp.concatenate([jnp.zeros((depth, heads + sheads), F32), ssd_a_log, lane_pad], axis=-1)[:, None, :]
    norm_w3 = norm_w.reshape(depth * 6, 1, d)
    lnw3, lnb3 = gm_ln_w[:, None, :], gm_ln_b[:, None, :]
    reps = GM_CHUNK // dseq
    eye = jnp.eye(reps, dtype=F32)
    ws_s = jnp.einsum('ab,lgts->lgatbs', eye, gm_ws[:, :, :dseq, :dseq]).reshape(depth, groups, GM_CHUNK, GM_CHUNK)
    bs_p = jnp.transpose(gm_bs, (0, 2, 1))
    bs_s = jnp.tile(jnp.transpose(gm_bs[:, :, :dseq], (0, 2, 1)), (1, reps, 1))
    conv_b3 = ssd_conv_b[:, None, :]
    dvec = jnp.repeat(ssd_d, hp, axis=-1)[:, None, :]
    snw3 = ssd_norm_w[:, None, :]
    cache_k = cache_fox_k.reshape(depth, dbsz, past, fox_w)
    cache_v = cache_fox_v.reshape(depth, dbsz, past, fox_w)
    conv_prev_p = jnp.zeros((bsz, 8, cch), F32)
    conv_prev_s = jnp.pad(state_ssd_conv, ((0, 0), (0, 0), (8 - (conv_k - 1), 0), (0, 0)))
    s0_p = jnp.zeros((bsz, sheads, hp, sn), F32)

    x = jnp.concatenate([x_prompt.reshape(n_p, d), x_sample.reshape(n_s, d)], axis=0)
    tm_big = m // 8 if m % 64 == 0 and (m // 8) % 16 == 0 else 256
    tm_ffn = tm_big
    tm_out = m // 16 if m % 16 == 0 and (m // 16) % 16 == 0 else 256
    lf_len = 2 * past
    assert past + dseq <= lf_len

    outs = {k: [] for k in ("plf", "pconv", "pst", "slf", "sgm", "sconv", "sst")}
    kv_p = kv_s = None
    for l in range(depth):
        x = _ffn(x, norm_w3, wg, wu, wd, l, 0, tm_ffn, 256)
        proj = _inproj(x, norm_w3, w_r, bias_vec, alog_vec, l, tm_big, tn, misc_in_tile)
        misc = proj[:, off_misc:off_misc + heads + 2 * sheads]
        logf_p, logf_s = misc[:n_p, :heads], misc[n_p:, :heads]
        lf_rows_p = _head_rows(logf_p, bsz, seq).reshape(bsz, heads, seq // LANES, LANES)
        lf_all = jnp.concatenate([cache_fox_logf[l], logf_s.reshape(dbsz, dseq, heads),
                                  jnp.zeros((dbsz, lf_len - past - dseq, heads), F32)], axis=1)
        lf_rows_s = jnp.transpose(lf_all, (0, 2, 1)).reshape(dbsz, heads, lf_len // LANES, LANES)
        dt_rows_p = _head_rows(misc[:n_p, heads:heads + sheads], bsz, seq)
        a_rows_p = _head_rows(misc[:n_p, heads + sheads:], bsz, seq)
        dt_rows_s = _pad_lanes(_head_rows(misc[n_p:, heads:heads + sheads], dbsz, dseq), LANES)
        a_rows_s = _pad_lanes(_head_rows(misc[n_p:, heads + sheads:], dbsz, dseq), LANES)

        fox_p, *kv_p = _fox_prompt(proj, lf_rows_p, kv_p, l, depth, bsz, seq, heads, dh, col_q, col_k, col_v, 256)
        fox_s, *kv_s = _fox_sample(proj, lf_rows_s, cache_k, cache_v, kv_s, l, n_p, dbsz, dseq, heads, dh, 0, 1, 2)
        gm_o, gm_v_s = _gmlp(proj, lnw3, lnb3, gm_ws, ws_s, bs_p, bs_s, l, n_p, groups, gch, col_u, col_gv, 256)
        ssd_p, conv_p, st_p = _ssd(proj, dt_rows_p, a_rows_p, conv_prev_p, s0_p, ssd_conv_w, conv_b3, dvec, snw3,
                                   l, 0, bsz, seq, 256, SSD_CHUNK, sheads, hp, ngroups, sn,
                                   off_xbc // cch, off_z // ssd_w, off_misc // MISC_W)
        ssd_s, conv_s, st_s = _ssd(proj, dt_rows_s, a_rows_s, conv_prev_s[l], state_ssd[l], ssd_conv_w, conv_b3,
                                   dvec, snw3, l, n_p, dbsz, dseq, dseq, dseq, sheads, hp, ngroups, sn,
                                   off_xbc // cch, off_z // ssd_w, off_misc // MISC_W)
        fox_o = jnp.concatenate([fox_p, fox_s], axis=0)
        ssd_o = jnp.concatenate([ssd_p, ssd_s], axis=0)
        x = _outproj(fox_o, gm_o, ssd_o, x, norm_w3, wo, l, tm_out)
        x = _ffn(x, norm_w3, wg, wu, wd, l, 1, tm_ffn, 256)

        outs["plf"].append(logf_p.reshape(bsz, seq, heads))
        outs["pconv"].append(conv_p)
        outs["pst"].append(st_p)
        outs["slf"].append(logf_s.reshape(dbsz, dseq, heads))
        outs["sgm"].append(gm_v_s.reshape(dbsz, dseq, gm_w))
        outs["sconv"].append(conv_s)
        outs["sst"].append(st_s)

    st = {k: jnp.stack(v) for k, v in outs.items()}
    y_prompt = x[:n_p].reshape(bsz, seq, d)
    y_sample = x[n_p:].reshape(dbsz, dseq, d)
    p_k, p_v = (a.reshape(depth, bsz, seq, heads, dh) for a in kv_p)
    s_k, s_v = (a.reshape(depth, dbsz, dseq, heads, dh) for a in kv_s)
    return (y_prompt, y_sample, p_k, p_v, st["plf"], st["pconv"], st["pst"],
            s_k, s_v, st["slf"], st["sgm"], st["sconv"], st["sst"])
```

```python
import functools

import jax
import jax.numpy as jnp
from jax import lax
from jax.experimental import pallas as pl
from jax.experimental.pallas import tpu as pltpu

F32 = jnp.float32
BF16 = jnp.bfloat16
EPS = 1e-6
LANES = 128
SSD_CHUNK = 64
GM_CHUNK = 128
VMEM_LIMIT = 56 * 1024 * 1024

MISC_W = LANES


def _cparams(sem):
    return pltpu.CompilerParams(dimension_semantics=sem, vmem_limit_bytes=VMEM_LIMIT)


def _rmsnorm(x, w):
    return x * lax.rsqrt(jnp.mean(x * x, axis=-1, keepdims=True) + EPS) * w


def _softplus(x):
    return jnp.maximum(x, 0.0) + jnp.log1p(jnp.exp(-jnp.abs(x)))


def _dot(a, b):
    return jnp.dot(a, b, preferred_element_type=F32)


def _dot_nt(a, b):
    return lax.dot_general(a, b, (((1,), (1,)), ((), ())), preferred_element_type=F32)


def _dot_tn(a, b):
    return lax.dot_general(a, b, (((0,), (0,)), ((), ())), preferred_element_type=F32)


def _row_to_col(row):
    n = row.shape[1]
    eye = lax.broadcasted_iota(jnp.int32, (n, n), 0) == lax.broadcasted_iota(jnp.int32, (n, n), 1)
    return jnp.sum(jnp.where(eye, jnp.broadcast_to(row, (n, n)), 0.0), axis=1, keepdims=True)


def _lane_cumsum(x, seg):
    lane = lax.broadcasted_iota(jnp.int32, x.shape, 1) & (seg - 1)
    k = 1
    while k < seg:
        x = x + jnp.where(lane >= k, pltpu.roll(x, k, axis=1), 0.0)
        k *= 2
    return x


def _flat_cumsum(x):
    x = _lane_cumsum(x, LANES)
    tot = x[:, LANES - 1:LANES]
    rows = []
    carry = jnp.zeros((1, 1), F32)
    for r in range(x.shape[0]):
        rows.append(x[r:r + 1, :] + carry)
        carry = carry + tot[r:r + 1, :]
    return jnp.concatenate(rows, axis=0)


def _ffn_body(x_ref, nwa_ref, nwb_ref, wg_ref, wu_ref, wd_ref, o_ref, h_ref):
    f = pl.program_id(1)

    @pl.when(f == 0)
    def _():
        h_ref[...] = _rmsnorm(x_ref[...], nwa_ref[...]).astype(BF16)
        o_ref[...] = jnp.zeros_like(o_ref)

    h = h_ref[...]
    a = _dot(h, wg_ref[...])
    g = _dot(h, wu_ref[...])
    act = (a * jax.nn.sigmoid(a) * g).astype(BF16)
    o_ref[...] += _dot(act, wd_ref[...])

    @pl.when(f == pl.num_programs(1) - 1)
    def _():
        o_ref[...] = x_ref[...] + 0.5 * _rmsnorm(o_ref[...], nwb_ref[...])


def _ffn(x, norm_w3, wg, wu, wd, l, j, tm, tf):
    m, d = x.shape
    dff = wg.shape[-1]
    return pl.pallas_call(
        _ffn_body,
        grid=(m // tm, dff // tf),
        in_specs=[
            pl.BlockSpec((tm, d), lambda i, f: (i, 0)),
            pl.BlockSpec((None, 1, d), lambda i, f: (l * 6 + 3 * j + j, 0, 0)),
            pl.BlockSpec((None, 1, d), lambda i, f: (l * 6 + 3 * j + j + 1, 0, 0)),
            pl.BlockSpec((None, None, d, tf), lambda i, f: (l, j, 0, f)),
            pl.BlockSpec((None, None, d, tf), lambda i, f: (l, j, 0, f)),
            pl.BlockSpec((None, None, tf, d), lambda i, f: (l, j, f, 0)),
        ],
        out_specs=pl.BlockSpec((tm, d), lambda i, f: (i, 0)),
        out_shape=jax.ShapeDtypeStruct((m, d), F32),
        scratch_shapes=[pltpu.VMEM((tm, d), BF16)],
        compiler_params=_cparams(("parallel", "arbitrary")),
        name=f"ffn_{l}_{j}",
    )(x, norm_w3, norm_w3, wg, wu, wd)


def _inproj_body(x_ref, nw_ref, w_ref, bias_ref, alog_ref, o_ref, h_ref, *, misc_off):
    j = pl.program_id(1)

    @pl.when(j == 0)
    def _():
        h_ref[...] = _rmsnorm(x_ref[...], nw_ref[...]).astype(BF16)

    acc = _dot_nt(h_ref[...], w_ref[...])
    o_ref[...] = acc

    @pl.when(j == pl.num_programs(1) - 1)
    def _():
        raw = acc[:, misc_off:misc_off + MISC_W] + bias_ref[...]
        lane = lax.broadcasted_iota(jnp.int32, raw.shape, 1)
        sp = _softplus(raw)
        logf = -_softplus(-raw)
        neg_a = -jnp.exp(alog_ref[...])
        val = jnp.where(lane < 8, logf, jnp.where(lane < 16, sp, jnp.where(lane < 24, sp * neg_a, 0.0)))
        o_ref[:, misc_off:misc_off + MISC_W] = val


def _inproj(x, norm_w3, w_r, bias_vec, alog_vec, l, tm, tn, misc_off):
    m, d = x.shape
    npad = w_r.shape[1]
    return pl.pallas_call(
        functools.partial(_inproj_body, misc_off=misc_off),
        grid=(m // tm, npad // tn),
        in_specs=[
            pl.BlockSpec((tm, d), lambda i, j: (i, 0)),
            pl.BlockSpec((None, 1, d), lambda i, j: (l * 6 + 2, 0, 0)),
            pl.BlockSpec((None, tn, d), lambda i, j: (l, j, 0)),
            pl.BlockSpec((None, 1, MISC_W), lambda i, j: (l, 0, 0)),
            pl.BlockSpec((None, 1, MISC_W), lambda i, j: (l, 0, 0)),
        ],
        out_specs=pl.BlockSpec((tm, tn), lambda i, j: (i, j)),
        out_shape=jax.ShapeDtypeStruct((m, npad), F32),
        scratch_shapes=[pltpu.VMEM((tm, d), BF16)],
        compiler_params=_cparams(("parallel", "arbitrary")),
        name=f"inproj_{l}",
    )(x, norm_w3, w_r, bias_vec, alog_vec)


def _rows_as_lane_vector(c_ref, start, count):
    rows = c_ref[pl.ds(start, count), :]
    return jnp.concatenate([rows[r:r + 1, :] for r in range(count)], axis=1)


def _store_layer_rows(out_ref, val, l, aliased):
    if aliased:
        out_ref[...] = val
    else:
        for slot in range(out_ref.shape[0]):
            out_ref[slot] = val if slot == l else jnp.zeros_like(val)


def _split3(x):
    hi = x.astype(BF16).astype(F32)
    r = x - hi
    mid = r.astype(BF16).astype(F32)
    return hi, mid, r - mid


def _fox_prompt_body(lf_ref, q_ref, k_ref, v_ref, *rest, tq, scale, l, aliased):
    o_ref, ko_ref, vo_ref, c_scr, qx_scr, kx_scr, vb_scr, s_scr = rest[2 * aliased:]
    seq, dh = q_ref.shape
    nq = seq // tq
    sub = tq // LANES
    log2e = 1.4426950408889634
    c_scr[...] = _flat_cumsum(lf_ref[...]) * log2e
    _store_layer_rows(ko_ref, k_ref[...], l, aliased)
    _store_layer_rows(vo_ref, v_ref[...], l, aliased)
    vb_scr[...] = v_ref[...].astype(BF16)
    lane = lax.broadcasted_iota(jnp.int32, (tq, LANES), 1)
    for i in range(nq):
        rows = slice(i * tq, (i + 1) * tq)
        hi, mid, lo = _split3(_row_to_col(_rows_as_lane_vector(c_scr, i * sub, sub)))
        q_ext = jnp.where(lane == 0, hi, jnp.where(lane == 1, mid, jnp.where(lane == 2, lo,
                          jnp.where(lane < 6, 1.0, 0.0))))
        k_ext = jnp.where(lane < 3, 1.0, jnp.where(lane == 3, -hi, jnp.where(lane == 4, -mid,
                          jnp.where(lane == 5, -lo, 0.0))))
        qx_scr[rows, 0:dh] = (q_ref[rows, :] * (scale * log2e)).astype(BF16)
        qx_scr[rows, dh:dh + LANES] = q_ext.astype(BF16)
        kx_scr[rows, 0:dh] = k_ref[rows, :].astype(BF16)
        kx_scr[rows, dh:dh + LANES] = k_ext.astype(BF16)

    causal = lax.broadcasted_iota(jnp.int32, (tq, tq), 1) <= lax.broadcasted_iota(jnp.int32, (tq, tq), 0)
    for i in range(nq):
        qx = qx_scr[i * tq:(i + 1) * tq, :]
        m_acc = jnp.full((tq, LANES), -jnp.inf, F32)
        for j in range(i + 1):
            s = _dot_nt(qx, kx_scr[j * tq:(j + 1) * tq, :])
            if j == i:
                s = jnp.where(causal, s, -jnp.inf)
            s_scr[:, j * tq:(j + 1) * tq] = s
            for u in range(sub):
                m_acc = jnp.maximum(m_acc, s[:, u * LANES:(u + 1) * LANES])
        m = jnp.max(m_acc, axis=1, keepdims=True)
        l_acc = jnp.zeros((tq, LANES), F32)
        acc = jnp.zeros((tq, dh), F32)
        for j in range(i + 1):
            p = jnp.exp2(s_scr[:, j * tq:(j + 1) * tq] - m)
            for u in range(sub):
                l_acc = l_acc + p[:, u * LANES:(u + 1) * LANES]
            acc = acc + _dot(p.astype(BF16), vb_scr[j * tq:(j + 1) * tq, :])
        o_ref[i * tq:(i + 1) * tq, :] = (acc / jnp.sum(l_acc, axis=1, keepdims=True)).astype(o_ref.dtype)


def _fox_prompt(proj, lf_rows, kv_prev, l, depth, bsz, seq, heads, dh, col_q, col_k, col_v, tq):
    n = seq // LANES
    aliased = kv_prev is not None
    kv_shape = jax.ShapeDtypeStruct((depth, bsz * seq, heads * dh), F32)
    kv_spec = (pl.BlockSpec((None, seq, dh), lambda b, h: (l, b, h)) if aliased else
               pl.BlockSpec((depth, seq, dh), lambda b, h: (0, b, h)))
    in_specs = [
        pl.BlockSpec((None, None, n, LANES), lambda b, h: (b, h, 0, 0)),
        pl.BlockSpec((seq, dh), lambda b, h: (b, col_q + h)),
        pl.BlockSpec((seq, dh), lambda b, h: (b, col_k + h)),
        pl.BlockSpec((seq, dh), lambda b, h: (b, col_v + h)),
    ]
    args = [lf_rows, proj, proj, proj]
    if aliased:
        in_specs += [pl.BlockSpec(memory_space=pl.ANY)] * 2
        args += list(kv_prev)
    return pl.pallas_call(
        functools.partial(_fox_prompt_body, tq=tq, scale=dh ** -0.5, l=l, aliased=aliased),
        grid=(bsz, heads),
        in_specs=in_specs,
        out_specs=[pl.BlockSpec((seq, dh), lambda b, h: (b, h)), kv_spec, kv_spec],
        out_shape=[jax.ShapeDtypeStruct((bsz * seq, heads * dh), BF16), kv_shape, kv_shape],
        scratch_shapes=[pltpu.VMEM((n, LANES), F32), pltpu.VMEM((seq, dh + LANES), BF16),
                        pltpu.VMEM((seq, dh + LANES), BF16), pltpu.VMEM((seq, dh), BF16),
                        pltpu.VMEM((tq, seq), F32)],
        input_output_aliases={4: 1, 5: 2} if aliased else {},
        compiler_params=_cparams(("parallel", "parallel")),
        name="fox_prompt",
    )(*args)


def _fox_sample_body(lf_ref, q_ref, k_ref, v_ref, ck_ref, cv_ref, *rest, past, new, heads, dh, scale, l,
                     aliased):
    o_ref, ko_ref, vo_ref = rest[2 * aliased:]
    _store_layer_rows(ko_ref, k_ref[...], l, aliased)
    _store_layer_rows(vo_ref, v_ref[...], l, aliased)
    n_past = past // LANES
    causal = lax.broadcasted_iota(jnp.int32, (new, new), 1) <= lax.broadcasted_iota(jnp.int32, (new, new), 0)
    for h in range(heads):
        cols = slice(h * dh, (h + 1) * dh)
        c = _flat_cumsum(lf_ref[h])
        c_past = jnp.concatenate([c[r:r + 1, :] for r in range(n_past)], axis=1)
        c_new = c[n_past:n_past + 1, 0:new]
        cq_col = _row_to_col(c_new)
        q = q_ref[:, cols].astype(BF16)
        s_past = _dot_nt(q, ck_ref[:, h, :].astype(BF16)) * scale + cq_col - c_past
        s_new = _dot_nt(q, k_ref[:, cols].astype(BF16)) * scale + cq_col - c_new
        s_new = jnp.where(causal, s_new, -jnp.inf)
        m = jnp.maximum(jnp.max(s_past, axis=1, keepdims=True), jnp.max(s_new, axis=1, keepdims=True))
        p_past = jnp.exp(s_past - m)
        p_new = jnp.exp(s_new - m)
        denom = jnp.sum(p_past, axis=1, keepdims=True) + jnp.sum(p_new, axis=1, keepdims=True)
        acc = _dot(p_past.astype(BF16), cv_ref[:, h, :].astype(BF16)) + _dot(
            p_new.astype(BF16), v_ref[:, cols].astype(BF16))
        o_ref[:, cols] = (acc / denom).astype(o_ref.dtype)


def _fox_sample(proj, lf_rows, cache_k, cache_v, kv_prev, l, row0, bsz, new, heads, dh, col_q, col_k, col_v):
    depth, _, past = cache_k.shape[:3]
    fw = heads * dh
    n = lf_rows.shape[2]
    rb0 = row0 // new
    aliased = kv_prev is not None
    kv_shape = jax.ShapeDtypeStruct((depth, bsz * new, fw), F32)
    kv_spec = (pl.BlockSpec((None, new, fw), lambda b: (l, b, 0)) if aliased else
               pl.BlockSpec((depth, new, fw), lambda b: (0, b, 0)))
    in_specs = [
        pl.BlockSpec((None, heads, n, LANES), lambda b: (b, 0, 0, 0)),
        pl.BlockSpec((new, fw), lambda b: (rb0 + b, col_q)),
        pl.BlockSpec((new, fw), lambda b: (rb0 + b, col_k)),
        pl.BlockSpec((new, fw), lambda b: (rb0 + b, col_v)),
        pl.BlockSpec((None, None, past, heads, dh), lambda b: (l, b, 0, 0, 0)),
        pl.BlockSpec((None, None, past, heads, dh), lambda b: (l, b, 0, 0, 0)),
    ]
    args = [lf_rows, proj, proj, proj, cache_k, cache_v]
    if aliased:
        in_specs += [pl.BlockSpec(memory_space=pl.ANY)] * 2
        args += list(kv_prev)
    return pl.pallas_call(
        functools.partial(_fox_sample_body, past=past, new=new, heads=heads, dh=dh, scale=dh ** -0.5,
                          l=l, aliased=aliased),
        grid=(bsz,),
        in_specs=in_specs,
        out_specs=[pl.BlockSpec((new, fw), lambda b: (b, 0)), kv_spec, kv_spec],
        out_shape=[jax.ShapeDtypeStruct((bsz * new, fw), BF16), kv_shape, kv_shape],
        input_output_aliases={6: 1, 7: 2} if aliased else {},
        compiler_params=_cparams(("parallel",)),
        name="fox_sample",
    )(*args)


def _gmlp_body(gu_ref, gv_ref, lnw_ref, lnb_ref, wp_ref, ws_ref, bp_ref, bs_ref, o_ref, v_ref, *,
               n_prompt_tiles, groups, gch):
    i = pl.program_id(0)
    is_sample = i >= n_prompt_tiles
    u = jax.nn.gelu(gu_ref[...])
    gv = jax.nn.gelu(gv_ref[...])
    mu = jnp.mean(gv, axis=-1, keepdims=True)
    var = jnp.mean(jnp.square(gv - mu), axis=-1, keepdims=True)
    v = (gv - mu) * lax.rsqrt(var + EPS) * lnw_ref[...] + lnb_ref[...]

    @pl.when(is_sample)
    def _():
        v_ref[...] = v

    tm = u.shape[0]
    tril = lax.broadcasted_iota(jnp.int32, (GM_CHUNK, GM_CHUNK), 1) <= lax.broadcasted_iota(
        jnp.int32, (GM_CHUNK, GM_CHUNK), 0)
    bias_all = jnp.where(is_sample, bs_ref[...], bp_ref[...])
    for g in range(groups):
        wm = jnp.where(is_sample, ws_ref[g], wp_ref[g])
        wm = jnp.where(tril, wm, 0.0).astype(BF16)
        bias = bias_all[:, g:g + 1]
        for c in range(tm // GM_CHUNK):
            rows = slice(c * GM_CHUNK, (c + 1) * GM_CHUNK)
            cols = slice(g * gch, (g + 1) * gch)
            sp = _dot(wm, v[rows, cols].astype(BF16)) + bias
            o_ref[rows, cols] = (u[rows, cols] * sp).astype(o_ref.dtype)


def _gmlp(proj, lnw, lnb, w_prompt, w_sample, b_prompt, b_sample, l, n_prompt_rows, groups, gch,
          col_u, col_v, tm):
    m = proj.shape[0]
    gw = groups * gch
    npt = n_prompt_rows // tm
    return pl.pallas_call(
        functools.partial(_gmlp_body, n_prompt_tiles=npt, groups=groups, gch=gch),
        grid=(m // tm,),
        in_specs=[
            pl.BlockSpec((tm, gw), lambda i: (i, col_u)),
            pl.BlockSpec((tm, gw), lambda i: (i, col_v)),
            pl.BlockSpec((None, 1, gw), lambda i: (l, 0, 0)),
            pl.BlockSpec((None, 1, gw), lambda i: (l, 0, 0)),
            pl.BlockSpec((None, groups, GM_CHUNK, GM_CHUNK), lambda i: (l, 0, 0, 0)),
            pl.BlockSpec((None, groups, GM_CHUNK, GM_CHUNK), lambda i: (l, 0, 0, 0)),
            pl.BlockSpec((None, GM_CHUNK, groups), lambda i: (l, 0, 0)),
            pl.BlockSpec((None, GM_CHUNK, groups), lambda i: (l, 0, 0)),
        ],
        out_specs=[
            pl.BlockSpec((tm, gw), lambda i: (i, 0)),
            pl.BlockSpec((tm, gw), lambda i: (jnp.maximum(i - npt, 0), 0)),
        ],
        out_shape=[jax.ShapeDtypeStruct((m, gw), BF16), jax.ShapeDtypeStruct((m - n_prompt_rows, gw), F32)],
        compiler_params=_cparams(("arbitrary",)),
        name="gmlp",
    )(proj, proj, lnw, lnb, w_prompt, w_sample, b_prompt, b_sample)


def _ssd_body(xbc_ref, z_ref, misc_ref, dtrow_ref, arow_ref, prev_ref, s0_ref, cw_ref, cb_ref, dvec_ref,
              nw_ref, o_ref, conv_ref, st_ref, xp_scr, s_scr, y_scr, *, lt, q, heads, hp, ngroups, sn, conv_k):
    t = pl.program_id(1)
    nt = pl.num_programs(1)
    xw = heads * hp
    pad = 8

    @pl.when(t == 0)
    def _():
        xp_scr[0:pad, :] = prev_ref[...]
        s_scr[...] = s0_ref[...]

    xbc = xbc_ref[...]
    xp_scr[pad:pad + lt, :] = xbc
    acc = cb_ref[...]
    for i in range(conv_k):
        off = pad - (conv_k - 1) + i
        acc = acc + xp_scr[off:off + lt, :] * cw_ref[i:i + 1, :]
    xp_scr[0:pad, :] = xbc[lt - pad:lt, :]
    act = acc * jax.nn.sigmoid(acc)

    @pl.when(t == nt - 1)
    def _():
        conv_ref[...] = xbc[lt - (conv_k - 1):lt, :]

    dt_rows = dtrow_ref[...]
    ltp = dt_rows.shape[1]
    a_rows = jnp.concatenate(
        [_lane_cumsum(arow_ref[:, j * LANES:(j + 1) * LANES], q) for j in range(ltp // LANES)], axis=1)
    row = lax.broadcasted_iota(jnp.int32, (lt, lt), 0)
    col = lax.broadcasted_iota(jnp.int32, (lt, lt), 1)
    tri_all = jnp.where((col <= row) & (col >= (row & -q)), 1.0, 0.0).astype(BF16)
    a_cols = sum(_dot(tri_all, part.astype(BF16)) for part in _split3(misc_ref[...]))
    x_all = act[:, 0:xw]
    if lt < LANES:
        x_all = jnp.concatenate([x_all, jnp.zeros((LANES - lt, xw), F32)], axis=0)
    x_t = x_all.T
    tril = lax.broadcasted_iota(jnp.int32, (q, q), 1) <= lax.broadcasted_iota(jnp.int32, (q, q), 0)
    rep = heads // ngroups
    for c in range(lt // q):
        r0 = c * q
        bmat = [act[r0:r0 + q, xw + g * sn: xw + (g + 1) * sn].astype(BF16) for g in range(ngroups)]
        cmat = [act[r0:r0 + q, xw + (ngroups + g) * sn: xw + (ngroups + g + 1) * sn].astype(BF16)
                for g in range(ngroups)]
        gmat = [_dot_nt(cmat[g], bmat[g]) for g in range(ngroups)]
        for h in range(heads):
            g = h // rep
            a_row = a_rows[h:h + 1, r0:r0 + q]
            dt_row = dt_rows[h:h + 1, r0:r0 + q]
            a_col = a_cols[r0:r0 + q, 16 + h:17 + h]
            a_last = a_row[:, q - 1:q]
            lmat = jnp.exp(jnp.where(tril, a_col - a_row, -jnp.inf))
            scores = gmat[g] * lmat * dt_row
            xh = act[r0:r0 + q, h * hp:(h + 1) * hp]
            y = _dot(scores.astype(BF16), xh.astype(BF16))
            decay_end = jnp.exp(a_last - a_row) * dt_row
            xd_t = (x_t[h * hp:(h + 1) * hp, r0:r0 + q] * decay_end).astype(BF16)
            chunk_state = _dot(xd_t, bmat[g])
            s_in = s_scr[h]
            y = y + _dot_nt(cmat[g], s_in.astype(BF16)) * jnp.exp(a_col)
            s_scr[h] = s_in * jnp.exp(a_last) + chunk_state
            y_scr[r0:r0 + q, h * hp:(h + 1) * hp] = y + dvec_ref[:, h * hp:(h + 1) * hp] * xh

    zz = z_ref[...]
    o_ref[...] = _rmsnorm(y_scr[...] * (zz * jax.nn.sigmoid(zz)), nw_ref[...]).astype(o_ref.dtype)

    @pl.when(t == nt - 1)
    def _():
        st_ref[...] = s_scr[...]


def _ssd(proj, dt_rows, a_rows, conv_prev, s0, conv_w, conv_b, dvec, norm_w, l, row0, bsz, seq, lt, q,
         heads, hp, ngroups, sn, col_xbc, col_z, col_misc):
    nt = seq // lt
    xw = heads * hp
    cch = xw + 2 * ngroups * sn
    conv_k = conv_w.shape[1]
    rb0 = row0 // lt
    ltp = max(lt, LANES)
    body = functools.partial(_ssd_body, lt=lt, q=q, heads=heads, hp=hp, ngroups=ngroups, sn=sn, conv_k=conv_k)
    return pl.pallas_call(
        body,
        grid=(bsz, nt),
        in_specs=[
            pl.BlockSpec((lt, cch), lambda b, t: (rb0 + b * nt + t, col_xbc)),
            pl.BlockSpec((lt, xw), lambda b, t: (rb0 + b * nt + t, col_z)),
            pl.BlockSpec((lt, MISC_W), lambda b, t: (rb0 + b * nt + t, col_misc)),
            pl.BlockSpec((None, heads, ltp), lambda b, t: (b, 0, t)),
            pl.BlockSpec((None, heads, ltp), lambda b, t: (b, 0, t)),
            pl.BlockSpec((None, 8, cch), lambda b, t: (b, 0, 0)),
            pl.BlockSpec((None, heads, hp, sn), lambda b, t: (b, 0, 0, 0)),
            pl.BlockSpec((None, conv_k, cch), lambda b, t: (l, 0, 0)),
            pl.BlockSpec((None, 1, cch), lambda b, t: (l, 0, 0)),
            pl.BlockSpec((None, 1, xw), lambda b, t: (l, 0, 0)),
            pl.BlockSpec((None, 1, xw), lambda b, t: (l, 0, 0)),
        ],
        out_specs=[
            pl.BlockSpec((lt, xw), lambda b, t: (b * nt + t, 0)),
            pl.BlockSpec((None, conv_k - 1, cch), lambda b, t: (b, 0, 0)),
            pl.BlockSpec((None, heads, hp, sn), lambda b, t: (b, 0, 0, 0)),
        ],
        out_shape=[
            jax.ShapeDtypeStruct((bsz * seq, xw), BF16),
            jax.ShapeDtypeStruct((bsz, conv_k - 1, cch), F32),
            jax.ShapeDtypeStruct((bsz, heads, hp, sn), F32),
        ],
        scratch_shapes=[pltpu.VMEM((lt + 8, cch), F32), pltpu.VMEM((heads, hp, sn), F32),
                        pltpu.VMEM((lt, xw), F32)],
        compiler_params=_cparams(("parallel", "arbitrary")),
        name="ssd",
    )(proj, proj, proj, dt_rows, a_rows, conv_prev, s0, conv_w, conv_b, dvec, norm_w)


def _outproj_body(fox_ref, gm_ref, ssd_ref, x_ref, nw_ref, w_ref, o_ref, *, fw, gw):
    o = _dot(fox_ref[...], w_ref[0:fw, :])
    o = o + _dot(gm_ref[...], w_ref[fw:fw + gw, :])
    o = o + _dot(ssd_ref[...], w_ref[fw + gw:, :])
    o_ref[...] = x_ref[...] + _rmsnorm(o, nw_ref[...])


def _outproj(fox_o, gm_o, ssd_o, x, norm_w3, w_out, l, tm):
    m, d = x.shape
    fw, gw, sw = fox_o.shape[1], gm_o.shape[1], ssd_o.shape[1]
    return pl.pallas_call(
        functools.partial(_outproj_body, fw=fw, gw=gw),
        grid=(m // tm,),
        in_specs=[
            pl.BlockSpec((tm, fw), lambda i: (i, 0)),
            pl.BlockSpec((tm, gw), lambda i: (i, 0)),
            pl.BlockSpec((tm, sw), lambda i: (i, 0)),
            pl.BlockSpec((tm, d), lambda i: (i, 0)),
            pl.BlockSpec((None, 1, d), lambda i: (l * 6 + 3, 0, 0)),
            pl.BlockSpec((None, fw + gw + sw, d), lambda i: (l, 0, 0)),
        ],
        out_specs=pl.BlockSpec((tm, d), lambda i: (i, 0)),
        out_shape=jax.ShapeDtypeStruct((m, d), F32),
        compiler_params=_cparams(("parallel",)),
        name=f"outproj_{l}",
    )(fox_o, gm_o, ssd_o, x, norm_w3, w_out)


def _head_rows(vals, bsz, seq):
    return jnp.transpose(vals.reshape(bsz, seq, vals.shape[-1]), (0, 2, 1))


def _pad_lanes(rows, width):
    return jnp.pad(rows, ((0, 0), (0, 0), (0, width - rows.shape[-1])))


def kernel(x_prompt, x_sample, cache_fox_k, cache_fox_v, cache_fox_logf, state_ssd_conv, state_ssd, norm_w,
           w_ffn_gate, w_ffn_up, w_ffn_down, w_in, fox_fb, gm_ln_w, gm_ln_b, gm_ws, gm_bs, ssd_conv_w,
           ssd_conv_b, ssd_dt_bias, ssd_a_log, ssd_d, ssd_norm_w, w_out):
    bsz, seq, d = x_prompt.shape
    dbsz, dseq, _ = x_sample.shape
    depth, _, past, heads, dh = cache_fox_k.shape
    fox_w = heads * dh
    groups, gchunk = gm_ws.shape[1], gm_ws.shape[2]
    gm_w = gm_ln_w.shape[1]
    gch = gm_w // groups
    _, _, sheads, hp, sn = state_ssd.shape
    ssd_w = sheads * hp
    cch = ssd_conv_w.shape[2]
    ngroups = (cch - ssd_w) // (2 * sn)
    conv_k = ssd_conv_w.shape[1]
    n_p, n_s = bsz * seq, dbsz * dseq
    m = n_p + n_s
    assert gchunk == GM_CHUNK and heads == 8 and sheads == 8 and dh == LANES
    assert seq % GM_CHUNK == 0 and GM_CHUNK % dseq == 0 and dseq < SSD_CHUNK and past % LANES == 0

    wg, wu, wd = (w.astype(BF16) for w in (w_ffn_gate, w_ffn_up, w_ffn_down))
    wo = w_out.astype(BF16)
    sizes = [fox_w, fox_w, fox_w, heads, gm_w, gm_w, ssd_w, cch, sheads]
    offs = [0]
    for s in sizes:
        offs.append(offs[-1] + s)
    w_t = jnp.transpose(w_in, (0, 2, 1))
    wq, wk, wv, wf, wgu, wgv, wz, wxbc, wdt = (w_t[:, offs[i]:offs[i + 1], :] for i in range(9))
    tn = 1024
    used = 3 * fox_w + 2 * gm_w + cch + ssd_w + MISC_W
    npad = -(-used // tn) * tn
    zeros_misc = jnp.zeros((depth, MISC_W - heads - 2 * sheads, d), F32)
    zeros_tail = jnp.zeros((depth, npad - used, d), F32)
    w_r = jnp.concatenate([wq, wk, wv, wgu, wgv, wxbc, wz, wf, wdt, wdt, zeros_misc, zeros_tail],
                          axis=1).astype(BF16)
    col_q, col_k, col_v = 0, fox_w // dh, 2 * fox_w // dh
    col_u, col_gv = 3 * fox_w // gm_w, 3 * fox_w // gm_w + 1
    off_xbc = 3 * fox_w + 2 * gm_w
    off_z = off_xbc + cch
    off_misc = off_z + ssd_w
    assert off_xbc % cch == 0 and off_z % ssd_w == 0 and off_misc % MISC_W == 0
    misc_in_tile = off_misc % tn
    assert off_misc // tn == npad // tn - 1 and misc_in_tile + MISC_W <= tn

    lane_pad = jnp.zeros((depth, MISC_W - heads - 2 * sheads), F32)
    bias_vec = jnp.concatenate([fox_fb, ssd_dt_bias, ssd_dt_bias, lane_pad], axis=-1)[:, None, :]
    alog_vec = jnp.concatenate([jnp.zeros((depth, heads + sheads), F32), ssd_a_log, lane_pad], axis=-1)[:, None, :]
    norm_w3 = norm_w.reshape(depth * 6, 1, d)
    lnw3, lnb3 = gm_ln_w[:, None, :], gm_ln_b[:, None, :]
    reps = GM_CHUNK // dseq
    eye = jnp.eye(reps, dtype=F32)
    ws_s = jnp.einsum('ab,lgts->lgatbs', eye, gm_ws[:, :, :dseq, :dseq]).reshape(depth, groups, GM_CHUNK, GM_CHUNK)
    bs_p = jnp.transpose(gm_bs, (0, 2, 1))
    bs_s = jnp.tile(jnp.transpose(gm_bs[:, :, :dseq], (0, 2, 1)), (1, reps, 1))
    conv_b3 = ssd_conv_b[:, None, :]
    dvec = jnp.repeat(ssd_d, hp, axis=-1)[:, None, :]
    snw3 = ssd_norm_w[:, None, :]
    cache_k, cache_v = cache_fox_k, cache_fox_v
    conv_prev_p = jnp.zeros((bsz, 8, cch), F32)
    conv_prev_s = jnp.pad(state_ssd_conv, ((0, 0), (0, 0), (8 - (conv_k - 1), 0), (0, 0)))
    s0_p = jnp.zeros((bsz, sheads, hp, sn), F32)

    x = jnp.concatenate([x_prompt.reshape(n_p, d), x_sample.reshape(n_s, d)], axis=0)
    tm_big = m // 8 if m % 64 == 0 and (m // 8) % 16 == 0 else 256
    tm_ffn = m // 12 if m % 12 == 0 and (m // 12) % 16 == 0 else 256
    tm_out = m // 16 if m % 16 == 0 and (m // 16) % 16 == 0 else 256
    lf_len = 2 * past
    assert past + dseq <= lf_len

    outs = {k: [] for k in ("plf", "pconv", "pst", "slf", "sgm", "sconv", "sst")}
    kv_p = kv_s = None
    for l in range(depth):
        x = _ffn(x, norm_w3, wg, wu, wd, l, 0, tm_ffn, 512)
        proj = _inproj(x, norm_w3, w_r, bias_vec, alog_vec, l, tm_big, tn, misc_in_tile)
        misc = proj[:, off_misc:off_misc + heads + 2 * sheads]
        logf_p, logf_s = misc[:n_p, :heads], misc[n_p:, :heads]
        lf_rows_p = _head_rows(logf_p, bsz, seq).reshape(bsz, heads, seq // LANES, LANES)
        lf_all = jnp.concatenate([cache_fox_logf[l], logf_s.reshape(dbsz, dseq, heads),
                                  jnp.zeros((dbsz, lf_len - past - dseq, heads), F32)], axis=1)
        lf_rows_s = jnp.transpose(lf_all, (0, 2, 1)).reshape(dbsz, heads, lf_len // LANES, LANES)
        dt_rows_p = _head_rows(misc[:n_p, heads:heads + sheads], bsz, seq)
        a_rows_p = _head_rows(misc[:n_p, heads + sheads:], bsz, seq)
        dt_rows_s = _pad_lanes(_head_rows(misc[n_p:, heads:heads + sheads], dbsz, dseq), LANES)
        a_rows_s = _pad_lanes(_head_rows(misc[n_p:, heads + sheads:], dbsz, dseq), LANES)

        fox_p, *kv_p = _fox_prompt(proj, lf_rows_p, kv_p, l, depth, bsz, seq, heads, dh, col_q, col_k, col_v, 256)
        fox_s, *kv_s = _fox_sample(proj, lf_rows_s, cache_k, cache_v, kv_s, l, n_p, dbsz, dseq, heads, dh, 0, 1, 2)
        gm_o, gm_v_s = _gmlp(proj, lnw3, lnb3, gm_ws, ws_s, bs_p, bs_s, l, n_p, groups, gch, col_u, col_gv, 256)
        ssd_p, conv_p, st_p = _ssd(proj, dt_rows_p, a_rows_p, conv_prev_p, s0_p, ssd_conv_w, conv_b3, dvec, snw3,
                                   l, 0, bsz, seq, 256, SSD_CHUNK, sheads, hp, ngroups, sn,
                                   off_xbc // cch, off_z // ssd_w, off_misc // MISC_W)
        ssd_s, conv_s, st_s = _ssd(proj, dt_rows_s, a_rows_s, conv_prev_s[l], state_ssd[l], ssd_conv_w, conv_b3,
                                   dvec, snw3, l, n_p, dbsz, dseq, dseq, dseq, sheads, hp, ngroups, sn,
                                   off_xbc // cch, off_z // ssd_w, off_misc // MISC_W)
        fox_o = jnp.concatenate([fox_p, fox_s], axis=0)
        ssd_o = jnp.concatenate([ssd_p, ssd_s], axis=0)
        x = _outproj(fox_o, gm_o, ssd_o, x, norm_w3, wo, l, tm_out)
        x = _ffn(x, norm_w3, wg, wu, wd, l, 1, tm_ffn, 512)

        outs["plf"].append(logf_p.reshape(bsz, seq, heads))
        outs["pconv"].append(conv_p)
        outs["pst"].append(st_p)
        outs["slf"].append(logf_s.reshape(dbsz, dseq, heads))
        outs["sgm"].append(gm_v_s.reshape(dbsz, dseq, gm_w))
        outs["sconv"].append(conv_s)
        outs["sst"].append(st_s)

    st = {k: jnp.stack(v) for k, v in outs.items()}
    y_prompt = x[:n_p].reshape(bsz, seq, d)
    y_sample = x[n_p:].reshape(dbsz, dseq, d)
    p_k, p_v = (a.reshape(depth, bsz, seq, heads, dh) for a in kv_p)
    s_k, s_v = (a.reshape(depth, dbsz, dseq, heads, dh) for a in kv_s)
    return (y_prompt, y_sample, p_k, p_v, st["plf"], st["pconv"], st["pst"],
            s_k, s_v, st["slf"], st["sgm"], st["sconv"], st["sst"])
```

```python
import functools

import jax
import jax.numpy as jnp
from jax import lax
from jax.experimental import pallas as pl
from jax.experimental.pallas import tpu as pltpu

F32 = jnp.float32
BF16 = jnp.bfloat16
EPS = 1e-6
LANES = 128
SSD_CHUNK = 64
GM_CHUNK = 128
VMEM_LIMIT = 56 * 1024 * 1024

MISC_W = LANES


def _cparams(sem):
    return pltpu.CompilerParams(dimension_semantics=sem, vmem_limit_bytes=VMEM_LIMIT)


def _rmsnorm(x, w):
    return x * lax.rsqrt(jnp.mean(x * x, axis=-1, keepdims=True) + EPS) * w


def _softplus(x):
    return jnp.maximum(x, 0.0) + jnp.log1p(jnp.exp(-jnp.abs(x)))


def _dot(a, b):
    return jnp.dot(a, b, preferred_element_type=F32)


def _dot_nt(a, b):
    return lax.dot_general(a, b, (((1,), (1,)), ((), ())), preferred_element_type=F32)


def _dot_tn(a, b):
    return lax.dot_general(a, b, (((0,), (0,)), ((), ())), preferred_element_type=F32)


def _row_to_col(row):
    n = row.shape[1]
    eye = lax.broadcasted_iota(jnp.int32, (n, n), 0) == lax.broadcasted_iota(jnp.int32, (n, n), 1)
    return jnp.sum(jnp.where(eye, jnp.broadcast_to(row, (n, n)), 0.0), axis=1, keepdims=True)


def _lane_cumsum(x, seg):
    lane = lax.broadcasted_iota(jnp.int32, x.shape, 1) & (seg - 1)
    k = 1
    while k < seg:
        x = x + jnp.where(lane >= k, pltpu.roll(x, k, axis=1), 0.0)
        k *= 2
    return x


def _flat_cumsum(x):
    x = _lane_cumsum(x, LANES)
    tot = x[:, LANES - 1:LANES]
    rows = []
    carry = jnp.zeros((1, 1), F32)
    for r in range(x.shape[0]):
        rows.append(x[r:r + 1, :] + carry)
        carry = carry + tot[r:r + 1, :]
    return jnp.concatenate(rows, axis=0)


def _ffn_body(x_ref, nwa_ref, nwb_ref, wg_ref, wu_ref, wd_ref, o_ref, h_ref):
    f = pl.program_id(1)

    @pl.when(f == 0)
    def _():
        h_ref[...] = _rmsnorm(x_ref[...], nwa_ref[...]).astype(BF16)
        o_ref[...] = jnp.zeros_like(o_ref)

    h = h_ref[...]
    a = _dot(h, wg_ref[...])
    g = _dot(h, wu_ref[...])
    act = (a * jax.nn.sigmoid(a) * g).astype(BF16)
    o_ref[...] += _dot(act, wd_ref[...].astype(BF16))

    @pl.when(f == pl.num_programs(1) - 1)
    def _():
        o_ref[...] = x_ref[...] + 0.5 * _rmsnorm(o_ref[...], nwb_ref[...])


def _ffn(x, norm_w3, wg, wu, wd, l, j, tm, tf):
    m, d = x.shape
    dff = wg.shape[-1]
    return pl.pallas_call(
        _ffn_body,
        grid=(m // tm, dff // tf),
        in_specs=[
            pl.BlockSpec((tm, d), lambda i, f: (i, 0)),
            pl.BlockSpec((None, 1, d), lambda i, f: (l * 6 + 3 * j + j, 0, 0)),
            pl.BlockSpec((None, 1, d), lambda i, f: (l * 6 + 3 * j + j + 1, 0, 0)),
            pl.BlockSpec((None, None, d, tf), lambda i, f: (l, j, 0, f)),
            pl.BlockSpec((None, None, d, tf), lambda i, f: (l, j, 0, f)),
            pl.BlockSpec((None, None, tf, d), lambda i, f: (l, j, f, 0)),
        ],
        out_specs=pl.BlockSpec((tm, d), lambda i, f: (i, 0)),
        out_shape=jax.ShapeDtypeStruct((m, d), F32),
        scratch_shapes=[pltpu.VMEM((tm, d), BF16)],
        compiler_params=_cparams(("parallel", "arbitrary")),
        name=f"ffn_{l}_{j}",
    )(x, norm_w3, norm_w3, wg, wu, wd)


def _inproj_body(x_ref, nw_ref, w_ref, bias_ref, alog_ref, o_ref, h_ref, *, misc_off):
    j = pl.program_id(1)

    @pl.when(j == 0)
    def _():
        h_ref[...] = _rmsnorm(x_ref[...], nw_ref[...]).astype(BF16)

    acc = _dot_nt(h_ref[...], w_ref[...])
    o_ref[...] = acc

    @pl.when(j == pl.num_programs(1) - 1)
    def _():
        raw = acc[:, misc_off:misc_off + MISC_W] + bias_ref[...]
        lane = lax.broadcasted_iota(jnp.int32, raw.shape, 1)
        sp = _softplus(raw)
        logf = -_softplus(-raw)
        neg_a = -jnp.exp(alog_ref[...])
        val = jnp.where(lane < 8, logf, jnp.where(lane < 16, sp, jnp.where(lane < 24, sp * neg_a, 0.0)))
        o_ref[:, misc_off:misc_off + MISC_W] = val


def _inproj(x, norm_w3, w_r, bias_vec, alog_vec, l, tm, tn, misc_off):
    m, d = x.shape
    npad = w_r.shape[1]
    return pl.pallas_call(
        functools.partial(_inproj_body, misc_off=misc_off),
        grid=(m // tm, npad // tn),
        in_specs=[
            pl.BlockSpec((tm, d), lambda i, j: (i, 0)),
            pl.BlockSpec((None, 1, d), lambda i, j: (l * 6 + 2, 0, 0)),
            pl.BlockSpec((None, tn, d), lambda i, j: (l, j, 0)),
            pl.BlockSpec((None, 1, MISC_W), lambda i, j: (l, 0, 0)),
            pl.BlockSpec((None, 1, MISC_W), lambda i, j: (l, 0, 0)),
        ],
        out_specs=pl.BlockSpec((tm, tn), lambda i, j: (i, j)),
        out_shape=jax.ShapeDtypeStruct((m, npad), F32),
        scratch_shapes=[pltpu.VMEM((tm, d), BF16)],
        compiler_params=_cparams(("parallel", "arbitrary")),
        name=f"inproj_{l}",
    )(x, norm_w3, w_r, bias_vec, alog_vec)


def _rows_as_lane_vector(c_ref, start, count):
    rows = c_ref[pl.ds(start, count), :]
    return jnp.concatenate([rows[r:r + 1, :] for r in range(count)], axis=1)


def _store_layer_rows(out_ref, val, l, aliased):
    if aliased:
        out_ref[...] = val
    else:
        for slot in range(out_ref.shape[0]):
            out_ref[slot] = val if slot == l else jnp.zeros_like(val)


def _split3(x):
    hi = x.astype(BF16).astype(F32)
    r = x - hi
    mid = r.astype(BF16).astype(F32)
    return hi, mid, r - mid


def _fox_prompt_body(lf_ref, q_ref, k_ref, v_ref, *rest, tq, scale, l, aliased):
    o_ref, ko_ref, vo_ref, c_scr, qx_scr, kx_scr, vb_scr, s_scr, p_scr = rest[2 * aliased:]
    seq, dh = q_ref.shape
    nq = seq // tq
    sub = tq // LANES
    log2e = 1.4426950408889634
    c_scr[...] = _flat_cumsum(lf_ref[...]) * log2e
    lane = lax.broadcasted_iota(jnp.int32, (tq, LANES), 1)

    def prepare(i):
        rows = slice(i * tq, (i + 1) * tq)
        hi, mid, lo = _split3(_row_to_col(_rows_as_lane_vector(c_scr, i * sub, sub)))
        q_ext = jnp.where(lane == 0, hi, jnp.where(lane == 1, mid, jnp.where(lane == 2, lo,
                          jnp.where(lane < 6, 1.0, 0.0))))
        k_ext = jnp.where(lane < 3, 1.0, jnp.where(lane == 3, -hi, jnp.where(lane == 4, -mid,
                          jnp.where(lane == 5, -lo, 0.0))))
        k_rows, v_rows = k_ref[rows, :], v_ref[rows, :]
        qx_scr[rows, 0:dh] = (q_ref[rows, :] * (scale * log2e)).astype(BF16)
        qx_scr[rows, dh:dh + LANES] = q_ext.astype(BF16)
        kx_scr[rows, 0:dh] = k_rows.astype(BF16)
        kx_scr[rows, dh:dh + LANES] = k_ext.astype(BF16)
        vb_scr[rows, :] = v_rows.astype(BF16)
        if aliased:
            ko_ref[rows, :] = k_rows
            vo_ref[rows, :] = v_rows
        else:
            for slot in range(ko_ref.shape[0]):
                ko_ref[slot, rows, :] = k_rows if slot == l else jnp.zeros_like(k_rows)
                vo_ref[slot, rows, :] = v_rows if slot == l else jnp.zeros_like(v_rows)

    causal = lax.broadcasted_iota(jnp.int32, (tq, tq), 1) <= lax.broadcasted_iota(jnp.int32, (tq, tq), 0)
    slab = 64
    n_slabs = tq // slab

    def scores(i, j):
        s = _dot_nt(qx_scr[i * tq:(i + 1) * tq, :], kx_scr[j * tq:(j + 1) * tq, :])
        if j == i:
            s = jnp.where(causal, s, -jnp.inf)
        s_scr[i % 2, :, j * tq:(j + 1) * tq] = s

    def softmax_slab(i, r):
        nk = (i + 1) * tq
        rs = slice(r * slab, (r + 1) * slab)
        m_acc = s_scr[i % 2, rs, 0:LANES]
        for u in range(1, nk // LANES):
            m_acc = jnp.maximum(m_acc, s_scr[i % 2, rs, u * LANES:(u + 1) * LANES])
        m = jnp.broadcast_to(jnp.max(m_acc, axis=1, keepdims=True), (slab, LANES))
        l_acc = jnp.zeros((slab, LANES), F32)
        for u in range(nk // LANES):
            p = jnp.exp2(s_scr[i % 2, rs, u * LANES:(u + 1) * LANES] - m)
            l_acc = l_acc + p
            p_scr[i % 2, rs, u * LANES:(u + 1) * LANES] = p.astype(BF16)
        return jnp.sum(l_acc, axis=1, keepdims=True)

    def weighted_values(i, denom):
        nk = (i + 1) * tq
        acc = _dot(p_scr[i % 2, :, 0:nk], vb_scr[0:nk, :])
        o_ref[i * tq:(i + 1) * tq, :] = (acc / jnp.concatenate(denom, axis=0)).astype(o_ref.dtype)

    prepare(0)
    scores(0, 0)
    denom_prev = None
    for i in range(nq):
        next_keys = list(range(i + 2)) if i + 1 < nq else []
        if next_keys:
            prepare(i + 1)
        denom = []
        for r in range(n_slabs):
            for j in next_keys[r::n_slabs]:
                scores(i + 1, j)
            if r == 1 and denom_prev is not None:
                weighted_values(i - 1, denom_prev)
            denom.append(softmax_slab(i, r))
        denom_prev = denom
    weighted_values(nq - 1, denom_prev)


def _fox_prompt(proj, lf_rows, kv_prev, l, depth, bsz, seq, heads, dh, col_q, col_k, col_v, tq):
    n = seq // LANES
    aliased = kv_prev is not None
    kv_shape = jax.ShapeDtypeStruct((depth, bsz * seq, heads * dh), F32)
    kv_spec = (pl.BlockSpec((None, seq, dh), lambda b, h: (l, b, h)) if aliased else
               pl.BlockSpec((depth, seq, dh), lambda b, h: (0, b, h)))
    in_specs = [
        pl.BlockSpec((None, None, n, LANES), lambda b, h: (b, h, 0, 0)),
        pl.BlockSpec((seq, dh), lambda b, h: (b, col_q + h)),
        pl.BlockSpec((seq, dh), lambda b, h: (b, col_k + h)),
        pl.BlockSpec((seq, dh), lambda b, h: (b, col_v + h)),
    ]
    args = [lf_rows, proj, proj, proj]
    if aliased:
        in_specs += [pl.BlockSpec(memory_space=pl.ANY)] * 2
        args += list(kv_prev)
    return pl.pallas_call(
        functools.partial(_fox_prompt_body, tq=tq, scale=dh ** -0.5, l=l, aliased=aliased),
        grid=(bsz, heads),
        in_specs=in_specs,
        out_specs=[pl.BlockSpec((seq, dh), lambda b, h: (b, h)), kv_spec, kv_spec],
        out_shape=[jax.ShapeDtypeStruct((bsz * seq, heads * dh), BF16), kv_shape, kv_shape],
        scratch_shapes=[pltpu.VMEM((n, LANES), F32), pltpu.VMEM((seq, dh + LANES), BF16),
                        pltpu.VMEM((seq, dh + LANES), BF16), pltpu.VMEM((seq, dh), BF16),
                        pltpu.VMEM((2, tq, seq), F32), pltpu.VMEM((2, tq, seq), BF16)],
        input_output_aliases={4: 1, 5: 2} if aliased else {},
        compiler_params=_cparams(("parallel", "parallel")),
        name="fox_prompt",
    )(*args)


def _fox_sample_body(lf_ref, q_ref, k_ref, v_ref, ck_ref, cv_ref, *rest, past, new, heads, dh, scale, l,
                     aliased):
    o_ref, ko_ref, vo_ref = rest[2 * aliased:]
    _store_layer_rows(ko_ref, k_ref[...], l, aliased)
    _store_layer_rows(vo_ref, v_ref[...], l, aliased)
    n_past = past // LANES
    causal = lax.broadcasted_iota(jnp.int32, (new, new), 1) <= lax.broadcasted_iota(jnp.int32, (new, new), 0)
    for h in range(heads):
        cols = slice(h * dh, (h + 1) * dh)
        c = _flat_cumsum(lf_ref[h])
        c_past = jnp.concatenate([c[r:r + 1, :] for r in range(n_past)], axis=1)
        c_new = c[n_past:n_past + 1, 0:new]
        cq_col = _row_to_col(c_new)
        q = q_ref[:, cols].astype(BF16)
        s_past = _dot_nt(q, ck_ref[:, h, :].astype(BF16)) * scale + cq_col - c_past
        s_new = _dot_nt(q, k_ref[:, cols].astype(BF16)) * scale + cq_col - c_new
        s_new = jnp.where(causal, s_new, -jnp.inf)
        m = jnp.maximum(jnp.max(s_past, axis=1, keepdims=True), jnp.max(s_new, axis=1, keepdims=True))
        p_past = jnp.exp(s_past - m)
        p_new = jnp.exp(s_new - m)
        denom = jnp.sum(p_past, axis=1, keepdims=True) + jnp.sum(p_new, axis=1, keepdims=True)
        acc = _dot(p_past.astype(BF16), cv_ref[:, h, :].astype(BF16)) + _dot(
            p_new.astype(BF16), v_ref[:, cols].astype(BF16))
        o_ref[:, cols] = (acc / denom).astype(o_ref.dtype)


def _fox_sample(proj, lf_rows, cache_k, cache_v, kv_prev, l, row0, bsz, new, heads, dh, col_q, col_k, col_v):
    depth, _, past = cache_k.shape[:3]
    fw = heads * dh
    n = lf_rows.shape[2]
    rb0 = row0 // new
    aliased = kv_prev is not None
    kv_shape = jax.ShapeDtypeStruct((depth, bsz * new, fw), F32)
    kv_spec = (pl.BlockSpec((None, new, fw), lambda b: (l, b, 0)) if aliased else
               pl.BlockSpec((depth, new, fw), lambda b: (0, b, 0)))
    in_specs = [
        pl.BlockSpec((None, heads, n, LANES), lambda b: (b, 0, 0, 0)),
        pl.BlockSpec((new, fw), lambda b: (rb0 + b, col_q)),
        pl.BlockSpec((new, fw), lambda b: (rb0 + b, col_k)),
        pl.BlockSpec((new, fw), lambda b: (rb0 + b, col_v)),
        pl.BlockSpec((None, None, past, heads, dh), lambda b: (l, b, 0, 0, 0)),
        pl.BlockSpec((None, None, past, heads, dh), lambda b: (l, b, 0, 0, 0)),
    ]
    args = [lf_rows, proj, proj, proj, cache_k, cache_v]
    if aliased:
        in_specs += [pl.BlockSpec(memory_space=pl.ANY)] * 2
        args += list(kv_prev)
    return pl.pallas_call(
        functools.partial(_fox_sample_body, past=past, new=new, heads=heads, dh=dh, scale=dh ** -0.5,
                          l=l, aliased=aliased),
        grid=(bsz,),
        in_specs=in_specs,
        out_specs=[pl.BlockSpec((new, fw), lambda b: (b, 0)), kv_spec, kv_spec],
        out_shape=[jax.ShapeDtypeStruct((bsz * new, fw), BF16), kv_shape, kv_shape],
        input_output_aliases={6: 1, 7: 2} if aliased else {},
        compiler_params=_cparams(("parallel",)),
        name="fox_sample",
    )(*args)


def _gmlp_body(gu_ref, gv_ref, lnw_ref, lnb_ref, wp_ref, ws_ref, bp_ref, bs_ref, o_ref, v_ref, *,
               n_prompt_tiles, groups, gch):
    i = pl.program_id(0)
    is_sample = i >= n_prompt_tiles
    u = jax.nn.gelu(gu_ref[...])
    gv = jax.nn.gelu(gv_ref[...])
    mu = jnp.mean(gv, axis=-1, keepdims=True)
    var = jnp.mean(jnp.square(gv - mu), axis=-1, keepdims=True)
    v = (gv - mu) * lax.rsqrt(var + EPS) * lnw_ref[...] + lnb_ref[...]

    @pl.when(is_sample)
    def _():
        v_ref[...] = v

    tm = u.shape[0]
    tril = lax.broadcasted_iota(jnp.int32, (GM_CHUNK, GM_CHUNK), 1) <= lax.broadcasted_iota(
        jnp.int32, (GM_CHUNK, GM_CHUNK), 0)
    bias_all = jnp.where(is_sample, bs_ref[...], bp_ref[...])
    for g in range(groups):
        wm = jnp.where(is_sample, ws_ref[g], wp_ref[g])
        wm = jnp.where(tril, wm, 0.0).astype(BF16)
        bias = bias_all[:, g:g + 1]
        for c in range(tm // GM_CHUNK):
            rows = slice(c * GM_CHUNK, (c + 1) * GM_CHUNK)
            cols = slice(g * gch, (g + 1) * gch)
            sp = _dot(wm, v[rows, cols].astype(BF16)) + bias
            o_ref[rows, cols] = (u[rows, cols] * sp).astype(o_ref.dtype)


def _gmlp(proj, lnw, lnb, w_prompt, w_sample, b_prompt, b_sample, l, n_prompt_rows, groups, gch,
          col_u, col_v, tm):
    m = proj.shape[0]
    gw = groups * gch
    npt = n_prompt_rows // tm
    return pl.pallas_call(
        functools.partial(_gmlp_body, n_prompt_tiles=npt, groups=groups, gch=gch),
        grid=(m // tm,),
        in_specs=[
            pl.BlockSpec((tm, gw), lambda i: (i, col_u)),
            pl.BlockSpec((tm, gw), lambda i: (i, col_v)),
            pl.BlockSpec((None, 1, gw), lambda i: (l, 0, 0)),
            pl.BlockSpec((None, 1, gw), lambda i: (l, 0, 0)),
            pl.BlockSpec((None, groups, GM_CHUNK, GM_CHUNK), lambda i: (l, 0, 0, 0)),
            pl.BlockSpec((None, groups, GM_CHUNK, GM_CHUNK), lambda i: (l, 0, 0, 0)),
            pl.BlockSpec((None, GM_CHUNK, groups), lambda i: (l, 0, 0)),
            pl.BlockSpec((None, GM_CHUNK, groups), lambda i: (l, 0, 0)),
        ],
        out_specs=[
            pl.BlockSpec((tm, gw), lambda i: (i, 0)),
            pl.BlockSpec((tm, gw), lambda i: (jnp.maximum(i - npt, 0), 0)),
        ],
        out_shape=[jax.ShapeDtypeStruct((m, gw), BF16), jax.ShapeDtypeStruct((m - n_prompt_rows, gw), F32)],
        compiler_params=_cparams(("arbitrary",)),
        name="gmlp",
    )(proj, proj, lnw, lnb, w_prompt, w_sample, b_prompt, b_sample)


def _ssd_body(xbc_ref, z_ref, misc_ref, dtrow_ref, arow_ref, prev_ref, s0_ref, cw_ref, cb_ref, dvec_ref,
              nw_ref, o_ref, conv_ref, st_ref, xp_scr, s_scr, y_scr, *, lt, q, heads, hp, ngroups, sn, conv_k):
    t = pl.program_id(1)
    nt = pl.num_programs(1)
    xw = heads * hp
    pad = 8

    @pl.when(t == 0)
    def _():
        xp_scr[0:pad, :] = prev_ref[...]
        s_scr[...] = s0_ref[...]

    xbc = xbc_ref[...]
    xp_scr[pad:pad + lt, :] = xbc
    acc = cb_ref[...]
    for i in range(conv_k):
        off = pad - (conv_k - 1) + i
        acc = acc + xp_scr[off:off + lt, :] * cw_ref[i:i + 1, :]
    xp_scr[0:pad, :] = xbc[lt - pad:lt, :]
    act = acc * jax.nn.sigmoid(acc)

    @pl.when(t == nt - 1)
    def _():
        conv_ref[...] = xbc[lt - (conv_k - 1):lt, :]

    dt_rows = dtrow_ref[...]
    ltp = dt_rows.shape[1]
    a_rows = jnp.concatenate(
        [_lane_cumsum(arow_ref[:, j * LANES:(j + 1) * LANES], q) for j in range(ltp // LANES)], axis=1)
    row = lax.broadcasted_iota(jnp.int32, (lt, lt), 0)
    col = lax.broadcasted_iota(jnp.int32, (lt, lt), 1)
    tri_all = jnp.where((col <= row) & (col >= (row & -q)), 1.0, 0.0).astype(BF16)
    a_cols = sum(_dot(tri_all, part.astype(BF16)) for part in _split3(misc_ref[...]))
    x_all = act[:, 0:xw]
    if lt < LANES:
        x_all = jnp.concatenate([x_all, jnp.zeros((LANES - lt, xw), F32)], axis=0)
    x_t = x_all.T
    tril = lax.broadcasted_iota(jnp.int32, (q, q), 1) <= lax.broadcasted_iota(jnp.int32, (q, q), 0)
    rep = heads // ngroups
    for c in range(lt // q):
        r0 = c * q
        bmat = [act[r0:r0 + q, xw + g * sn: xw + (g + 1) * sn].astype(BF16) for g in range(ngroups)]
        cmat = [act[r0:r0 + q, xw + (ngroups + g) * sn: xw + (ngroups + g + 1) * sn].astype(BF16)
                for g in range(ngroups)]
        gmat = [_dot_nt(cmat[g], bmat[g]) for g in range(ngroups)]
        for h in range(heads):
            g = h // rep
            a_row = a_rows[h:h + 1, r0:r0 + q]
            dt_row = dt_rows[h:h + 1, r0:r0 + q]
            a_col = a_cols[r0:r0 + q, 16 + h:17 + h]
            a_last = a_row[:, q - 1:q]
            lmat = jnp.exp(jnp.where(tril, a_col - a_row, -jnp.inf))
            scores = gmat[g] * lmat * dt_row
            xh = act[r0:r0 + q, h * hp:(h + 1) * hp]
            y = _dot(scores.astype(BF16), xh.astype(BF16))
            decay_end = jnp.exp(a_last - a_row) * dt_row
            xd_t = (x_t[h * hp:(h + 1) * hp, r0:r0 + q] * decay_end).astype(BF16)
            chunk_state = _dot(xd_t, bmat[g])
            s_in = s_scr[h]
            y = y + _dot_nt(cmat[g], s_in.astype(BF16)) * jnp.exp(a_col)
            s_scr[h] = s_in * jnp.exp(a_last) + chunk_state
            y_scr[r0:r0 + q, h * hp:(h + 1) * hp] = y + dvec_ref[:, h * hp:(h + 1) * hp] * xh

    zz = z_ref[...]
    o_ref[...] = _rmsnorm(y_scr[...] * (zz * jax.nn.sigmoid(zz)), nw_ref[...]).astype(o_ref.dtype)

    @pl.when(t == nt - 1)
    def _():
        st_ref[...] = s_scr[...]


def _ssd(proj, dt_rows, a_rows, conv_prev, s0, conv_w, conv_b, dvec, norm_w, l, row0, bsz, seq, lt, q,
         heads, hp, ngroups, sn, col_xbc, col_z, col_misc):
    nt = seq // lt
    xw = heads * hp
    cch = xw + 2 * ngroups * sn
    conv_k = conv_w.shape[1]
    rb0 = row0 // lt
    ltp = max(lt, LANES)
    body = functools.partial(_ssd_body, lt=lt, q=q, heads=heads, hp=hp, ngroups=ngroups, sn=sn, conv_k=conv_k)
    return pl.pallas_call(
        body,
        grid=(bsz, nt),
        in_specs=[
            pl.BlockSpec((lt, cch), lambda b, t: (rb0 + b * nt + t, col_xbc)),
            pl.BlockSpec((lt, xw), lambda b, t: (rb0 + b * nt + t, col_z)),
            pl.BlockSpec((lt, MISC_W), lambda b, t: (rb0 + b * nt + t, col_misc)),
            pl.BlockSpec((None, heads, ltp), lambda b, t: (b, 0, t)),
            pl.BlockSpec((None, heads, ltp), lambda b, t: (b, 0, t)),
            pl.BlockSpec((None, 8, cch), lambda b, t: (b, 0, 0)),
            pl.BlockSpec((None, heads, hp, sn), lambda b, t: (b, 0, 0, 0)),
            pl.BlockSpec((None, conv_k, cch), lambda b, t: (l, 0, 0)),
            pl.BlockSpec((None, 1, cch), lambda b, t: (l, 0, 0)),
            pl.BlockSpec((None, 1, xw), lambda b, t: (l, 0, 0)),
            pl.BlockSpec((None, 1, xw), lambda b, t: (l, 0, 0)),
        ],
        out_specs=[
            pl.BlockSpec((lt, xw), lambda b, t: (b * nt + t, 0)),
            pl.BlockSpec((None, conv_k - 1, cch), lambda b, t: (b, 0, 0)),
            pl.BlockSpec((None, heads, hp, sn), lambda b, t: (b, 0, 0, 0)),
        ],
        out_shape=[
            jax.ShapeDtypeStruct((bsz * seq, xw), BF16),
            jax.ShapeDtypeStruct((bsz, conv_k - 1, cch), F32),
            jax.ShapeDtypeStruct((bsz, heads, hp, sn), F32),
        ],
        scratch_shapes=[pltpu.VMEM((lt + 8, cch), F32), pltpu.VMEM((heads, hp, sn), F32),
                        pltpu.VMEM((lt, xw), F32)],
        compiler_params=_cparams(("parallel", "arbitrary")),
        name="ssd",
    )(proj, proj, proj, dt_rows, a_rows, conv_prev, s0, conv_w, conv_b, dvec, norm_w)


def _outproj_body(fox_ref, gm_ref, ssd_ref, x_ref, nw_ref, w_ref, o_ref, *, fw, gw):
    o = _dot(fox_ref[...], w_ref[0:fw, :])
    o = o + _dot(gm_ref[...], w_ref[fw:fw + gw, :])
    o = o + _dot(ssd_ref[...], w_ref[fw + gw:, :])
    o_ref[...] = x_ref[...] + _rmsnorm(o, nw_ref[...])


def _outproj(fox_o, gm_o, ssd_o, x, norm_w3, w_out, l, tm):
    m, d = x.shape
    fw, gw, sw = fox_o.shape[1], gm_o.shape[1], ssd_o.shape[1]
    return pl.pallas_call(
        functools.partial(_outproj_body, fw=fw, gw=gw),
        grid=(m // tm,),
        in_specs=[
            pl.BlockSpec((tm, fw), lambda i: (i, 0)),
            pl.BlockSpec((tm, gw), lambda i: (i, 0)),
            pl.BlockSpec((tm, sw), lambda i: (i, 0)),
            pl.BlockSpec((tm, d), lambda i: (i, 0)),
            pl.BlockSpec((None, 1, d), lambda i: (l * 6 + 3, 0, 0)),
            pl.BlockSpec((None, fw + gw + sw, d), lambda i: (l, 0, 0)),
        ],
        out_specs=pl.BlockSpec((tm, d), lambda i: (i, 0)),
        out_shape=jax.ShapeDtypeStruct((m, d), F32),
        compiler_params=_cparams(("parallel",)),
        name=f"outproj_{l}",
    )(fox_o, gm_o, ssd_o, x, norm_w3, w_out)


def _head_rows(vals, bsz, seq):
    return jnp.transpose(vals.reshape(bsz, seq, vals.shape[-1]), (0, 2, 1))


def _pad_lanes(rows, width):
    return jnp.pad(rows, ((0, 0), (0, 0), (0, width - rows.shape[-1])))


def kernel(x_prompt, x_sample, cache_fox_k, cache_fox_v, cache_fox_logf, state_ssd_conv, state_ssd, norm_w,
           w_ffn_gate, w_ffn_up, w_ffn_down, w_in, fox_fb, gm_ln_w, gm_ln_b, gm_ws, gm_bs, ssd_conv_w,
           ssd_conv_b, ssd_dt_bias, ssd_a_log, ssd_d, ssd_norm_w, w_out):
    bsz, seq, d = x_prompt.shape
    dbsz, dseq, _ = x_sample.shape
    depth, _, past, heads, dh = cache_fox_k.shape
    fox_w = heads * dh
    groups, gchunk = gm_ws.shape[1], gm_ws.shape[2]
    gm_w = gm_ln_w.shape[1]
    gch = gm_w // groups
    _, _, sheads, hp, sn = state_ssd.shape
    ssd_w = sheads * hp
    cch = ssd_conv_w.shape[2]
    ngroups = (cch - ssd_w) // (2 * sn)
    conv_k = ssd_conv_w.shape[1]
    n_p, n_s = bsz * seq, dbsz * dseq
    m = n_p + n_s
    assert gchunk == GM_CHUNK and heads == 8 and sheads == 8 and dh == LANES
    assert seq % GM_CHUNK == 0 and GM_CHUNK % dseq == 0 and dseq < SSD_CHUNK and past % LANES == 0

    wg, wu, wd = w_ffn_gate.astype(BF16), w_ffn_up.astype(BF16), w_ffn_down
    wo = w_out.astype(BF16)
    sizes = [fox_w, fox_w, fox_w, heads, gm_w, gm_w, ssd_w, cch, sheads]
    offs = [0]
    for s in sizes:
        offs.append(offs[-1] + s)
    w_t = jnp.transpose(w_in, (0, 2, 1))
    wq, wk, wv, wf, wgu, wgv, wz, wxbc, wdt = (w_t[:, offs[i]:offs[i + 1], :] for i in range(9))
    tn = 1024
    used = 3 * fox_w + 2 * gm_w + cch + ssd_w + MISC_W
    npad = -(-used // tn) * tn
    zeros_misc = jnp.zeros((depth, MISC_W - heads - 2 * sheads, d), F32)
    zeros_tail = jnp.zeros((depth, npad - used, d), F32)
    w_r = jnp.concatenate([wq, wk, wv, wgu, wgv, wxbc, wz, wf, wdt, wdt, zeros_misc, zeros_tail],
                          axis=1).astype(BF16)
    col_q, col_k, col_v = 0, fox_w // dh, 2 * fox_w // dh
    col_u, col_gv = 3 * fox_w // gm_w, 3 * fox_w // gm_w + 1
    off_xbc = 3 * fox_w + 2 * gm_w
    off_z = off_xbc + cch
    off_misc = off_z + ssd_w
    assert off_xbc % cch == 0 and off_z % ssd_w == 0 and off_misc % MISC_W == 0
    misc_in_tile = off_misc % tn
    assert off_misc // tn == npad // tn - 1 and misc_in_tile + MISC_W <= tn

    lane_pad = jnp.zeros((depth, MISC_W - heads - 2 * sheads), F32)
    bias_vec = jnp.concatenate([fox_fb, ssd_dt_bias, ssd_dt_bias, lane_pad], axis=-1)[:, None, :]
    alog_vec = jnp.concatenate([jnp.zeros((depth, heads + sheads), F32), ssd_a_log, lane_pad], axis=-1)[:, None, :]
    norm_w3 = norm_w.reshape(depth * 6, 1, d)
    lnw3, lnb3 = gm_ln_w[:, None, :], gm_ln_b[:, None, :]
    reps = GM_CHUNK // dseq
    eye = jnp.eye(reps, dtype=F32)
    ws_s = jnp.einsum('ab,lgts->lgatbs', eye, gm_ws[:, :, :dseq, :dseq]).reshape(depth, groups, GM_CHUNK, GM_CHUNK)
    bs_p = jnp.transpose(gm_bs, (0, 2, 1))
    bs_s = jnp.tile(jnp.transpose(gm_bs[:, :, :dseq], (0, 2, 1)), (1, reps, 1))
    conv_b3 = ssd_conv_b[:, None, :]
    dvec = jnp.repeat(ssd_d, hp, axis=-1)[:, None, :]
    snw3 = ssd_norm_w[:, None, :]
    cache_k, cache_v = cache_fox_k, cache_fox_v
    conv_prev_p = jnp.zeros((bsz, 8, cch), F32)
    conv_prev_s = jnp.pad(state_ssd_conv, ((0, 0), (0, 0), (8 - (conv_k - 1), 0), (0, 0)))
    s0_p = jnp.zeros((bsz, sheads, hp, sn), F32)

    x = jnp.concatenate([x_prompt.reshape(n_p, d), x_sample.reshape(n_s, d)], axis=0)
    tm_big = m // 8 if m % 64 == 0 and (m // 8) % 16 == 0 else 256
    tm_ffn = m // 12 if m % 12 == 0 and (m // 12) % 16 == 0 else 256
    tm_out = m // 16 if m % 16 == 0 and (m // 16) % 16 == 0 else 256
    lf_len = 2 * past
    assert past + dseq <= lf_len

    outs = {k: [] for k in ("plf", "pconv", "pst", "slf", "sgm", "sconv", "sst")}
    kv_p = kv_s = None
    for l in range(depth):
        x = _ffn(x, norm_w3, wg, wu, wd, l, 0, tm_ffn, 512)
        proj = _inproj(x, norm_w3, w_r, bias_vec, alog_vec, l, tm_big, tn, misc_in_tile)
        misc = proj[:, off_misc:off_misc + heads + 2 * sheads]
        logf_p, logf_s = misc[:n_p, :heads], misc[n_p:, :heads]
        lf_rows_p = _head_rows(logf_p, bsz, seq).reshape(bsz, heads, seq // LANES, LANES)
        lf_all = jnp.concatenate([cache_fox_logf[l], logf_s.reshape(dbsz, dseq, heads),
                                  jnp.zeros((dbsz, lf_len - past - dseq, heads), F32)], axis=1)
        lf_rows_s = jnp.transpose(lf_all, (0, 2, 1)).reshape(dbsz, heads, lf_len // LANES, LANES)
        dt_rows_p = _head_rows(misc[:n_p, heads:heads + sheads], bsz, seq)
        a_rows_p = _head_rows(misc[:n_p, heads + sheads:], bsz, seq)
        dt_rows_s = _pad_lanes(_head_rows(misc[n_p:, heads:heads + sheads], dbsz, dseq), LANES)
        a_rows_s = _pad_lanes(_head_rows(misc[n_p:, heads + sheads:], dbsz, dseq), LANES)

        fox_p, *kv_p = _fox_prompt(proj, lf_rows_p, kv_p, l, depth, bsz, seq, heads, dh, col_q, col_k, col_v, 256)
        fox_s, *kv_s = _fox_sample(proj, lf_rows_s, cache_k, cache_v, kv_s, l, n_p, dbsz, dseq, heads, dh, 0, 1, 2)
        gm_o, gm_v_s = _gmlp(proj, lnw3, lnb3, gm_ws, ws_s, bs_p, bs_s, l, n_p, groups, gch, col_u, col_gv, 256)
        ssd_p, conv_p, st_p = _ssd(proj, dt_rows_p, a_rows_p, conv_prev_p, s0_p, ssd_conv_w, conv_b3, dvec, snw3,
                                   l, 0, bsz, seq, 256, SSD_CHUNK, sheads, hp, ngroups, sn,
                                   off_xbc // cch, off_z // ssd_w, off_misc // MISC_W)
        ssd_s, conv_s, st_s = _ssd(proj, dt_rows_s, a_rows_s, conv_prev_s[l], state_ssd[l], ssd_conv_w, conv_b3,
                                   dvec, snw3, l, n_p, dbsz, dseq, dseq, dseq, sheads, hp, ngroups, sn,
                                   off_xbc // cch, off_z // ssd_w, off_misc // MISC_W)
        fox_o = jnp.concatenate([fox_p, fox_s], axis=0)
        ssd_o = jnp.concatenate([ssd_p, ssd_s], axis=0)
        x = _outproj(fox_o, gm_o, ssd_o, x, norm_w3, wo, l, tm_out)
        x = _ffn(x, norm_w3, wg, wu, wd, l, 1, tm_ffn, 512)

        outs["plf"].append(logf_p.reshape(bsz, seq, heads))
        outs["pconv"].append(conv_p)
        outs["pst"].append(st_p)
        outs["slf"].append(logf_s.reshape(dbsz, dseq, heads))
        outs["sgm"].append(gm_v_s.reshape(dbsz, dseq, gm_w))
        outs["sconv"].append(conv_s)
        outs["sst"].append(st_s)

    st = {k: jnp.stack(v) for k, v in outs.items()}
    y_prompt = x[:n_p].reshape(bsz, seq, d)
    y_sample = x[n_p:].reshape(dbsz, dseq, d)
    p_k, p_v = (a.reshape(depth, bsz, seq, heads, dh) for a in kv_p)
    s_k, s_v = (a.reshape(depth, dbsz, dseq, heads, dh) for a in kv_s)
    return (y_prompt, y_sample, p_k, p_v, st["plf"], st["pconv"], st["pst"],
            s_k, s_v, st["slf"], st["sgm"], st["sconv"], st["sst"])
```

```python
import functools

import jax
import jax.numpy as jnp
from jax import lax
from jax.experimental import pallas as pl
from jax.experimental.pallas import tpu as pltpu

F32 = jnp.float32
BF16 = jnp.bfloat16
EPS = 1e-6
LANES = 128
SSD_CHUNK = 64
GM_CHUNK = 128
VMEM_LIMIT = 56 * 1024 * 1024

MISC_W = LANES


def _cparams(sem):
    return pltpu.CompilerParams(dimension_semantics=sem, vmem_limit_bytes=VMEM_LIMIT)


def _rmsnorm(x, w):
    return x * lax.rsqrt(jnp.mean(x * x, axis=-1, keepdims=True) + EPS) * w


def _softplus(x):
    return jnp.maximum(x, 0.0) + jnp.log1p(jnp.exp(-jnp.abs(x)))


def _dot(a, b):
    return jnp.dot(a, b, preferred_element_type=F32)


def _dot_nt(a, b):
    return lax.dot_general(a, b, (((1,), (1,)), ((), ())), preferred_element_type=F32)


def _dot_tn(a, b):
    return lax.dot_general(a, b, (((0,), (0,)), ((), ())), preferred_element_type=F32)


def _row_to_col(row):
    n = row.shape[1]
    eye = lax.broadcasted_iota(jnp.int32, (n, n), 0) == lax.broadcasted_iota(jnp.int32, (n, n), 1)
    return jnp.sum(jnp.where(eye, jnp.broadcast_to(row, (n, n)), 0.0), axis=1, keepdims=True)


def _lane_cumsum(x, seg):
    lane = lax.broadcasted_iota(jnp.int32, x.shape, 1) & (seg - 1)
    k = 1
    while k < seg:
        x = x + jnp.where(lane >= k, pltpu.roll(x, k, axis=1), 0.0)
        k *= 2
    return x


def _flat_cumsum(x):
    x = _lane_cumsum(x, LANES)
    tot = x[:, LANES - 1:LANES]
    rows = []
    carry = jnp.zeros((1, 1), F32)
    for r in range(x.shape[0]):
        rows.append(x[r:r + 1, :] + carry)
        carry = carry + tot[r:r + 1, :]
    return jnp.concatenate(rows, axis=0)


def _ffn_body(x_ref, nwa_ref, nwb_ref, wg_ref, wu_ref, wd_ref, o_ref, h_ref):
    f = pl.program_id(1)

    @pl.when(f == 0)
    def _():
        h_ref[...] = _rmsnorm(x_ref[...], nwa_ref[...]).astype(BF16)
        o_ref[...] = jnp.zeros_like(o_ref)

    h = h_ref[...]
    a = _dot(h, wg_ref[...].astype(BF16))
    g = _dot(h, wu_ref[...])
    act = (a * jax.nn.sigmoid(a) * g).astype(BF16)
    o_ref[...] += _dot(act, wd_ref[...].astype(BF16))

    @pl.when(f == pl.num_programs(1) - 1)
    def _():
        o_ref[...] = x_ref[...] + 0.5 * _rmsnorm(o_ref[...], nwb_ref[...])


def _ffn(x, norm_w3, wg, wu, wd, l, j, tm, tf):
    m, d = x.shape
    dff = wg.shape[-1]
    return pl.pallas_call(
        _ffn_body,
        grid=(m // tm, dff // tf),
        in_specs=[
            pl.BlockSpec((tm, d), lambda i, f: (i, 0)),
            pl.BlockSpec((None, 1, d), lambda i, f: (l * 6 + 3 * j + j, 0, 0)),
            pl.BlockSpec((None, 1, d), lambda i, f: (l * 6 + 3 * j + j + 1, 0, 0)),
            pl.BlockSpec((None, None, d, tf), lambda i, f: (l, j, 0, f)),
            pl.BlockSpec((None, None, d, tf), lambda i, f: (l, j, 0, f)),
            pl.BlockSpec((None, None, tf, d), lambda i, f: (l, j, f, 0)),
        ],
        out_specs=pl.BlockSpec((tm, d), lambda i, f: (i, 0)),
        out_shape=jax.ShapeDtypeStruct((m, d), F32),
        scratch_shapes=[pltpu.VMEM((tm, d), BF16)],
        compiler_params=_cparams(("parallel", "arbitrary")),
        name=f"ffn_{l}_{j}",
    )(x, norm_w3, norm_w3, wg, wu, wd)


def _inproj_body(x_ref, nw_ref, w_ref, bias_ref, alog_ref, o_ref, h_ref, *, misc_off):
    j = pl.program_id(1)

    @pl.when(j == 0)
    def _():
        h_ref[...] = _rmsnorm(x_ref[...], nw_ref[...]).astype(BF16)

    acc = _dot_nt(h_ref[...], w_ref[...])
    o_ref[...] = acc

    @pl.when(j == pl.num_programs(1) - 1)
    def _():
        raw = acc[:, misc_off:misc_off + MISC_W] + bias_ref[...]
        lane = lax.broadcasted_iota(jnp.int32, raw.shape, 1)
        sp = _softplus(raw)
        logf = -_softplus(-raw)
        neg_a = -jnp.exp(alog_ref[...])
        val = jnp.where(lane < 8, logf, jnp.where(lane < 16, sp, jnp.where(lane < 24, sp * neg_a, 0.0)))
        o_ref[:, misc_off:misc_off + MISC_W] = val


def _inproj(x, norm_w3, w_r, bias_vec, alog_vec, l, tm, tn, misc_off):
    m, d = x.shape
    npad = w_r.shape[1]
    return pl.pallas_call(
        functools.partial(_inproj_body, misc_off=misc_off),
        grid=(m // tm, npad // tn),
        in_specs=[
            pl.BlockSpec((tm, d), lambda i, j: (i, 0)),
            pl.BlockSpec((None, 1, d), lambda i, j: (l * 6 + 2, 0, 0)),
            pl.BlockSpec((None, tn, d), lambda i, j: (l, j, 0)),
            pl.BlockSpec((None, 1, MISC_W), lambda i, j: (l, 0, 0)),
            pl.BlockSpec((None, 1, MISC_W), lambda i, j: (l, 0, 0)),
        ],
        out_specs=pl.BlockSpec((tm, tn), lambda i, j: (i, j)),
        out_shape=jax.ShapeDtypeStruct((m, npad), F32),
        scratch_shapes=[pltpu.VMEM((tm, d), BF16)],
        compiler_params=_cparams(("parallel", "arbitrary")),
        name=f"inproj_{l}",
    )(x, norm_w3, w_r, bias_vec, alog_vec)


def _rows_as_lane_vector(c_ref, start, count):
    rows = c_ref[pl.ds(start, count), :]
    return jnp.concatenate([rows[r:r + 1, :] for r in range(count)], axis=1)


def _store_layer_rows(out_ref, val, l, aliased):
    if aliased:
        out_ref[...] = val
    else:
        for slot in range(out_ref.shape[0]):
            out_ref[slot] = val if slot == l else jnp.zeros_like(val)


def _split3(x):
    hi = x.astype(BF16).astype(F32)
    r = x - hi
    mid = r.astype(BF16).astype(F32)
    return hi, mid, r - mid


def _fox_prompt_body(lf_ref, q_ref, k_ref, v_ref, *rest, tq, scale, l, aliased):
    o_ref, ko_ref, vo_ref, c_scr, qx_scr, kx_scr, vb_scr, s_scr, p_scr = rest[2 * aliased:]
    seq, dh = q_ref.shape
    nq = seq // tq
    sub = tq // LANES
    log2e = 1.4426950408889634
    c_scr[...] = _flat_cumsum(lf_ref[...]) * log2e
    lane = lax.broadcasted_iota(jnp.int32, (tq, LANES), 1)

    def prepare(i):
        rows = slice(i * tq, (i + 1) * tq)
        hi, mid, lo = _split3(_row_to_col(_rows_as_lane_vector(c_scr, i * sub, sub)))
        q_ext = jnp.where(lane == 0, hi, jnp.where(lane == 1, mid, jnp.where(lane == 2, lo,
                          jnp.where(lane < 6, 1.0, 0.0))))
        k_ext = jnp.where(lane < 3, 1.0, jnp.where(lane == 3, -hi, jnp.where(lane == 4, -mid,
                          jnp.where(lane == 5, -lo, 0.0))))
        k_rows, v_rows = k_ref[rows, :], v_ref[rows, :]
        qx_scr[rows, 0:dh] = (q_ref[rows, :] * (scale * log2e)).astype(BF16)
        qx_scr[rows, dh:dh + LANES] = q_ext.astype(BF16)
        kx_scr[0:dh, rows] = k_rows.T.astype(BF16)
        kx_scr[dh:dh + LANES, rows] = k_ext.T.astype(BF16)
        vb_scr[rows, :] = v_rows.astype(BF16)
        if aliased:
            ko_ref[rows, :] = k_rows
            vo_ref[rows, :] = v_rows
        else:
            for slot in range(ko_ref.shape[0]):
                ko_ref[slot, rows, :] = k_rows if slot == l else jnp.zeros_like(k_rows)
                vo_ref[slot, rows, :] = v_rows if slot == l else jnp.zeros_like(v_rows)

    causal = lax.broadcasted_iota(jnp.int32, (tq, tq), 1) <= lax.broadcasted_iota(jnp.int32, (tq, tq), 0)
    slab = 64
    n_slabs = tq // slab

    def scores(i, j):
        s = _dot(qx_scr[i * tq:(i + 1) * tq, :], kx_scr[:, j * tq:(j + 1) * tq])
        if j == i:
            s = jnp.where(causal, s, -jnp.inf)
        s_scr[i % 2, :, j * tq:(j + 1) * tq] = s

    def softmax_slab(i, r):
        nk = (i + 1) * tq
        rs = slice(r * slab, (r + 1) * slab)
        m_acc = s_scr[i % 2, rs, 0:LANES]
        for u in range(1, nk // LANES):
            m_acc = jnp.maximum(m_acc, s_scr[i % 2, rs, u * LANES:(u + 1) * LANES])
        m = jnp.broadcast_to(jnp.max(m_acc, axis=1, keepdims=True), (slab, LANES))
        l_acc = jnp.zeros((slab, LANES), F32)
        for u in range(nk // LANES):
            p = jnp.exp2(s_scr[i % 2, rs, u * LANES:(u + 1) * LANES] - m)
            l_acc = l_acc + p
            p_scr[i % 2, rs, u * LANES:(u + 1) * LANES] = p.astype(BF16)
        return jnp.sum(l_acc, axis=1, keepdims=True)

    def weighted_values(i, denom):
        nk = (i + 1) * tq
        acc = _dot(p_scr[i % 2, :, 0:nk], vb_scr[0:nk, :])
        o_ref[i * tq:(i + 1) * tq, :] = (acc / jnp.concatenate(denom, axis=0)).astype(o_ref.dtype)

    prepare(0)
    scores(0, 0)
    denom_prev = None
    for i in range(nq):
        next_keys = list(range(i + 2)) if i + 1 < nq else []
        if next_keys:
            prepare(i + 1)
        denom = []
        for r in range(n_slabs):
            for j in next_keys[r::n_slabs]:
                scores(i + 1, j)
            if r == 1 and denom_prev is not None:
                weighted_values(i - 1, denom_prev)
            denom.append(softmax_slab(i, r))
        denom_prev = denom
    weighted_values(nq - 1, denom_prev)


def _fox_prompt(proj, lf_rows, kv_prev, l, depth, bsz, seq, heads, dh, col_q, col_k, col_v, tq):
    n = seq // LANES
    aliased = kv_prev is not None
    kv_shape = jax.ShapeDtypeStruct((depth, bsz * seq, heads * dh), F32)
    kv_spec = (pl.BlockSpec((None, seq, dh), lambda b, h: (l, b, h)) if aliased else
               pl.BlockSpec((depth, seq, dh), lambda b, h: (0, b, h)))
    in_specs = [
        pl.BlockSpec((None, None, n, LANES), lambda b, h: (b, h, 0, 0)),
        pl.BlockSpec((seq, dh), lambda b, h: (b, col_q + h)),
        pl.BlockSpec((seq, dh), lambda b, h: (b, col_k + h)),
        pl.BlockSpec((seq, dh), lambda b, h: (b, col_v + h)),
    ]
    args = [lf_rows, proj, proj, proj]
    if aliased:
        in_specs += [pl.BlockSpec(memory_space=pl.ANY)] * 2
        args += list(kv_prev)
    return pl.pallas_call(
        functools.partial(_fox_prompt_body, tq=tq, scale=dh ** -0.5, l=l, aliased=aliased),
        grid=(bsz, heads),
        in_specs=in_specs,
        out_specs=[pl.BlockSpec((seq, dh), lambda b, h: (b, h)), kv_spec, kv_spec],
        out_shape=[jax.ShapeDtypeStruct((bsz * seq, heads * dh), BF16), kv_shape, kv_shape],
        scratch_shapes=[pltpu.VMEM((n, LANES), F32), pltpu.VMEM((seq, dh + LANES), BF16),
                        pltpu.VMEM((dh + LANES, seq), BF16), pltpu.VMEM((seq, dh), BF16),
                        pltpu.VMEM((2, tq, seq), F32), pltpu.VMEM((2, tq, seq), BF16)],
        input_output_aliases={4: 1, 5: 2} if aliased else {},
        compiler_params=_cparams(("parallel", "parallel")),
        name="fox_prompt",
    )(*args)


def _fox_sample_body(lf_ref, q_ref, k_ref, v_ref, ck_ref, cv_ref, *rest, past, new, heads, dh, scale, l,
                     aliased):
    o_ref, ko_ref, vo_ref = rest[2 * aliased:]
    _store_layer_rows(ko_ref, k_ref[...], l, aliased)
    _store_layer_rows(vo_ref, v_ref[...], l, aliased)
    n_past = past // LANES
    causal = lax.broadcasted_iota(jnp.int32, (new, new), 1) <= lax.broadcasted_iota(jnp.int32, (new, new), 0)
    for h in range(heads):
        cols = slice(h * dh, (h + 1) * dh)
        c = _flat_cumsum(lf_ref[h])
        c_past = jnp.concatenate([c[r:r + 1, :] for r in range(n_past)], axis=1)
        c_new = c[n_past:n_past + 1, 0:new]
        cq_col = _row_to_col(c_new)
        q = q_ref[:, cols].astype(BF16)
        head_rows = pl.ds(h, past, stride=heads)
        s_past = _dot_nt(q, ck_ref[head_rows, :].astype(BF16)) * scale + cq_col - c_past
        s_new = _dot_nt(q, k_ref[:, cols].astype(BF16)) * scale + cq_col - c_new
        s_new = jnp.where(causal, s_new, -jnp.inf)
        m = jnp.maximum(jnp.max(s_past, axis=1, keepdims=True), jnp.max(s_new, axis=1, keepdims=True))
        p_past = jnp.exp(s_past - m)
        p_new = jnp.exp(s_new - m)
        denom = jnp.sum(p_past, axis=1, keepdims=True) + jnp.sum(p_new, axis=1, keepdims=True)
        acc = _dot(p_past.astype(BF16), cv_ref[head_rows, :].astype(BF16)) + _dot(
            p_new.astype(BF16), v_ref[:, cols].astype(BF16))
        o_ref[:, cols] = (acc / denom).astype(o_ref.dtype)


def _fox_sample(proj, lf_rows, cache_k, cache_v, kv_prev, l, row0, bsz, new, heads, dh, col_q, col_k, col_v):
    depth = cache_k.shape[0]
    past = cache_k.shape[2] // heads
    fw = heads * dh
    n = lf_rows.shape[2]
    rb0 = row0 // new
    aliased = kv_prev is not None
    kv_shape = jax.ShapeDtypeStruct((depth, bsz * new, fw), F32)
    kv_spec = (pl.BlockSpec((None, new, fw), lambda b: (l, b, 0)) if aliased else
               pl.BlockSpec((depth, new, fw), lambda b: (0, b, 0)))
    in_specs = [
        pl.BlockSpec((None, heads, n, LANES), lambda b: (b, 0, 0, 0)),
        pl.BlockSpec((new, fw), lambda b: (rb0 + b, col_q)),
        pl.BlockSpec((new, fw), lambda b: (rb0 + b, col_k)),
        pl.BlockSpec((new, fw), lambda b: (rb0 + b, col_v)),
        pl.BlockSpec((None, None, past * heads, dh), lambda b: (l, b, 0, 0)),
        pl.BlockSpec((None, None, past * heads, dh), lambda b: (l, b, 0, 0)),
    ]
    args = [lf_rows, proj, proj, proj, cache_k, cache_v]
    if aliased:
        in_specs += [pl.BlockSpec(memory_space=pl.ANY)] * 2
        args += list(kv_prev)
    return pl.pallas_call(
        functools.partial(_fox_sample_body, past=past, new=new, heads=heads, dh=dh, scale=dh ** -0.5,
                          l=l, aliased=aliased),
        grid=(bsz,),
        in_specs=in_specs,
        out_specs=[pl.BlockSpec((new, fw), lambda b: (b, 0)), kv_spec, kv_spec],
        out_shape=[jax.ShapeDtypeStruct((bsz * new, fw), BF16), kv_shape, kv_shape],
        input_output_aliases={6: 1, 7: 2} if aliased else {},
        compiler_params=_cparams(("parallel",)),
        name="fox_sample",
    )(*args)


def _gmlp_body(gu_ref, gv_ref, lnw_ref, lnb_ref, wp_ref, ws_ref, bp_ref, bs_ref, o_ref, v_ref, *,
               n_prompt_tiles, groups, gch):
    i = pl.program_id(0)
    is_sample = i >= n_prompt_tiles
    u = jax.nn.gelu(gu_ref[...])
    gv = jax.nn.gelu(gv_ref[...])
    mu = jnp.mean(gv, axis=-1, keepdims=True)
    var = jnp.mean(jnp.square(gv - mu), axis=-1, keepdims=True)
    v = (gv - mu) * lax.rsqrt(var + EPS) * lnw_ref[...] + lnb_ref[...]

    @pl.when(is_sample)
    def _():
        v_ref[...] = v

    tm = u.shape[0]
    tril = lax.broadcasted_iota(jnp.int32, (GM_CHUNK, GM_CHUNK), 1) <= lax.broadcasted_iota(
        jnp.int32, (GM_CHUNK, GM_CHUNK), 0)
    bias_all = jnp.where(is_sample, bs_ref[...], bp_ref[...])
    for g in range(groups):
        wm = jnp.where(is_sample, ws_ref[g], wp_ref[g])
        wm = jnp.where(tril, wm, 0.0).astype(BF16)
        bias = bias_all[:, g:g + 1]
        for c in range(tm // GM_CHUNK):
            rows = slice(c * GM_CHUNK, (c + 1) * GM_CHUNK)
            cols = slice(g * gch, (g + 1) * gch)
            sp = _dot(wm, v[rows, cols].astype(BF16)) + bias
            o_ref[rows, cols] = (u[rows, cols] * sp).astype(o_ref.dtype)


def _gmlp(proj, lnw, lnb, w_prompt, w_sample, b_prompt, b_sample, l, n_prompt_rows, groups, gch,
          col_u, col_v, tm):
    m = proj.shape[0]
    gw = groups * gch
    npt = n_prompt_rows // tm
    return pl.pallas_call(
        functools.partial(_gmlp_body, n_prompt_tiles=npt, groups=groups, gch=gch),
        grid=(m // tm,),
        in_specs=[
            pl.BlockSpec((tm, gw), lambda i: (i, col_u)),
            pl.BlockSpec((tm, gw), lambda i: (i, col_v)),
            pl.BlockSpec((None, 1, gw), lambda i: (l, 0, 0)),
            pl.BlockSpec((None, 1, gw), lambda i: (l, 0, 0)),
            pl.BlockSpec((None, groups, GM_CHUNK, GM_CHUNK), lambda i: (l, 0, 0, 0)),
            pl.BlockSpec((None, groups, GM_CHUNK, GM_CHUNK), lambda i: (l, 0, 0, 0)),
            pl.BlockSpec((None, GM_CHUNK, groups), lambda i: (l, 0, 0)),
            pl.BlockSpec((None, GM_CHUNK, groups), lambda i: (l, 0, 0)),
        ],
        out_specs=[
            pl.BlockSpec((tm, gw), lambda i: (i, 0)),
            pl.BlockSpec((tm, gw), lambda i: (jnp.maximum(i - npt, 0), 0)),
        ],
        out_shape=[jax.ShapeDtypeStruct((m, gw), BF16), jax.ShapeDtypeStruct((m - n_prompt_rows, gw), F32)],
        compiler_params=_cparams(("arbitrary",)),
        name="gmlp",
    )(proj, proj, lnw, lnb, w_prompt, w_sample, b_prompt, b_sample)


def _ssd_body(xbc_ref, z_ref, misc_ref, dtrow_ref, arow_ref, prev_ref, s0_ref, cw_ref, cb_ref, dvec_ref,
              nw_ref, o_ref, conv_ref, st_ref, xp_scr, s_scr, y_scr, *, lt, q, heads, hp, ngroups, sn, conv_k):
    t = pl.program_id(1)
    nt = pl.num_programs(1)
    xw = heads * hp
    pad = 8

    @pl.when(t == 0)
    def _():
        xp_scr[0:pad, :] = prev_ref[...]
        s_scr[...] = s0_ref[...]

    xbc = xbc_ref[...]
    xp_scr[pad:pad + lt, :] = xbc
    acc = cb_ref[...]
    for i in range(conv_k):
        off = pad - (conv_k - 1) + i
        acc = acc + xp_scr[off:off + lt, :] * cw_ref[i:i + 1, :]
    xp_scr[0:pad, :] = xbc[lt - pad:lt, :]
    act = acc * jax.nn.sigmoid(acc)

    @pl.when(t == nt - 1)
    def _():
        conv_ref[...] = xbc[lt - (conv_k - 1):lt, :]

    dt_rows = dtrow_ref[...]
    ltp = dt_rows.shape[1]
    a_rows = jnp.concatenate(
        [_lane_cumsum(arow_ref[:, j * LANES:(j + 1) * LANES], q) for j in range(ltp // LANES)], axis=1)
    row = lax.broadcasted_iota(jnp.int32, (lt, lt), 0)
    col = lax.broadcasted_iota(jnp.int32, (lt, lt), 1)
    tri_all = jnp.where((col <= row) & (col >= (row & -q)), 1.0, 0.0).astype(BF16)
    a_cols = sum(_dot(tri_all, part.astype(BF16)) for part in _split3(misc_ref[...]))
    x_all = act[:, 0:xw]
    if lt < LANES:
        x_all = jnp.concatenate([x_all, jnp.zeros((LANES - lt, xw), F32)], axis=0)
    x_t = x_all.T
    tril = lax.broadcasted_iota(jnp.int32, (q, q), 1) <= lax.broadcasted_iota(jnp.int32, (q, q), 0)
    rep = heads // ngroups
    for c in range(lt // q):
        r0 = c * q
        bmat = [act[r0:r0 + q, xw + g * sn: xw + (g + 1) * sn].astype(BF16) for g in range(ngroups)]
        cmat = [act[r0:r0 + q, xw + (ngroups + g) * sn: xw + (ngroups + g + 1) * sn].astype(BF16)
                for g in range(ngroups)]
        gmat = [_dot_nt(cmat[g], bmat[g]) for g in range(ngroups)]
        for h in range(heads):
            g = h // rep
            a_row = a_rows[h:h + 1, r0:r0 + q]
            dt_row = dt_rows[h:h + 1, r0:r0 + q]
            a_col = a_cols[r0:r0 + q, 16 + h:17 + h]
            a_last = a_row[:, q - 1:q]
            lmat = jnp.exp(jnp.where(tril, a_col - a_row, -jnp.inf))
            scores = gmat[g] * lmat * dt_row
            xh = act[r0:r0 + q, h * hp:(h + 1) * hp]
            y = _dot(scores.astype(BF16), xh.astype(BF16))
            decay_end = jnp.exp(a_last - a_row) * dt_row
            xd_t = (x_t[h * hp:(h + 1) * hp, r0:r0 + q] * decay_end).astype(BF16)
            chunk_state = _dot(xd_t, bmat[g])
            s_in = s_scr[h]
            y = y + _dot_nt(cmat[g], s_in.astype(BF16)) * jnp.exp(a_col)
            s_scr[h] = s_in * jnp.exp(a_last) + chunk_state
            y_scr[r0:r0 + q, h * hp:(h + 1) * hp] = y + dvec_ref[:, h * hp:(h + 1) * hp] * xh

    zz = z_ref[...]
    o_ref[...] = _rmsnorm(y_scr[...] * (zz * jax.nn.sigmoid(zz)), nw_ref[...]).astype(o_ref.dtype)

    @pl.when(t == nt - 1)
    def _():
        st_ref[...] = s_scr[...]


def _ssd(proj, dt_rows, a_rows, conv_prev, s0, conv_w, conv_b, dvec, norm_w, l, row0, bsz, seq, lt, q,
         heads, hp, ngroups, sn, col_xbc, col_z, col_misc):
    nt = seq // lt
    xw = heads * hp
    cch = xw + 2 * ngroups * sn
    conv_k = conv_w.shape[1]
    rb0 = row0 // lt
    ltp = max(lt, LANES)
    body = functools.partial(_ssd_body, lt=lt, q=q, heads=heads, hp=hp, ngroups=ngroups, sn=sn, conv_k=conv_k)
    return pl.pallas_call(
        body,
        grid=(bsz, nt),
        in_specs=[
            pl.BlockSpec((lt, cch), lambda b, t: (rb0 + b * nt + t, col_xbc)),
            pl.BlockSpec((lt, xw), lambda b, t: (rb0 + b * nt + t, col_z)),
            pl.BlockSpec((lt, MISC_W), lambda b, t: (rb0 + b * nt + t, col_misc)),
            pl.BlockSpec((None, heads, ltp), lambda b, t: (b, 0, t)),
            pl.BlockSpec((None, heads, ltp), lambda b, t: (b, 0, t)),
            pl.BlockSpec((None, 8, cch), lambda b, t: (b, 0, 0)),
            pl.BlockSpec((None, heads, hp, sn), lambda b, t: (b, 0, 0, 0)),
            pl.BlockSpec((None, conv_k, cch), lambda b, t: (l, 0, 0)),
            pl.BlockSpec((None, 1, cch), lambda b, t: (l, 0, 0)),
            pl.BlockSpec((None, 1, xw), lambda b, t: (l, 0, 0)),
            pl.BlockSpec((None, 1, xw), lambda b, t: (l, 0, 0)),
        ],
        out_specs=[
            pl.BlockSpec((lt, xw), lambda b, t: (b * nt + t, 0)),
            pl.BlockSpec((None, conv_k - 1, cch), lambda b, t: (b, 0, 0)),
            pl.BlockSpec((None, heads, hp, sn), lambda b, t: (b, 0, 0, 0)),
        ],
        out_shape=[
            jax.ShapeDtypeStruct((bsz * seq, xw), BF16),
            jax.ShapeDtypeStruct((bsz, conv_k - 1, cch), F32),
            jax.ShapeDtypeStruct((bsz, heads, hp, sn), F32),
        ],
        scratch_shapes=[pltpu.VMEM((lt + 8, cch), F32), pltpu.VMEM((heads, hp, sn), F32),
                        pltpu.VMEM((lt, xw), F32)],
        compiler_params=_cparams(("parallel", "arbitrary")),
        name="ssd",
    )(proj, proj, proj, dt_rows, a_rows, conv_prev, s0, conv_w, conv_b, dvec, norm_w)


def _outproj_body(fox_ref, gm_ref, ssd_ref, x_ref, nw_ref, w_ref, o_ref, *, fw, gw):
    o = _dot(fox_ref[...], w_ref[0:fw, :])
    o = o + _dot(gm_ref[...], w_ref[fw:fw + gw, :])
    o = o + _dot(ssd_ref[...], w_ref[fw + gw:, :])
    o_ref[...] = x_ref[...] + _rmsnorm(o, nw_ref[...])


def _outproj(fox_o, gm_o, ssd_o, x, norm_w3, w_out, l, tm):
    m, d = x.shape
    fw, gw, sw = fox_o.shape[1], gm_o.shape[1], ssd_o.shape[1]
    return pl.pallas_call(
        functools.partial(_outproj_body, fw=fw, gw=gw),
        grid=(m // tm,),
        in_specs=[
            pl.BlockSpec((tm, fw), lambda i: (i, 0)),
            pl.BlockSpec((tm, gw), lambda i: (i, 0)),
            pl.BlockSpec((tm, sw), lambda i: (i, 0)),
            pl.BlockSpec((tm, d), lambda i: (i, 0)),
            pl.BlockSpec((None, 1, d), lambda i: (l * 6 + 3, 0, 0)),
            pl.BlockSpec((None, fw + gw + sw, d), lambda i: (l, 0, 0)),
        ],
        out_specs=pl.BlockSpec((tm, d), lambda i: (i, 0)),
        out_shape=jax.ShapeDtypeStruct((m, d), F32),
        compiler_params=_cparams(("parallel",)),
        name=f"outproj_{l}",
    )(fox_o, gm_o, ssd_o, x, norm_w3, w_out)


def _head_rows(vals, bsz, seq):
    return jnp.transpose(vals.reshape(bsz, seq, vals.shape[-1]), (0, 2, 1))


def _pad_lanes(rows, width):
    return jnp.pad(rows, ((0, 0), (0, 0), (0, width - rows.shape[-1])))


def kernel(x_prompt, x_sample, cache_fox_k, cache_fox_v, cache_fox_logf, state_ssd_conv, state_ssd, norm_w,
           w_ffn_gate, w_ffn_up, w_ffn_down, w_in, fox_fb, gm_ln_w, gm_ln_b, gm_ws, gm_bs, ssd_conv_w,
           ssd_conv_b, ssd_dt_bias, ssd_a_log, ssd_d, ssd_norm_w, w_out):
    bsz, seq, d = x_prompt.shape
    dbsz, dseq, _ = x_sample.shape
    depth, _, past, heads, dh = cache_fox_k.shape
    fox_w = heads * dh
    groups, gchunk = gm_ws.shape[1], gm_ws.shape[2]
    gm_w = gm_ln_w.shape[1]
    gch = gm_w // groups
    _, _, sheads, hp, sn = state_ssd.shape
    ssd_w = sheads * hp
    cch = ssd_conv_w.shape[2]
    ngroups = (cch - ssd_w) // (2 * sn)
    conv_k = ssd_conv_w.shape[1]
    n_p, n_s = bsz * seq, dbsz * dseq
    m = n_p + n_s
    assert gchunk == GM_CHUNK and heads == 8 and sheads == 8 and dh == LANES
    assert seq % GM_CHUNK == 0 and GM_CHUNK % dseq == 0 and dseq < SSD_CHUNK and past % LANES == 0

    wg, wu, wd = w_ffn_gate, w_ffn_up.astype(BF16), w_ffn_down
    wo = w_out.astype(BF16)
    sizes = [fox_w, fox_w, fox_w, heads, gm_w, gm_w, ssd_w, cch, sheads]
    offs = [0]
    for s in sizes:
        offs.append(offs[-1] + s)
    w_t = jnp.transpose(w_in, (0, 2, 1))
    wq, wk, wv, wf, wgu, wgv, wz, wxbc, wdt = (w_t[:, offs[i]:offs[i + 1], :] for i in range(9))
    tn = 1024
    used = 3 * fox_w + 2 * gm_w + cch + ssd_w + MISC_W
    npad = -(-used // tn) * tn
    zeros_misc = jnp.zeros((depth, MISC_W - heads - 2 * sheads, d), F32)
    zeros_tail = jnp.zeros((depth, npad - used, d), F32)
    w_r = jnp.concatenate([wq, wk, wv, wgu, wgv, wxbc, wz, wf, wdt, wdt, zeros_misc, zeros_tail],
                          axis=1).astype(BF16)
    col_q, col_k, col_v = 0, fox_w // dh, 2 * fox_w // dh
    col_u, col_gv = 3 * fox_w // gm_w, 3 * fox_w // gm_w + 1
    off_xbc = 3 * fox_w + 2 * gm_w
    off_z = off_xbc + cch
    off_misc = off_z + ssd_w
    assert off_xbc % cch == 0 and off_z % ssd_w == 0 and off_misc % MISC_W == 0
    misc_in_tile = off_misc % tn
    assert off_misc // tn == npad // tn - 1 and misc_in_tile + MISC_W <= tn

    lane_pad = jnp.zeros((depth, MISC_W - heads - 2 * sheads), F32)
    bias_vec = jnp.concatenate([fox_fb, ssd_dt_bias, ssd_dt_bias, lane_pad], axis=-1)[:, None, :]
    alog_vec = jnp.concatenate([jnp.zeros((depth, heads + sheads), F32), ssd_a_log, lane_pad], axis=-1)[:, None, :]
    norm_w3 = norm_w.reshape(depth * 6, 1, d)
    lnw3, lnb3 = gm_ln_w[:, None, :], gm_ln_b[:, None, :]
    reps = GM_CHUNK // dseq
    eye = jnp.eye(reps, dtype=F32)
    ws_s = jnp.einsum('ab,lgts->lgatbs', eye, gm_ws[:, :, :dseq, :dseq]).reshape(depth, groups, GM_CHUNK, GM_CHUNK)
    bs_p = jnp.transpose(gm_bs, (0, 2, 1))
    bs_s = jnp.tile(jnp.transpose(gm_bs[:, :, :dseq], (0, 2, 1)), (1, reps, 1))
    conv_b3 = ssd_conv_b[:, None, :]
    dvec = jnp.repeat(ssd_d, hp, axis=-1)[:, None, :]
    snw3 = ssd_norm_w[:, None, :]
    cache_k = cache_fox_k.reshape(depth, dbsz, past * heads, dh)
    cache_v = cache_fox_v.reshape(depth, dbsz, past * heads, dh)
    conv_prev_p = jnp.zeros((bsz, 8, cch), F32)
    conv_prev_s = jnp.pad(state_ssd_conv, ((0, 0), (0, 0), (8 - (conv_k - 1), 0), (0, 0)))
    s0_p = jnp.zeros((bsz, sheads, hp, sn), F32)

    x = jnp.concatenate([x_prompt.reshape(n_p, d), x_sample.reshape(n_s, d)], axis=0)
    tm_big = m // 8 if m % 64 == 0 and (m // 8) % 16 == 0 else 256
    tm_ffn = m // 12 if m % 12 == 0 and (m // 12) % 16 == 0 else 256
    tm_out = m // 16 if m % 16 == 0 and (m // 16) % 16 == 0 else 256
    lf_len = 2 * past
    assert past + dseq <= lf_len

    outs = {k: [] for k in ("plf", "pconv", "pst", "slf", "sgm", "sconv", "sst")}
    kv_p = kv_s = None
    for l in range(depth):
        x = _ffn(x, norm_w3, wg, wu, wd, l, 0, tm_ffn, 512)
        proj = _inproj(x, norm_w3, w_r, bias_vec, alog_vec, l, tm_big, tn, misc_in_tile)
        misc = proj[:, off_misc:off_misc + heads + 2 * sheads]
        logf_p, logf_s = misc[:n_p, :heads], misc[n_p:, :heads]
        lf_rows_p = _head_rows(logf_p, bsz, seq).reshape(bsz, heads, seq // LANES, LANES)
        lf_all = jnp.concatenate([cache_fox_logf[l], logf_s.reshape(dbsz, dseq, heads),
                                  jnp.zeros((dbsz, lf_len - past - dseq, heads), F32)], axis=1)
        lf_rows_s = jnp.transpose(lf_all, (0, 2, 1)).reshape(dbsz, heads, lf_len // LANES, LANES)
        dt_rows_p = _head_rows(misc[:n_p, heads:heads + sheads], bsz, seq)
        a_rows_p = _head_rows(misc[:n_p, heads + sheads:], bsz, seq)
        dt_rows_s = _pad_lanes(_head_rows(misc[n_p:, heads:heads + sheads], dbsz, dseq), LANES)
        a_rows_s = _pad_lanes(_head_rows(misc[n_p:, heads + sheads:], dbsz, dseq), LANES)

        fox_p, *kv_p = _fox_prompt(proj, lf_rows_p, kv_p, l, depth, bsz, seq, heads, dh, col_q, col_k, col_v, 256)
        fox_s, *kv_s = _fox_sample(proj, lf_rows_s, cache_k, cache_v, kv_s, l, n_p, dbsz, dseq, heads, dh, 0, 1, 2)
        gm_o, gm_v_s = _gmlp(proj, lnw3, lnb3, gm_ws, ws_s, bs_p, bs_s, l, n_p, groups, gch, col_u, col_gv, 256)
        ssd_p, conv_p, st_p = _ssd(proj, dt_rows_p, a_rows_p, conv_prev_p, s0_p, ssd_conv_w, conv_b3, dvec, snw3,
                                   l, 0, bsz, seq, 256, SSD_CHUNK, sheads, hp, ngroups, sn,
                                   off_xbc // cch, off_z // ssd_w, off_misc // MISC_W)
        ssd_s, conv_s, st_s = _ssd(proj, dt_rows_s, a_rows_s, conv_prev_s[l], state_ssd[l], ssd_conv_w, conv_b3,
                                   dvec, snw3, l, n_p, dbsz, dseq, dseq, dseq, sheads, hp, ngroups, sn,
                                   off_xbc // cch, off_z // ssd_w, off_misc // MISC_W)
        fox_o = jnp.concatenate([fox_p, fox_s], axis=0)
        ssd_o = jnp.concatenate([ssd_p, ssd_s], axis=0)
        x = _outproj(fox_o, gm_o, ssd_o, x, norm_w3, wo, l, tm_out)
        x = _ffn(x, norm_w3, wg, wu, wd, l, 1, tm_ffn, 512)

        outs["plf"].append(logf_p.reshape(bsz, seq, heads))
        outs["pconv"].append(conv_p)
        outs["pst"].append(st_p)
        outs["slf"].append(logf_s.reshape(dbsz, dseq, heads))
        outs["sgm"].append(gm_v_s.reshape(dbsz, dseq, gm_w))
        outs["sconv"].append(conv_s)
        outs["sst"].append(st_s)

    st = {k: jnp.stack(v) for k, v in outs.items()}
    y_prompt = x[:n_p].reshape(bsz, seq, d)
    y_sample = x[n_p:].reshape(dbsz, dseq, d)
    p_k, p_v = (a.reshape(depth, bsz, seq, heads, dh) for a in kv_p)
    s_k, s_v = (a.reshape(depth, dbsz, dseq, heads, dh) for a in kv_s)
    return (y_prompt, y_sample, p_k, p_v, st["plf"], st["pconv"], st["pst"],
            s_k, s_v, st["slf"], st["sgm"], st["sconv"], st["sst"])
```

```python
import functools

import jax
import jax.numpy as jnp
from jax import lax
from jax.experimental import pallas as pl
from jax.experimental.pallas import tpu as pltpu

F32 = jnp.float32
BF16 = jnp.bfloat16
EPS = 1e-6
LANES = 128
SSD_CHUNK = 64
GM_CHUNK = 128
VMEM_LIMIT = 56 * 1024 * 1024
FFN_VMEM_LIMIT = 60 * 1024 * 1024

MISC_W = LANES


def _cparams(sem, vmem_limit=VMEM_LIMIT):
    return pltpu.CompilerParams(dimension_semantics=sem, vmem_limit_bytes=vmem_limit)


def _rmsnorm(x, w):
    return x * lax.rsqrt(jnp.mean(x * x, axis=-1, keepdims=True) + EPS) * w


def _softplus(x):
    return jnp.maximum(x, 0.0) + jnp.log1p(jnp.exp(-jnp.abs(x)))


def _dot(a, b):
    return jnp.dot(a, b, preferred_element_type=F32)


def _dot_nt(a, b):
    return lax.dot_general(a, b, (((1,), (1,)), ((), ())), preferred_element_type=F32)


def _dot_tn(a, b):
    return lax.dot_general(a, b, (((0,), (0,)), ((), ())), preferred_element_type=F32)


def _row_to_col(row):
    n = row.shape[1]
    eye = lax.broadcasted_iota(jnp.int32, (n, n), 0) == lax.broadcasted_iota(jnp.int32, (n, n), 1)
    return jnp.sum(jnp.where(eye, jnp.broadcast_to(row, (n, n)), 0.0), axis=1, keepdims=True)


def _lane_cumsum(x, seg):
    lane = lax.broadcasted_iota(jnp.int32, x.shape, 1) & (seg - 1)
    k = 1
    while k < seg:
        x = x + jnp.where(lane >= k, pltpu.roll(x, k, axis=1), 0.0)
        k *= 2
    return x


def _flat_cumsum(x):
    x = _lane_cumsum(x, LANES)
    tot = x[:, LANES - 1:LANES]
    rows = []
    carry = jnp.zeros((1, 1), F32)
    for r in range(x.shape[0]):
        rows.append(x[r:r + 1, :] + carry)
        carry = carry + tot[r:r + 1, :]
    return jnp.concatenate(rows, axis=0)


def _ffn_body(*refs, split_in, split_out, rows_a):
    refs = list(refs)
    x_ref = refs.pop(0)
    xs_ref = refs.pop(0) if split_in else None
    nwa_ref, nwb_ref, wg_ref, wu_ref, wd_ref, o_ref = refs[:6]
    os_ref = refs[6] if split_out else None
    h_ref = refs[-1]
    i, f = pl.program_id(0), pl.program_id(1)
    last_i, last_f = pl.num_programs(0) - 1, pl.num_programs(1) - 1

    tm = o_ref.shape[0]
    whole = [(slice(0, tm), x_ref, slice(0, tm))]
    pieces_last = [(slice(0, rows_a), x_ref, slice(0, rows_a)), (slice(rows_a, tm), xs_ref, slice(None))]

    def per_piece(fn):
        if not split_in:
            fn(whole)
            return
        pl.when(i < last_i)(lambda: fn(whole))
        pl.when(i == last_i)(lambda: fn(pieces_last))

    def start(pieces):
        for rows, ref, src in pieces:
            h_ref[rows, :] = _rmsnorm(ref[src, :], nwa_ref[...]).astype(BF16)
        o_ref[...] = jnp.zeros_like(o_ref)

    pl.when(f == 0)(lambda: per_piece(start))

    h = h_ref[...]
    a = _dot(h, wg_ref[...].astype(BF16))
    g = _dot(h, wu_ref[...])
    act = (a * jax.nn.sigmoid(a) * g).astype(BF16)
    o_ref[...] += _dot(act, wd_ref[...].astype(BF16))

    def finish(pieces):
        for rows, ref, src in pieces:
            o_ref[rows, :] = ref[src, :] + 0.5 * _rmsnorm(o_ref[rows, :], nwb_ref[...])
        if split_out:
            @pl.when(i == last_i)
            def _():
                os_ref[...] = o_ref[rows_a:, :]

    pl.when(f == last_f)(lambda: per_piece(finish))


def _ffn(x, norm_w3, wg, wu, wd, l, j, tm, tf, split_out_rows=None):
    split_in = isinstance(x, tuple)
    split_out = split_out_rows is not None
    xs = list(x) if split_in else [x]
    n_a, n_b = (xs[0].shape[0], xs[1].shape[0]) if split_in else (split_out_rows or (x.shape[0], 0))
    m, d = n_a + n_b, xs[0].shape[1]
    dff = wg.shape[-1]
    rows_a = n_a % tm
    assert m % tm == 0 and (not (split_in or split_out) or (rows_a + n_b == tm and rows_a % 8 == 0))
    row_spec = pl.BlockSpec((tm, d), lambda i, f: (i, 0))
    tail_spec = pl.BlockSpec((n_b, d), lambda i, f: (0, 0), pipeline_mode=pl.Buffered(1))
    out_shape = jax.ShapeDtypeStruct((m, d), F32)
    if split_out:
        out_shape = [jax.ShapeDtypeStruct((n_a, d), F32), jax.ShapeDtypeStruct((n_b, d), F32)]
    return pl.pallas_call(
        functools.partial(_ffn_body, split_in=split_in, split_out=split_out, rows_a=rows_a),
        grid=(m // tm, dff // tf),
        in_specs=[row_spec] + ([tail_spec] if split_in else []) + [
            pl.BlockSpec((None, 1, d), lambda i, f: (l * 6 + 3 * j + j, 0, 0)),
            pl.BlockSpec((None, 1, d), lambda i, f: (l * 6 + 3 * j + j + 1, 0, 0)),
            pl.BlockSpec((None, None, d, tf), lambda i, f: (l, j, 0, f)),
            pl.BlockSpec((None, None, d, tf), lambda i, f: (l, j, 0, f)),
            pl.BlockSpec((None, None, tf, d), lambda i, f: (l, j, f, 0)),
        ],
        out_specs=[row_spec, tail_spec] if split_out else row_spec,
        out_shape=out_shape,
        scratch_shapes=[pltpu.VMEM((tm, d), BF16)],
        compiler_params=_cparams(("parallel", "arbitrary"), FFN_VMEM_LIMIT),
        name=f"ffn_{l}_{j}",
    )(*xs, norm_w3, norm_w3, wg, wu, wd)


def _inproj_body(x_ref, nw_ref, w_ref, bias_ref, alog_ref, o_ref, h_ref, *, misc_off):
    j = pl.program_id(1)

    @pl.when(j == 0)
    def _():
        h_ref[...] = _rmsnorm(x_ref[...], nw_ref[...]).astype(BF16)

    acc = _dot_nt(h_ref[...], w_ref[...])
    o_ref[...] = acc

    @pl.when(j == pl.num_programs(1) - 1)
    def _():
        raw = acc[:, misc_off:misc_off + MISC_W] + bias_ref[...]
        lane = lax.broadcasted_iota(jnp.int32, raw.shape, 1)
        sp = _softplus(raw)
        logf = -_softplus(-raw)
        neg_a = -jnp.exp(alog_ref[...])
        val = jnp.where(lane < 8, logf, jnp.where(lane < 16, sp, jnp.where(lane < 24, sp * neg_a, 0.0)))
        o_ref[:, misc_off:misc_off + MISC_W] = val


def _inproj(x, norm_w3, w_r, bias_vec, alog_vec, l, tm, tn, misc_off):
    m, d = x.shape
    npad = w_r.shape[1]
    return pl.pallas_call(
        functools.partial(_inproj_body, misc_off=misc_off),
        grid=(m // tm, npad // tn),
        in_specs=[
            pl.BlockSpec((tm, d), lambda i, j: (i, 0)),
            pl.BlockSpec((None, 1, d), lambda i, j: (l * 6 + 2, 0, 0)),
            pl.BlockSpec((None, tn, d), lambda i, j: (l, j, 0)),
            pl.BlockSpec((None, 1, MISC_W), lambda i, j: (l, 0, 0)),
            pl.BlockSpec((None, 1, MISC_W), lambda i, j: (l, 0, 0)),
        ],
        out_specs=pl.BlockSpec((tm, tn), lambda i, j: (i, j)),
        out_shape=jax.ShapeDtypeStruct((m, npad), F32),
        scratch_shapes=[pltpu.VMEM((tm, d), BF16)],
        compiler_params=_cparams(("parallel", "arbitrary")),
        name=f"inproj_{l}",
    )(x, norm_w3, w_r, bias_vec, alog_vec)


def _rows_as_lane_vector(c_ref, start, count):
    rows = c_ref[pl.ds(start, count), :]
    return jnp.concatenate([rows[r:r + 1, :] for r in range(count)], axis=1)


def _store_layer_rows(out_ref, val, l, aliased):
    if aliased:
        out_ref[...] = val
    else:
        for slot in range(out_ref.shape[0]):
            out_ref[slot] = val if slot == l else jnp.zeros_like(val)


def _split3(x):
    hi = x.astype(BF16).astype(F32)
    r = x - hi
    mid = r.astype(BF16).astype(F32)
    return hi, mid, r - mid


def _fox_prompt_body(lf_ref, q_ref, k_ref, v_ref, *rest, tq, scale, l, aliased):
    o_ref, ko_ref, vo_ref, c_scr, qx_scr, kx_scr, vb_scr, s_scr, p_scr = rest[2 * aliased:]
    seq, dh = q_ref.shape
    nq = seq // tq
    sub = tq // LANES
    log2e = 1.4426950408889634
    c_scr[...] = _flat_cumsum(lf_ref[...]) * log2e
    lane = lax.broadcasted_iota(jnp.int32, (tq, LANES), 1)

    def prepare(i):
        rows = slice(i * tq, (i + 1) * tq)
        hi, mid, lo = _split3(_row_to_col(_rows_as_lane_vector(c_scr, i * sub, sub)))
        q_ext = jnp.where(lane == 0, hi, jnp.where(lane == 1, mid, jnp.where(lane == 2, lo,
                          jnp.where(lane < 6, 1.0, 0.0))))
        k_ext = jnp.where(lane < 3, 1.0, jnp.where(lane == 3, -hi, jnp.where(lane == 4, -mid,
                          jnp.where(lane == 5, -lo, 0.0))))
        k_rows, v_rows = k_ref[rows, :], v_ref[rows, :]
        qx_scr[rows, 0:dh] = (q_ref[rows, :] * (scale * log2e)).astype(BF16)
        qx_scr[rows, dh:dh + LANES] = q_ext.astype(BF16)
        kx_scr[0:dh, rows] = k_rows.T.astype(BF16)
        kx_scr[dh:dh + LANES, rows] = k_ext.T.astype(BF16)
        vb_scr[rows, :] = v_rows.astype(BF16)
        if aliased:
            ko_ref[rows, :] = k_rows
            vo_ref[rows, :] = v_rows
        else:
            for slot in range(ko_ref.shape[0]):
                ko_ref[slot, rows, :] = k_rows if slot == l else jnp.zeros_like(k_rows)
                vo_ref[slot, rows, :] = v_rows if slot == l else jnp.zeros_like(v_rows)

    causal = lax.broadcasted_iota(jnp.int32, (tq, tq), 1) <= lax.broadcasted_iota(jnp.int32, (tq, tq), 0)
    slab = 64
    n_slabs = tq // slab

    def scores(i, j):
        s = _dot(qx_scr[i * tq:(i + 1) * tq, :], kx_scr[:, j * tq:(j + 1) * tq])
        if j == i:
            s = jnp.where(causal, s, -jnp.inf)
        s_scr[i % 2, :, j * tq:(j + 1) * tq] = s

    def softmax_slab(i, r):
        nk = (i + 1) * tq
        rs = slice(r * slab, (r + 1) * slab)
        m_acc = s_scr[i % 2, rs, 0:LANES]
        for u in range(1, nk // LANES):
            m_acc = jnp.maximum(m_acc, s_scr[i % 2, rs, u * LANES:(u + 1) * LANES])
        m = jnp.broadcast_to(jnp.max(m_acc, axis=1, keepdims=True), (slab, LANES))
        l_acc = jnp.zeros((slab, LANES), F32)
        for u in range(nk // LANES):
            p = jnp.exp2(s_scr[i % 2, rs, u * LANES:(u + 1) * LANES] - m)
            l_acc = l_acc + p
            p_scr[i % 2, rs, u * LANES:(u + 1) * LANES] = p.astype(BF16)
        return jnp.sum(l_acc, axis=1, keepdims=True)

    def weighted_values(i, denom):
        nk = (i + 1) * tq
        acc = _dot(p_scr[i % 2, :, 0:nk], vb_scr[0:nk, :])
        o_ref[i * tq:(i + 1) * tq, :] = (acc / jnp.concatenate(denom, axis=0)).astype(o_ref.dtype)

    prepare(0)
    scores(0, 0)
    denom_prev = None
    for i in range(nq):
        next_keys = list(range(i + 2)) if i + 1 < nq else []
        if next_keys:
            prepare(i + 1)
        denom = []
        for r in range(n_slabs):
            for j in next_keys[r::n_slabs]:
                scores(i + 1, j)
            if r == 1 and denom_prev is not None:
                weighted_values(i - 1, denom_prev)
            denom.append(softmax_slab(i, r))
        denom_prev = denom
    weighted_values(nq - 1, denom_prev)


def _fox_prompt(proj, lf_rows, kv_prev, l, depth, bsz, seq, heads, dh, col_q, col_k, col_v, tq):
    n = seq // LANES
    aliased = kv_prev is not None
    kv_shape = jax.ShapeDtypeStruct((depth, bsz * seq, heads * dh), F32)
    kv_spec = (pl.BlockSpec((None, seq, dh), lambda b, h: (l, b, h)) if aliased else
               pl.BlockSpec((depth, seq, dh), lambda b, h: (0, b, h)))
    in_specs = [
        pl.BlockSpec((None, None, n, LANES), lambda b, h: (b, h, 0, 0)),
        pl.BlockSpec((seq, dh), lambda b, h: (b, col_q + h)),
        pl.BlockSpec((seq, dh), lambda b, h: (b, col_k + h)),
        pl.BlockSpec((seq, dh), lambda b, h: (b, col_v + h)),
    ]
    args = [lf_rows, proj, proj, proj]
    if aliased:
        in_specs += [pl.BlockSpec(memory_space=pl.ANY)] * 2
        args += list(kv_prev)
    return pl.pallas_call(
        functools.partial(_fox_prompt_body, tq=tq, scale=dh ** -0.5, l=l, aliased=aliased),
        grid=(bsz, heads),
        in_specs=in_specs,
        out_specs=[pl.BlockSpec((seq, dh), lambda b, h: (b, h)), kv_spec, kv_spec],
        out_shape=[jax.ShapeDtypeStruct((bsz * seq, heads * dh), BF16), kv_shape, kv_shape],
        scratch_shapes=[pltpu.VMEM((n, LANES), F32), pltpu.VMEM((seq, dh + LANES), BF16),
                        pltpu.VMEM((dh + LANES, seq), BF16), pltpu.VMEM((seq, dh), BF16),
                        pltpu.VMEM((2, tq, seq), F32), pltpu.VMEM((2, tq, seq), BF16)],
        input_output_aliases={4: 1, 5: 2} if aliased else {},
        compiler_params=_cparams(("parallel", "parallel")),
        name="fox_prompt",
    )(*args)


def _fox_sample_body(lf_ref, q_ref, k_ref, v_ref, ck_ref, cv_ref, *rest, past, new, heads, dh, scale, l,
                     aliased):
    o_ref, ko_ref, vo_ref = rest[2 * aliased:]
    _store_layer_rows(ko_ref, k_ref[...], l, aliased)
    _store_layer_rows(vo_ref, v_ref[...], l, aliased)
    n_past = past // LANES
    causal = lax.broadcasted_iota(jnp.int32, (new, new), 1) <= lax.broadcasted_iota(jnp.int32, (new, new), 0)
    for h in range(heads):
        cols = slice(h * dh, (h + 1) * dh)
        c = _flat_cumsum(lf_ref[h])
        c_past = jnp.concatenate([c[r:r + 1, :] for r in range(n_past)], axis=1)
        c_new = c[n_past:n_past + 1, 0:new]
        cq_col = _row_to_col(c_new)
        q = q_ref[:, cols].astype(BF16)
        head_rows = pl.ds(h, past, stride=heads)
        s_past = _dot_nt(q, ck_ref[head_rows, :].astype(BF16)) * scale + cq_col - c_past
        s_new = _dot_nt(q, k_ref[:, cols].astype(BF16)) * scale + cq_col - c_new
        s_new = jnp.where(causal, s_new, -jnp.inf)
        m = jnp.maximum(jnp.max(s_past, axis=1, keepdims=True), jnp.max(s_new, axis=1, keepdims=True))
        p_past = jnp.exp(s_past - m)
        p_new = jnp.exp(s_new - m)
        denom = jnp.sum(p_past, axis=1, keepdims=True) + jnp.sum(p_new, axis=1, keepdims=True)
        acc = _dot(p_past.astype(BF16), cv_ref[head_rows, :].astype(BF16)) + _dot(
            p_new.astype(BF16), v_ref[:, cols].astype(BF16))
        o_ref[:, cols] = (acc / denom).astype(o_ref.dtype)


def _fox_sample(proj, lf_rows, cache_k, cache_v, kv_prev, l, row0, bsz, new, heads, dh, col_q, col_k, col_v):
    depth = cache_k.shape[0]
    past = cache_k.shape[2] // heads
    fw = heads * dh
    n = lf_rows.shape[2]
    rb0 = row0 // new
    aliased = kv_prev is not None
    kv_shape = jax.ShapeDtypeStruct((depth, bsz * new, fw), F32)
    kv_spec = (pl.BlockSpec((None, new, fw), lambda b: (l, b, 0)) if aliased else
               pl.BlockSpec((depth, new, fw), lambda b: (0, b, 0)))
    in_specs = [
        pl.BlockSpec((None, heads, n, LANES), lambda b: (b, 0, 0, 0)),
        pl.BlockSpec((new, fw), lambda b: (rb0 + b, col_q)),
        pl.BlockSpec((new, fw), lambda b: (rb0 + b, col_k)),
        pl.BlockSpec((new, fw), lambda b: (rb0 + b, col_v)),
        pl.BlockSpec((None, None, past * heads, dh), lambda b: (l, b, 0, 0)),
        pl.BlockSpec((None, None, past * heads, dh), lambda b: (l, b, 0, 0)),
    ]
    args = [lf_rows, proj, proj, proj, cache_k, cache_v]
    if aliased:
        in_specs += [pl.BlockSpec(memory_space=pl.ANY)] * 2
        args += list(kv_prev)
    return pl.pallas_call(
        functools.partial(_fox_sample_body, past=past, new=new, heads=heads, dh=dh, scale=dh ** -0.5,
                          l=l, aliased=aliased),
        grid=(bsz,),
        in_specs=in_specs,
        out_specs=[pl.BlockSpec((new, fw), lambda b: (b, 0)), kv_spec, kv_spec],
        out_shape=[jax.ShapeDtypeStruct((bsz * new, fw), BF16), kv_shape, kv_shape],
        input_output_aliases={6: 1, 7: 2} if aliased else {},
        compiler_params=_cparams(("parallel",)),
        name="fox_sample",
    )(*args)


def _gmlp_body(gu_ref, gv_ref, lnw_ref, lnb_ref, wp_ref, ws_ref, bp_ref, bs_ref, o_ref, v_ref, *,
               n_prompt_tiles, groups, gch):
    i = pl.program_id(0)
    is_sample = i >= n_prompt_tiles
    u = jax.nn.gelu(gu_ref[...])
    gv = jax.nn.gelu(gv_ref[...])
    mu = jnp.mean(gv, axis=-1, keepdims=True)
    var = jnp.mean(jnp.square(gv - mu), axis=-1, keepdims=True)
    v = (gv - mu) * lax.rsqrt(var + EPS) * lnw_ref[...] + lnb_ref[...]

    @pl.when(is_sample)
    def _():
        v_ref[...] = v

    tm = u.shape[0]
    tril = lax.broadcasted_iota(jnp.int32, (GM_CHUNK, GM_CHUNK), 1) <= lax.broadcasted_iota(
        jnp.int32, (GM_CHUNK, GM_CHUNK), 0)
    bias_all = jnp.where(is_sample, bs_ref[...], bp_ref[...])
    for g in range(groups):
        wm = jnp.where(is_sample, ws_ref[g], wp_ref[g])
        wm = jnp.where(tril, wm, 0.0).astype(BF16)
        bias = bias_all[:, g:g + 1]
        for c in range(tm // GM_CHUNK):
            rows = slice(c * GM_CHUNK, (c + 1) * GM_CHUNK)
            cols = slice(g * gch, (g + 1) * gch)
            sp = _dot(wm, v[rows, cols].astype(BF16)) + bias
            o_ref[rows, cols] = (u[rows, cols] * sp).astype(o_ref.dtype)


def _gmlp(proj, lnw, lnb, w_prompt, w_sample, b_prompt, b_sample, l, n_prompt_rows, groups, gch,
          col_u, col_v, tm):
    m = proj.shape[0]
    gw = groups * gch
    npt = n_prompt_rows // tm
    return pl.pallas_call(
        functools.partial(_gmlp_body, n_prompt_tiles=npt, groups=groups, gch=gch),
        grid=(m // tm,),
        in_specs=[
            pl.BlockSpec((tm, gw), lambda i: (i, col_u)),
            pl.BlockSpec((tm, gw), lambda i: (i, col_v)),
            pl.BlockSpec((None, 1, gw), lambda i: (l, 0, 0)),
            pl.BlockSpec((None, 1, gw), lambda i: (l, 0, 0)),
            pl.BlockSpec((None, groups, GM_CHUNK, GM_CHUNK), lambda i: (l, 0, 0, 0)),
            pl.BlockSpec((None, groups, GM_CHUNK, GM_CHUNK), lambda i: (l, 0, 0, 0)),
            pl.BlockSpec((None, GM_CHUNK, groups), lambda i: (l, 0, 0)),
            pl.BlockSpec((None, GM_CHUNK, groups), lambda i: (l, 0, 0)),
        ],
        out_specs=[
            pl.BlockSpec((tm, gw), lambda i: (i, 0)),
            pl.BlockSpec((tm, gw), lambda i: (jnp.maximum(i - npt, 0), 0)),
        ],
        out_shape=[jax.ShapeDtypeStruct((m, gw), BF16), jax.ShapeDtypeStruct((m - n_prompt_rows, gw), F32)],
        compiler_params=_cparams(("arbitrary",)),
        name="gmlp",
    )(proj, proj, lnw, lnb, w_prompt, w_sample, b_prompt, b_sample)


def _ssd_body(xbc_ref, z_ref, misc_ref, dtrow_ref, arow_ref, prev_ref, s0_ref, cw_ref, cb_ref, dvec_ref,
              nw_ref, o_ref, conv_ref, st_ref, xp_scr, s_scr, y_scr, *, lt, q, heads, hp, ngroups, sn, conv_k):
    t = pl.program_id(1)
    nt = pl.num_programs(1)
    xw = heads * hp
    pad = 8

    @pl.when(t == 0)
    def _():
        xp_scr[0:pad, :] = prev_ref[...]
        s_scr[...] = s0_ref[...]

    xbc = xbc_ref[...]
    xp_scr[pad:pad + lt, :] = xbc
    acc = cb_ref[...]
    for i in range(conv_k):
        off = pad - (conv_k - 1) + i
        acc = acc + xp_scr[off:off + lt, :] * cw_ref[i:i + 1, :]
    xp_scr[0:pad, :] = xbc[lt - pad:lt, :]
    act = acc * jax.nn.sigmoid(acc)

    @pl.when(t == nt - 1)
    def _():
        conv_ref[...] = xbc[lt - (conv_k - 1):lt, :]

    dt_rows = dtrow_ref[...]
    ltp = dt_rows.shape[1]
    a_rows = jnp.concatenate(
        [_lane_cumsum(arow_ref[:, j * LANES:(j + 1) * LANES], q) for j in range(ltp // LANES)], axis=1)
    row = lax.broadcasted_iota(jnp.int32, (lt, lt), 0)
    col = lax.broadcasted_iota(jnp.int32, (lt, lt), 1)
    tri_all = jnp.where((col <= row) & (col >= (row & -q)), 1.0, 0.0).astype(BF16)
    a_cols = sum(_dot(tri_all, part.astype(BF16)) for part in _split3(misc_ref[...]))
    x_all = act[:, 0:xw]
    if lt < LANES:
        x_all = jnp.concatenate([x_all, jnp.zeros((LANES - lt, xw), F32)], axis=0)
    x_t = x_all.T
    tril = lax.broadcasted_iota(jnp.int32, (q, q), 1) <= lax.broadcasted_iota(jnp.int32, (q, q), 0)
    rep = heads // ngroups
    for c in range(lt // q):
        r0 = c * q
        bmat = [act[r0:r0 + q, xw + g * sn: xw + (g + 1) * sn].astype(BF16) for g in range(ngroups)]
        cmat = [act[r0:r0 + q, xw + (ngroups + g) * sn: xw + (ngroups + g + 1) * sn].astype(BF16)
                for g in range(ngroups)]
        gmat = [_dot_nt(cmat[g], bmat[g]) for g in range(ngroups)]
        for h in range(heads):
            g = h // rep
            a_row = a_rows[h:h + 1, r0:r0 + q]
            dt_row = dt_rows[h:h + 1, r0:r0 + q]
            a_col = a_cols[r0:r0 + q, 16 + h:17 + h]
            a_last = a_row[:, q - 1:q]
            lmat = jnp.exp(jnp.where(tril, a_col - a_row, -jnp.inf))
            scores = gmat[g] * lmat * dt_row
            xh = act[r0:r0 + q, h * hp:(h + 1) * hp]
            y = _dot(scores.astype(BF16), xh.astype(BF16))
            decay_end = jnp.exp(a_last - a_row) * dt_row
            xd_t = (x_t[h * hp:(h + 1) * hp, r0:r0 + q] * decay_end).astype(BF16)
            chunk_state = _dot(xd_t, bmat[g])
            s_in = s_scr[h]
            y = y + _dot_nt(cmat[g], s_in.astype(BF16)) * jnp.exp(a_col)
            s_scr[h] = s_in * jnp.exp(a_last) + chunk_state
            y_scr[r0:r0 + q, h * hp:(h + 1) * hp] = y + dvec_ref[:, h * hp:(h + 1) * hp] * xh

    zz = z_ref[...]
    o_ref[...] = _rmsnorm(y_scr[...] * (zz * jax.nn.sigmoid(zz)), nw_ref[...]).astype(o_ref.dtype)

    @pl.when(t == nt - 1)
    def _():
        st_ref[...] = s_scr[...]


def _ssd(proj, dt_rows, a_rows, conv_prev, s0, conv_w, conv_b, dvec, norm_w, l, row0, bsz, seq, lt, q,
         heads, hp, ngroups, sn, col_xbc, col_z, col_misc):
    nt = seq // lt
    xw = heads * hp
    cch = xw + 2 * ngroups * sn
    conv_k = conv_w.shape[1]
    rb0 = row0 // lt
    ltp = max(lt, LANES)
    body = functools.partial(_ssd_body, lt=lt, q=q, heads=heads, hp=hp, ngroups=ngroups, sn=sn, conv_k=conv_k)
    return pl.pallas_call(
        body,
        grid=(bsz, nt),
        in_specs=[
            pl.BlockSpec((lt, cch), lambda b, t: (rb0 + b * nt + t, col_xbc)),
            pl.BlockSpec((lt, xw), lambda b, t: (rb0 + b * nt + t, col_z)),
            pl.BlockSpec((lt, MISC_W), lambda b, t: (rb0 + b * nt + t, col_misc)),
            pl.BlockSpec((None, heads, ltp), lambda b, t: (b, 0, t)),
            pl.BlockSpec((None, heads, ltp), lambda b, t: (b, 0, t)),
            pl.BlockSpec((None, 8, cch), lambda b, t: (b, 0, 0)),
            pl.BlockSpec((None, heads, hp, sn), lambda b, t: (b, 0, 0, 0)),
            pl.BlockSpec((None, conv_k, cch), lambda b, t: (l, 0, 0)),
            pl.BlockSpec((None, 1, cch), lambda b, t: (l, 0, 0)),
            pl.BlockSpec((None, 1, xw), lambda b, t: (l, 0, 0)),
            pl.BlockSpec((None, 1, xw), lambda b, t: (l, 0, 0)),
        ],
        out_specs=[
            pl.BlockSpec((lt, xw), lambda b, t: (b * nt + t, 0)),
            pl.BlockSpec((None, conv_k - 1, cch), lambda b, t: (b, 0, 0)),
            pl.BlockSpec((None, heads, hp, sn), lambda b, t: (b, 0, 0, 0)),
        ],
        out_shape=[
            jax.ShapeDtypeStruct((bsz * seq, xw), BF16),
            jax.ShapeDtypeStruct((bsz, conv_k - 1, cch), F32),
            jax.ShapeDtypeStruct((bsz, heads, hp, sn), F32),
        ],
        scratch_shapes=[pltpu.VMEM((lt + 8, cch), F32), pltpu.VMEM((heads, hp, sn), F32),
                        pltpu.VMEM((lt, xw), F32)],
        compiler_params=_cparams(("parallel", "arbitrary")),
        name="ssd",
    )(proj, proj, proj, dt_rows, a_rows, conv_prev, s0, conv_w, conv_b, dvec, norm_w)


def _outproj_body(foxp_ref, foxs_ref, gm_ref, ssdp_ref, ssds_ref, x_ref, nw_ref, w_ref, o_ref, *, fw, gw, rows_a):
    i = pl.program_id(0)
    last_i = pl.num_programs(0) - 1

    def run(fox, ssd):
        o = _dot(fox, w_ref[0:fw, :])
        o = o + _dot(gm_ref[...], w_ref[fw:fw + gw, :])
        o = o + _dot(ssd, w_ref[fw + gw:, :])
        o_ref[...] = x_ref[...] + _rmsnorm(o, nw_ref[...])

    pl.when(i < last_i)(lambda: run(foxp_ref[...], ssdp_ref[...]))
    pl.when(i == last_i)(lambda: run(jnp.concatenate([foxp_ref[0:rows_a, :], foxs_ref[...]], axis=0),
                                     jnp.concatenate([ssdp_ref[0:rows_a, :], ssds_ref[...]], axis=0)))


def _outproj(fox_p, fox_s, gm_o, ssd_p, ssd_s, x, norm_w3, w_out, l, tm):
    m, d = x.shape
    n_a, n_b = fox_p.shape[0], fox_s.shape[0]
    fw, gw, sw = fox_p.shape[1], gm_o.shape[1], ssd_p.shape[1]
    rows_a = n_a % tm
    assert m % tm == 0 and rows_a + n_b == tm and rows_a % 16 == 0
    return pl.pallas_call(
        functools.partial(_outproj_body, fw=fw, gw=gw, rows_a=rows_a),
        grid=(m // tm,),
        in_specs=[
            pl.BlockSpec((tm, fw), lambda i: (i, 0)),
            pl.BlockSpec((n_b, fw), lambda i: (0, 0)),
            pl.BlockSpec((tm, gw), lambda i: (i, 0)),
            pl.BlockSpec((tm, sw), lambda i: (i, 0)),
            pl.BlockSpec((n_b, sw), lambda i: (0, 0)),
            pl.BlockSpec((tm, d), lambda i: (i, 0)),
            pl.BlockSpec((None, 1, d), lambda i: (l * 6 + 3, 0, 0)),
            pl.BlockSpec((None, fw + gw + sw, d), lambda i: (l, 0, 0)),
        ],
        out_specs=pl.BlockSpec((tm, d), lambda i: (i, 0)),
        out_shape=jax.ShapeDtypeStruct((m, d), F32),
        compiler_params=_cparams(("parallel",)),
        name=f"outproj_{l}",
    )(fox_p, fox_s, gm_o, ssd_p, ssd_s, x, norm_w3, w_out)


def _head_rows(vals, bsz, seq):
    return jnp.transpose(vals.reshape(bsz, seq, vals.shape[-1]), (0, 2, 1))


def _pad_lanes(rows, width):
    return jnp.pad(rows, ((0, 0), (0, 0), (0, width - rows.shape[-1])))


def kernel(x_prompt, x_sample, cache_fox_k, cache_fox_v, cache_fox_logf, state_ssd_conv, state_ssd, norm_w,
           w_ffn_gate, w_ffn_up, w_ffn_down, w_in, fox_fb, gm_ln_w, gm_ln_b, gm_ws, gm_bs, ssd_conv_w,
           ssd_conv_b, ssd_dt_bias, ssd_a_log, ssd_d, ssd_norm_w, w_out):
    bsz, seq, d = x_prompt.shape
    dbsz, dseq, _ = x_sample.shape
    depth, _, past, heads, dh = cache_fox_k.shape
    fox_w = heads * dh
    groups, gchunk = gm_ws.shape[1], gm_ws.shape[2]
    gm_w = gm_ln_w.shape[1]
    gch = gm_w // groups
    _, _, sheads, hp, sn = state_ssd.shape
    ssd_w = sheads * hp
    cch = ssd_conv_w.shape[2]
    ngroups = (cch - ssd_w) // (2 * sn)
    conv_k = ssd_conv_w.shape[1]
    n_p, n_s = bsz * seq, dbsz * dseq
    m = n_p + n_s
    assert gchunk == GM_CHUNK and heads == 8 and sheads == 8 and dh == LANES
    assert seq % GM_CHUNK == 0 and GM_CHUNK % dseq == 0 and dseq < SSD_CHUNK and past % LANES == 0

    wg, wu, wd = w_ffn_gate, w_ffn_up.astype(BF16), w_ffn_down
    wo = w_out.astype(BF16)
    sizes = [fox_w, fox_w, fox_w, heads, gm_w, gm_w, ssd_w, cch, sheads]
    offs = [0]
    for s in sizes:
        offs.append(offs[-1] + s)
    w_t = jnp.transpose(w_in, (0, 2, 1))
    wq, wk, wv, wf, wgu, wgv, wz, wxbc, wdt = (w_t[:, offs[i]:offs[i + 1], :] for i in range(9))
    tn = 1024
    used = 3 * fox_w + 2 * gm_w + cch + ssd_w + MISC_W
    npad = -(-used // tn) * tn
    zeros_misc = jnp.zeros((depth, MISC_W - heads - 2 * sheads, d), F32)
    zeros_tail = jnp.zeros((depth, npad - used, d), F32)
    w_r = jnp.concatenate([wq, wk, wv, wgu, wgv, wxbc, wz, wf, wdt, wdt, zeros_misc, zeros_tail],
                          axis=1).astype(BF16)
    col_q, col_k, col_v = 0, fox_w // dh, 2 * fox_w // dh
    col_u, col_gv = 3 * fox_w // gm_w, 3 * fox_w // gm_w + 1
    off_xbc = 3 * fox_w + 2 * gm_w
    off_z = off_xbc + cch
    off_misc = off_z + ssd_w
    assert off_xbc % cch == 0 and off_z % ssd_w == 0 and off_misc % MISC_W == 0
    misc_in_tile = off_misc % tn
    assert off_misc // tn == npad // tn - 1 and misc_in_tile + MISC_W <= tn

    lane_pad = jnp.zeros((depth, MISC_W - heads - 2 * sheads), F32)
    bias_vec = jnp.concatenate([fox_fb, ssd_dt_bias, ssd_dt_bias, lane_pad], axis=-1)[:, None, :]
    alog_vec = jnp.concatenate([jnp.zeros((depth, heads + sheads), F32), ssd_a_log, lane_pad], axis=-1)[:, None, :]
    norm_w3 = norm_w.reshape(depth * 6, 1, d)
    lnw3, lnb3 = gm_ln_w[:, None, :], gm_ln_b[:, None, :]
    reps = GM_CHUNK // dseq
    eye = jnp.eye(reps, dtype=F32)
    ws_s = jnp.einsum('ab,lgts->lgatbs', eye, gm_ws[:, :, :dseq, :dseq]).reshape(depth, groups, GM_CHUNK, GM_CHUNK)
    bs_p = jnp.transpose(gm_bs, (0, 2, 1))
    bs_s = jnp.tile(jnp.transpose(gm_bs[:, :, :dseq], (0, 2, 1)), (1, reps, 1))
    conv_b3 = ssd_conv_b[:, None, :]
    dvec = jnp.repeat(ssd_d, hp, axis=-1)[:, None, :]
    snw3 = ssd_norm_w[:, None, :]
    cache_k = cache_fox_k.reshape(depth, dbsz, past * heads, dh)
    cache_v = cache_fox_v.reshape(depth, dbsz, past * heads, dh)
    conv_prev_p = jnp.zeros((bsz, 8, cch), F32)
    conv_prev_s = jnp.pad(state_ssd_conv, ((0, 0), (0, 0), (8 - (conv_k - 1), 0), (0, 0)))
    s0_p = jnp.zeros((bsz, sheads, hp, sn), F32)

    x = (x_prompt.reshape(n_p, d), x_sample.reshape(n_s, d))
    tm_big = m // 8 if m % 64 == 0 and (m // 8) % 16 == 0 else 256
    tm_ffn = m // 12 if m % 12 == 0 and (m // 12) % 16 == 0 else 256
    tm_out = m // 16 if m % 16 == 0 and (m // 16) % 16 == 0 else 256
    lf_len = 2 * past
    assert past + dseq <= lf_len

    outs = {k: [] for k in ("plf", "pconv", "pst", "slf", "sgm", "sconv", "sst")}
    kv_p = kv_s = None
    for l in range(depth):
        x = _ffn(x, norm_w3, wg, wu, wd, l, 0, tm_ffn, 512)
        proj = _inproj(x, norm_w3, w_r, bias_vec, alog_vec, l, tm_big, tn, misc_in_tile)
        misc = proj[:, off_misc:off_misc + heads + 2 * sheads]
        logf_p, logf_s = misc[:n_p, :heads], misc[n_p:, :heads]
        lf_rows_p = _head_rows(logf_p, bsz, seq).reshape(bsz, heads, seq // LANES, LANES)
        lf_all = jnp.concatenate([cache_fox_logf[l], logf_s.reshape(dbsz, dseq, heads),
                                  jnp.zeros((dbsz, lf_len - past - dseq, heads), F32)], axis=1)
        lf_rows_s = jnp.transpose(lf_all, (0, 2, 1)).reshape(dbsz, heads, lf_len // LANES, LANES)
        dt_rows_p = _head_rows(misc[:n_p, heads:heads + sheads], bsz, seq)
        a_rows_p = _head_rows(misc[:n_p, heads + sheads:], bsz, seq)
        dt_rows_s = _pad_lanes(_head_rows(misc[n_p:, heads:heads + sheads], dbsz, dseq), LANES)
        a_rows_s = _pad_lanes(_head_rows(misc[n_p:, heads + sheads:], dbsz, dseq), LANES)

        fox_p, *kv_p = _fox_prompt(proj, lf_rows_p, kv_p, l, depth, bsz, seq, heads, dh, col_q, col_k, col_v, 256)
        fox_s, *kv_s = _fox_sample(proj, lf_rows_s, cache_k, cache_v, kv_s, l, n_p, dbsz, dseq, heads, dh, 0, 1, 2)
        gm_o, gm_v_s = _gmlp(proj, lnw3, lnb3, gm_ws, ws_s, bs_p, bs_s, l, n_p, groups, gch, col_u, col_gv, 256)
        ssd_p, conv_p, st_p = _ssd(proj, dt_rows_p, a_rows_p, conv_prev_p, s0_p, ssd_conv_w, conv_b3, dvec, snw3,
                                   l, 0, bsz, seq, 256, SSD_CHUNK, sheads, hp, ngroups, sn,
                                   off_xbc // cch, off_z // ssd_w, off_misc // MISC_W)
        ssd_s, conv_s, st_s = _ssd(proj, dt_rows_s, a_rows_s, conv_prev_s[l], state_ssd[l], ssd_conv_w, conv_b3,
                                   dvec, snw3, l, n_p, dbsz, dseq, dseq, dseq, sheads, hp, ngroups, sn,
                                   off_xbc // cch, off_z // ssd_w, off_misc // MISC_W)
        x = _outproj(fox_p, fox_s, gm_o, ssd_p, ssd_s, x, norm_w3, wo, l, tm_out)
        x = _ffn(x, norm_w3, wg, wu, wd, l, 1, tm_ffn, 512,
                 split_out_rows=(n_p, n_s) if l == depth - 1 else None)

        outs["plf"].append(logf_p.reshape(bsz, seq, heads))
        outs["pconv"].append(conv_p)
        outs["pst"].append(st_p)
        outs["slf"].append(logf_s.reshape(dbsz, dseq, heads))
        outs["sgm"].append(gm_v_s.reshape(dbsz, dseq, gm_w))
        outs["sconv"].append(conv_s)
        outs["sst"].append(st_s)

    st = {k: jnp.stack(v) for k, v in outs.items()}
    y_prompt = x[0].reshape(bsz, seq, d)
    y_sample = x[1].reshape(dbsz, dseq, d)
    p_k, p_v = (a.reshape(depth, bsz, seq, heads, dh) for a in kv_p)
    s_k, s_v = (a.reshape(depth, dbsz, dseq, heads, dh) for a in kv_s)
    return (y_prompt, y_sample, p_k, p_v, st["plf"], st["pconv"], st["pst"],
            s_k, s_v, st["slf"], st["sgm"], st["sconv"], st["sst"])
```

```python
import functools

import jax
import jax.numpy as jnp
from jax import lax
from jax.experimental import pallas as pl
from jax.experimental.pallas import tpu as pltpu

F32 = jnp.float32
BF16 = jnp.bfloat16
EPS = 1e-6
LANES = 128
SSD_CHUNK = 64
GM_CHUNK = 128
VMEM_LIMIT = 56 * 1024 * 1024
FFN_VMEM_LIMIT = 60 * 1024 * 1024

MISC_W = LANES


def _cparams(sem, vmem_limit=VMEM_LIMIT):
    return pltpu.CompilerParams(dimension_semantics=sem, vmem_limit_bytes=vmem_limit)


def _rmsnorm(x, w):
    return x * lax.rsqrt(jnp.mean(x * x, axis=-1, keepdims=True) + EPS) * w


def _softplus(x):
    return jnp.maximum(x, 0.0) + jnp.log1p(jnp.exp(-jnp.abs(x)))


def _dot(a, b):
    return jnp.dot(a, b, preferred_element_type=F32)


def _dot_nt(a, b):
    return lax.dot_general(a, b, (((1,), (1,)), ((), ())), preferred_element_type=F32)


def _dot_tn(a, b):
    return lax.dot_general(a, b, (((0,), (0,)), ((), ())), preferred_element_type=F32)


def _row_to_col(row):
    n = row.shape[1]
    eye = lax.broadcasted_iota(jnp.int32, (n, n), 0) == lax.broadcasted_iota(jnp.int32, (n, n), 1)
    return jnp.sum(jnp.where(eye, jnp.broadcast_to(row, (n, n)), 0.0), axis=1, keepdims=True)


def _lane_cumsum(x, seg):
    lane = lax.broadcasted_iota(jnp.int32, x.shape, 1) & (seg - 1)
    k = 1
    while k < seg:
        x = x + jnp.where(lane >= k, pltpu.roll(x, k, axis=1), 0.0)
        k *= 2
    return x


def _flat_cumsum(x):
    x = _lane_cumsum(x, LANES)
    tot = x[:, LANES - 1:LANES]
    rows = []
    carry = jnp.zeros((1, 1), F32)
    for r in range(x.shape[0]):
        rows.append(x[r:r + 1, :] + carry)
        carry = carry + tot[r:r + 1, :]
    return jnp.concatenate(rows, axis=0)


def _ffn_body(*refs, split_in, split_out, rows_a):
    refs = list(refs)
    x_ref = refs.pop(0)
    xs_ref = refs.pop(0) if split_in else None
    nwa_ref, nwb_ref, wg_ref, wu_ref, wd_ref, o_ref = refs[:6]
    os_ref = refs[6] if split_out else None
    h_ref = refs[-1]
    i, f = pl.program_id(0), pl.program_id(1)
    last_i, last_f = pl.num_programs(0) - 1, pl.num_programs(1) - 1

    tm = o_ref.shape[0]
    whole = [(slice(0, tm), x_ref, slice(0, tm))]
    pieces_last = [(slice(0, rows_a), x_ref, slice(0, rows_a)), (slice(rows_a, tm), xs_ref, slice(None))]

    def per_piece(fn):
        if not split_in:
            fn(whole)
            return
        pl.when(i < last_i)(lambda: fn(whole))
        pl.when(i == last_i)(lambda: fn(pieces_last))

    def start(pieces):
        for rows, ref, src in pieces:
            h_ref[rows, :] = _rmsnorm(ref[src, :], nwa_ref[...]).astype(BF16)
        o_ref[...] = jnp.zeros_like(o_ref)

    pl.when(f == 0)(lambda: per_piece(start))

    h = h_ref[...]
    a = _dot(h, wg_ref[...].astype(BF16))
    g = _dot(h, wu_ref[...])
    act = (a * jax.nn.sigmoid(a) * g).astype(BF16)
    o_ref[...] += _dot(act, wd_ref[...].astype(BF16))

    def finish(pieces):
        for rows, ref, src in pieces:
            o_ref[rows, :] = ref[src, :] + 0.5 * _rmsnorm(o_ref[rows, :], nwb_ref[...])
        if split_out:
            @pl.when(i == last_i)
            def _():
                os_ref[...] = o_ref[rows_a:, :]

    pl.when(f == last_f)(lambda: per_piece(finish))


def _ffn(x, norm_w3, wg, wu, wd, l, j, tm, tf, split_out_rows=None):
    split_in = isinstance(x, tuple)
    split_out = split_out_rows is not None
    xs = list(x) if split_in else [x]
    n_a, n_b = (xs[0].shape[0], xs[1].shape[0]) if split_in else (split_out_rows or (x.shape[0], 0))
    m, d = n_a + n_b, xs[0].shape[1]
    dff = wg.shape[-1]
    rows_a = n_a % tm
    assert m % tm == 0 and (not (split_in or split_out) or (rows_a + n_b == tm and rows_a % 8 == 0))
    row_spec = pl.BlockSpec((tm, d), lambda i, f: (i, 0))
    tail_spec = pl.BlockSpec((n_b, d), lambda i, f: (0, 0), pipeline_mode=pl.Buffered(1))
    out_shape = jax.ShapeDtypeStruct((m, d), F32)
    if split_out:
        out_shape = [jax.ShapeDtypeStruct((n_a, d), F32), jax.ShapeDtypeStruct((n_b, d), F32)]
    return pl.pallas_call(
        functools.partial(_ffn_body, split_in=split_in, split_out=split_out, rows_a=rows_a),
        grid=(m // tm, dff // tf),
        in_specs=[row_spec] + ([tail_spec] if split_in else []) + [
            pl.BlockSpec((None, 1, d), lambda i, f: (l * 6 + 3 * j + j, 0, 0)),
            pl.BlockSpec((None, 1, d), lambda i, f: (l * 6 + 3 * j + j + 1, 0, 0)),
            pl.BlockSpec((None, None, d, tf), lambda i, f: (l, j, 0, f)),
            pl.BlockSpec((None, None, d, tf), lambda i, f: (l, j, 0, f)),
            pl.BlockSpec((None, None, tf, d), lambda i, f: (l, j, f, 0)),
        ],
        out_specs=[row_spec, tail_spec] if split_out else row_spec,
        out_shape=out_shape,
        scratch_shapes=[pltpu.VMEM((tm, d), BF16)],
        compiler_params=_cparams(("parallel", "arbitrary"), FFN_VMEM_LIMIT),
        name=f"ffn_{l}_{j}",
    )(*xs, norm_w3, norm_w3, wg, wu, wd)


def _inproj_body(x_ref, nw_ref, w_ref, bias_ref, alog_ref, o_ref, h_ref, *, misc_off):
    j = pl.program_id(1)

    @pl.when(j == 0)
    def _():
        h_ref[...] = _rmsnorm(x_ref[...], nw_ref[...]).astype(BF16)

    acc = _dot_nt(h_ref[...], w_ref[...])
    o_ref[...] = acc

    @pl.when(j == pl.num_programs(1) - 1)
    def _():
        raw = acc[:, misc_off:misc_off + MISC_W] + bias_ref[...]
        lane = lax.broadcasted_iota(jnp.int32, raw.shape, 1)
        sp = _softplus(raw)
        logf = -_softplus(-raw)
        neg_a = -jnp.exp(alog_ref[...])
        val = jnp.where(lane < 8, logf, jnp.where(lane < 16, sp, jnp.where(lane < 24, sp * neg_a, 0.0)))
        o_ref[:, misc_off:misc_off + MISC_W] = val


def _inproj(x, norm_w3, w_r, bias_vec, alog_vec, l, tm, tn, misc_off):
    m, d = x.shape
    npad = w_r.shape[1]
    return pl.pallas_call(
        functools.partial(_inproj_body, misc_off=misc_off),
        grid=(m // tm, npad // tn),
        in_specs=[
            pl.BlockSpec((tm, d), lambda i, j: (i, 0)),
            pl.BlockSpec((None, 1, d), lambda i, j: (l * 6 + 2, 0, 0)),
            pl.BlockSpec((None, tn, d), lambda i, j: (l, j, 0)),
            pl.BlockSpec((None, 1, MISC_W), lambda i, j: (l, 0, 0)),
            pl.BlockSpec((None, 1, MISC_W), lambda i, j: (l, 0, 0)),
        ],
        out_specs=pl.BlockSpec((tm, tn), lambda i, j: (i, j)),
        out_shape=jax.ShapeDtypeStruct((m, npad), F32),
        scratch_shapes=[pltpu.VMEM((tm, d), BF16)],
        compiler_params=_cparams(("parallel", "arbitrary")),
        name=f"inproj_{l}",
    )(x, norm_w3, w_r, bias_vec, alog_vec)


def _rows_as_lane_vector(c_ref, start, count):
    rows = c_ref[pl.ds(start, count), :]
    return jnp.concatenate([rows[r:r + 1, :] for r in range(count)], axis=1)


def _store_layer_rows(out_ref, val, l, aliased):
    if aliased:
        out_ref[...] = val
    else:
        for slot in range(out_ref.shape[0]):
            out_ref[slot] = val if slot == l else jnp.zeros_like(val)


def _split3(x):
    hi = x.astype(BF16).astype(F32)
    r = x - hi
    mid = r.astype(BF16).astype(F32)
    return hi, mid, r - mid


def _fox_head(hh, lf_ref, q_ref, k_ref, v_ref, *rest, tq, dh, scale, l, aliased):
    o_ref, ko_ref, vo_ref = rest[2 * aliased:2 * aliased + 3]
    c_scr, qx_scr, kx_scr, vb_scr, s_scr, p_scr = (scr.at[hh] for scr in rest[2 * aliased + 3:])
    seq = q_ref.shape[0]
    cols = slice(hh * dh, (hh + 1) * dh)
    nq = seq // tq
    sub = tq // LANES
    log2e = 1.4426950408889634
    c_scr[...] = _flat_cumsum(lf_ref[hh]) * log2e
    lane = lax.broadcasted_iota(jnp.int32, (tq, LANES), 1)

    def prepare(i):
        rows = slice(i * tq, (i + 1) * tq)
        hi, mid, lo = _split3(_row_to_col(_rows_as_lane_vector(c_scr, i * sub, sub)))
        q_ext = jnp.where(lane == 0, hi, jnp.where(lane == 1, mid, jnp.where(lane == 2, lo,
                          jnp.where(lane < 6, 1.0, 0.0))))
        k_ext = jnp.where(lane < 3, 1.0, jnp.where(lane == 3, -hi, jnp.where(lane == 4, -mid,
                          jnp.where(lane == 5, -lo, 0.0))))
        k_rows, v_rows = k_ref[rows, cols], v_ref[rows, cols]
        qx_scr[rows, 0:dh] = (q_ref[rows, cols] * (scale * log2e)).astype(BF16)
        qx_scr[rows, dh:dh + LANES] = q_ext.astype(BF16)
        kx_scr[0:dh, rows] = k_rows.T.astype(BF16)
        kx_scr[dh:dh + LANES, rows] = k_ext.T.astype(BF16)
        vb_scr[rows, :] = v_rows.astype(BF16)
        if aliased:
            ko_ref[rows, cols] = k_rows
            vo_ref[rows, cols] = v_rows
        else:
            for slot in range(ko_ref.shape[0]):
                ko_ref[slot, rows, cols] = k_rows if slot == l else jnp.zeros_like(k_rows)
                vo_ref[slot, rows, cols] = v_rows if slot == l else jnp.zeros_like(v_rows)

    causal = lax.broadcasted_iota(jnp.int32, (tq, tq), 1) <= lax.broadcasted_iota(jnp.int32, (tq, tq), 0)
    slab = 64
    n_slabs = tq // slab

    def scores(i, j):
        s = _dot(qx_scr[i * tq:(i + 1) * tq, :], kx_scr[:, j * tq:(j + 1) * tq])
        if j == i:
            s = jnp.where(causal, s, -jnp.inf)
        s_scr[i % 2, :, j * tq:(j + 1) * tq] = s

    def softmax_slab(i, r):
        nk = (i + 1) * tq
        rs = slice(r * slab, (r + 1) * slab)
        m_acc = s_scr[i % 2, rs, 0:LANES]
        for u in range(1, nk // LANES):
            m_acc = jnp.maximum(m_acc, s_scr[i % 2, rs, u * LANES:(u + 1) * LANES])
        m = jnp.broadcast_to(jnp.max(m_acc, axis=1, keepdims=True), (slab, LANES))
        l_acc = jnp.zeros((slab, LANES), F32)
        for u in range(nk // LANES):
            p = jnp.exp2(s_scr[i % 2, rs, u * LANES:(u + 1) * LANES] - m)
            l_acc = l_acc + p
            p_scr[i % 2, rs, u * LANES:(u + 1) * LANES] = p.astype(BF16)
        return jnp.sum(l_acc, axis=1, keepdims=True)

    def weighted_values(i, denom):
        nk = (i + 1) * tq
        acc = _dot(p_scr[i % 2, :, 0:nk], vb_scr[0:nk, :])
        o_ref[i * tq:(i + 1) * tq, cols] = (acc / jnp.concatenate(denom, axis=0)).astype(o_ref.dtype)

    prepare(0)
    scores(0, 0)
    denom_prev = None
    for i in range(nq):
        next_keys = list(range(i + 2)) if i + 1 < nq else []
        if next_keys:
            prepare(i + 1)
        denom = []
        for r in range(n_slabs):
            for j in next_keys[r::n_slabs]:
                scores(i + 1, j)
            if r == 1 and denom_prev is not None:
                weighted_values(i - 1, denom_prev)
            denom.append(softmax_slab(i, r))
        denom_prev = denom
    weighted_values(nq - 1, denom_prev)


def _fox_prompt_body(*refs, dh, **kw):
    for hh in range(refs[1].shape[1] // dh):
        _fox_head(hh, *refs, dh=dh, **kw)


def _fox_prompt(proj, lf_rows, kv_prev, l, depth, bsz, seq, heads, dh, col_q, col_k, col_v, tq, hps):
    n = seq // LANES
    aliased = kv_prev is not None
    kv_shape = jax.ShapeDtypeStruct((depth, bsz * seq, heads * dh), F32)
    gw = hps * dh
    assert heads % hps == 0 and col_q % hps == 0 and col_k % hps == 0 and col_v % hps == 0
    kv_spec = (pl.BlockSpec((None, seq, gw), lambda b, h: (l, b, h)) if aliased else
               pl.BlockSpec((depth, seq, gw), lambda b, h: (0, b, h)))
    in_specs = [
        pl.BlockSpec((None, hps, n, LANES), lambda b, h: (b, h, 0, 0)),
        pl.BlockSpec((seq, gw), lambda b, h: (b, col_q // hps + h)),
        pl.BlockSpec((seq, gw), lambda b, h: (b, col_k // hps + h)),
        pl.BlockSpec((seq, gw), lambda b, h: (b, col_v // hps + h)),
    ]
    args = [lf_rows, proj, proj, proj]
    if aliased:
        in_specs += [pl.BlockSpec(memory_space=pl.ANY)] * 2
        args += list(kv_prev)
    return pl.pallas_call(
        functools.partial(_fox_prompt_body, tq=tq, dh=dh, scale=dh ** -0.5, l=l, aliased=aliased),
        grid=(bsz, heads // hps),
        in_specs=in_specs,
        out_specs=[pl.BlockSpec((seq, gw), lambda b, h: (b, h)), kv_spec, kv_spec],
        out_shape=[jax.ShapeDtypeStruct((bsz * seq, heads * dh), BF16), kv_shape, kv_shape],
        scratch_shapes=[pltpu.VMEM((hps, n, LANES), F32), pltpu.VMEM((hps, seq, dh + LANES), BF16),
                        pltpu.VMEM((hps, dh + LANES, seq), BF16), pltpu.VMEM((hps, seq, dh), BF16),
                        pltpu.VMEM((hps, 2, tq, seq), F32), pltpu.VMEM((hps, 2, tq, seq), BF16)],
        input_output_aliases={4: 1, 5: 2} if aliased else {},
        compiler_params=_cparams(("parallel", "parallel")),
        name="fox_prompt",
    )(*args)


def _fox_sample_body(lf_ref, q_ref, k_ref, v_ref, ck_ref, cv_ref, *rest, past, new, heads, dh, scale, l,
                     aliased):
    o_ref, ko_ref, vo_ref = rest[2 * aliased:]
    _store_layer_rows(ko_ref, k_ref[...], l, aliased)
    _store_layer_rows(vo_ref, v_ref[...], l, aliased)
    n_past = past // LANES
    causal = lax.broadcasted_iota(jnp.int32, (new, new), 1) <= lax.broadcasted_iota(jnp.int32, (new, new), 0)
    for h in range(heads):
        cols = slice(h * dh, (h + 1) * dh)
        c = _flat_cumsum(lf_ref[h])
        c_past = jnp.concatenate([c[r:r + 1, :] for r in range(n_past)], axis=1)
        c_new = c[n_past:n_past + 1, 0:new]
        cq_col = _row_to_col(c_new)
        q = q_ref[:, cols].astype(BF16)
        head_rows = pl.ds(h, past, stride=heads)
        s_past = _dot_nt(q, ck_ref[head_rows, :].astype(BF16)) * scale + cq_col - c_past
        s_new = _dot_nt(q, k_ref[:, cols].astype(BF16)) * scale + cq_col - c_new
        s_new = jnp.where(causal, s_new, -jnp.inf)
        m = jnp.maximum(jnp.max(s_past, axis=1, keepdims=True), jnp.max(s_new, axis=1, keepdims=True))
        p_past = jnp.exp(s_past - m)
        p_new = jnp.exp(s_new - m)
        denom = jnp.sum(p_past, axis=1, keepdims=True) + jnp.sum(p_new, axis=1, keepdims=True)
        acc = _dot(p_past.astype(BF16), cv_ref[head_rows, :].astype(BF16)) + _dot(
            p_new.astype(BF16), v_ref[:, cols].astype(BF16))
        o_ref[:, cols] = (acc / denom).astype(o_ref.dtype)


def _fox_sample(proj, lf_rows, cache_k, cache_v, kv_prev, l, row0, bsz, new, heads, dh, col_q, col_k, col_v):
    depth = cache_k.shape[0]
    past = cache_k.shape[2] // heads
    fw = heads * dh
    n = lf_rows.shape[2]
    rb0 = row0 // new
    aliased = kv_prev is not None
    kv_shape = jax.ShapeDtypeStruct((depth, bsz * new, fw), F32)
    kv_spec = (pl.BlockSpec((None, new, fw), lambda b: (l, b, 0)) if aliased else
               pl.BlockSpec((depth, new, fw), lambda b: (0, b, 0)))
    in_specs = [
        pl.BlockSpec((None, heads, n, LANES), lambda b: (b, 0, 0, 0)),
        pl.BlockSpec((new, fw), lambda b: (rb0 + b, col_q)),
        pl.BlockSpec((new, fw), lambda b: (rb0 + b, col_k)),
        pl.BlockSpec((new, fw), lambda b: (rb0 + b, col_v)),
        pl.BlockSpec((None, None, past * heads, dh), lambda b: (l, b, 0, 0)),
        pl.BlockSpec((None, None, past * heads, dh), lambda b: (l, b, 0, 0)),
    ]
    args = [lf_rows, proj, proj, proj, cache_k, cache_v]
    if aliased:
        in_specs += [pl.BlockSpec(memory_space=pl.ANY)] * 2
        args += list(kv_prev)
    return pl.pallas_call(
        functools.partial(_fox_sample_body, past=past, new=new, heads=heads, dh=dh, scale=dh ** -0.5,
                          l=l, aliased=aliased),
        grid=(bsz,),
        in_specs=in_specs,
        out_specs=[pl.BlockSpec((new, fw), lambda b: (b, 0)), kv_spec, kv_spec],
        out_shape=[jax.ShapeDtypeStruct((bsz * new, fw), BF16), kv_shape, kv_shape],
        input_output_aliases={6: 1, 7: 2} if aliased else {},
        compiler_params=_cparams(("parallel",)),
        name="fox_sample",
    )(*args)


def _gmlp_body(gu_ref, gv_ref, lnw_ref, lnb_ref, wp_ref, ws_ref, bp_ref, bs_ref, o_ref, v_ref, *,
               n_prompt_tiles, groups, gch):
    i = pl.program_id(0)
    is_sample = i >= n_prompt_tiles
    u = jax.nn.gelu(gu_ref[...])
    gv = jax.nn.gelu(gv_ref[...])
    mu = jnp.mean(gv, axis=-1, keepdims=True)
    var = jnp.mean(jnp.square(gv - mu), axis=-1, keepdims=True)
    v = (gv - mu) * lax.rsqrt(var + EPS) * lnw_ref[...] + lnb_ref[...]

    @pl.when(is_sample)
    def _():
        v_ref[...] = v

    tm = u.shape[0]
    tril = lax.broadcasted_iota(jnp.int32, (GM_CHUNK, GM_CHUNK), 1) <= lax.broadcasted_iota(
        jnp.int32, (GM_CHUNK, GM_CHUNK), 0)
    bias_all = jnp.where(is_sample, bs_ref[...], bp_ref[...])
    for g in range(groups):
        wm = jnp.where(is_sample, ws_ref[g], wp_ref[g])
        wm = jnp.where(tril, wm, 0.0).astype(BF16)
        bias = bias_all[:, g:g + 1]
        for c in range(tm // GM_CHUNK):
            rows = slice(c * GM_CHUNK, (c + 1) * GM_CHUNK)
            cols = slice(g * gch, (g + 1) * gch)
            sp = _dot(wm, v[rows, cols].astype(BF16)) + bias
            o_ref[rows, cols] = (u[rows, cols] * sp).astype(o_ref.dtype)


def _gmlp(proj, lnw, lnb, w_prompt, w_sample, b_prompt, b_sample, l, n_prompt_rows, groups, gch,
          col_u, col_v, tm):
    m = proj.shape[0]
    gw = groups * gch
    npt = n_prompt_rows // tm
    return pl.pallas_call(
        functools.partial(_gmlp_body, n_prompt_tiles=npt, groups=groups, gch=gch),
        grid=(m // tm,),
        in_specs=[
            pl.BlockSpec((tm, gw), lambda i: (i, col_u)),
            pl.BlockSpec((tm, gw), lambda i: (i, col_v)),
            pl.BlockSpec((None, 1, gw), lambda i: (l, 0, 0)),
            pl.BlockSpec((None, 1, gw), lambda i: (l, 0, 0)),
            pl.BlockSpec((None, groups, GM_CHUNK, GM_CHUNK), lambda i: (l, 0, 0, 0)),
            pl.BlockSpec((None, groups, GM_CHUNK, GM_CHUNK), lambda i: (l, 0, 0, 0)),
            pl.BlockSpec((None, GM_CHUNK, groups), lambda i: (l, 0, 0)),
            pl.BlockSpec((None, GM_CHUNK, groups), lambda i: (l, 0, 0)),
        ],
        out_specs=[
            pl.BlockSpec((tm, gw), lambda i: (i, 0)),
            pl.BlockSpec((tm, gw), lambda i: (jnp.maximum(i - npt, 0), 0)),
        ],
        out_shape=[jax.ShapeDtypeStruct((m, gw), BF16), jax.ShapeDtypeStruct((m - n_prompt_rows, gw), F32)],
        compiler_params=_cparams(("arbitrary",)),
        name="gmlp",
    )(proj, proj, lnw, lnb, w_prompt, w_sample, b_prompt, b_sample)


def _ssd_body(xbc_ref, z_ref, misc_ref, dtrow_ref, arow_ref, prev_ref, s0_ref, cw_ref, cb_ref, dvec_ref,
              nw_ref, o_ref, conv_ref, st_ref, xp_scr, s_scr, y_scr, *, lt, q, heads, hp, ngroups, sn, conv_k):
    t = pl.program_id(1)
    nt = pl.num_programs(1)
    xw = heads * hp
    pad = 8

    @pl.when(t == 0)
    def _():
        xp_scr[0:pad, :] = prev_ref[...]
        s_scr[...] = s0_ref[...]

    xbc = xbc_ref[...]
    xp_scr[pad:pad + lt, :] = xbc
    acc = cb_ref[...]
    for i in range(conv_k):
        off = pad - (conv_k - 1) + i
        acc = acc + xp_scr[off:off + lt, :] * cw_ref[i:i + 1, :]
    xp_scr[0:pad, :] = xbc[lt - pad:lt, :]
    act = acc * jax.nn.sigmoid(acc)

    @pl.when(t == nt - 1)
    def _():
        conv_ref[...] = xbc[lt - (conv_k - 1):lt, :]

    dt_rows = dtrow_ref[...]
    ltp = dt_rows.shape[1]
    a_rows = jnp.concatenate(
        [_lane_cumsum(arow_ref[:, j * LANES:(j + 1) * LANES], q) for j in range(ltp // LANES)], axis=1)
    row = lax.broadcasted_iota(jnp.int32, (lt, lt), 0)
    col = lax.broadcasted_iota(jnp.int32, (lt, lt), 1)
    tri_all = jnp.where((col <= row) & (col >= (row & -q)), 1.0, 0.0).astype(BF16)
    a_cols = sum(_dot(tri_all, part.astype(BF16)) for part in _split3(misc_ref[...]))
    x_all = act[:, 0:xw]
    if lt < LANES:
        x_all = jnp.concatenate([x_all, jnp.zeros((LANES - lt, xw), F32)], axis=0)
    x_t = x_all.T
    tril = lax.broadcasted_iota(jnp.int32, (q, q), 1) <= lax.broadcasted_iota(jnp.int32, (q, q), 0)
    rep = heads // ngroups
    for c in range(lt // q):
        r0 = c * q
        bmat = [act[r0:r0 + q, xw + g * sn: xw + (g + 1) * sn].astype(BF16) for g in range(ngroups)]
        cmat = [act[r0:r0 + q, xw + (ngroups + g) * sn: xw + (ngroups + g + 1) * sn].astype(BF16)
                for g in range(ngroups)]
        gmat = [_dot_nt(cmat[g], bmat[g]) for g in range(ngroups)]
        for h in range(heads):
            g = h // rep
            a_row = a_rows[h:h + 1, r0:r0 + q]
            dt_row = dt_rows[h:h + 1, r0:r0 + q]
            a_col = a_cols[r0:r0 + q, 16 + h:17 + h]
            a_last = a_row[:, q - 1:q]
            lmat = jnp.exp(jnp.where(tril, a_col - a_row, -jnp.inf))
            scores = gmat[g] * lmat * dt_row
            xh = act[r0:r0 + q, h * hp:(h + 1) * hp]
            y = _dot(scores.astype(BF16), xh.astype(BF16))
            decay_end = jnp.exp(a_last - a_row) * dt_row
            xd_t = (x_t[h * hp:(h + 1) * hp, r0:r0 + q] * decay_end).astype(BF16)
            chunk_state = _dot(xd_t, bmat[g])
            s_in = s_scr[h]
            y = y + _dot_nt(cmat[g], s_in.astype(BF16)) * jnp.exp(a_col)
            s_scr[h] = s_in * jnp.exp(a_last) + chunk_state
            y_scr[r0:r0 + q, h * hp:(h + 1) * hp] = y + dvec_ref[:, h * hp:(h + 1) * hp] * xh

    zz = z_ref[...]
    o_ref[...] = _rmsnorm(y_scr[...] * (zz * jax.nn.sigmoid(zz)), nw_ref[...]).astype(o_ref.dtype)

    @pl.when(t == nt - 1)
    def _():
        st_ref[...] = s_scr[...]


def _ssd(proj, dt_rows, a_rows, conv_prev, s0, conv_w, conv_b, dvec, norm_w, l, row0, bsz, seq, lt, q,
         heads, hp, ngroups, sn, col_xbc, col_z, col_misc):
    nt = seq // lt
    xw = heads * hp
    cch = xw + 2 * ngroups * sn
    conv_k = conv_w.shape[1]
    rb0 = row0 // lt
    ltp = max(lt, LANES)
    body = functools.partial(_ssd_body, lt=lt, q=q, heads=heads, hp=hp, ngroups=ngroups, sn=sn, conv_k=conv_k)
    return pl.pallas_call(
        body,
        grid=(bsz, nt),
        in_specs=[
            pl.BlockSpec((lt, cch), lambda b, t: (rb0 + b * nt + t, col_xbc)),
            pl.BlockSpec((lt, xw), lambda b, t: (rb0 + b * nt + t, col_z)),
            pl.BlockSpec((lt, MISC_W), lambda b, t: (rb0 + b * nt + t, col_misc)),
            pl.BlockSpec((None, heads, ltp), lambda b, t: (b, 0, t)),
            pl.BlockSpec((None, heads, ltp), lambda b, t: (b, 0, t)),
            pl.BlockSpec((None, 8, cch), lambda b, t: (b, 0, 0)),
            pl.BlockSpec((None, heads, hp, sn), lambda b, t: (b, 0, 0, 0)),
            pl.BlockSpec((None, conv_k, cch), lambda b, t: (l, 0, 0)),
            pl.BlockSpec((None, 1, cch), lambda b, t: (l, 0, 0)),
            pl.BlockSpec((None, 1, xw), lambda b, t: (l, 0, 0)),
            pl.BlockSpec((None, 1, xw), lambda b, t: (l, 0, 0)),
        ],
        out_specs=[
            pl.BlockSpec((lt, xw), lambda b, t: (b * nt + t, 0)),
            pl.BlockSpec((None, conv_k - 1, cch), lambda b, t: (b, 0, 0)),
            pl.BlockSpec((None, heads, hp, sn), lambda b, t: (b, 0, 0, 0)),
        ],
        out_shape=[
            jax.ShapeDtypeStruct((bsz * seq, xw), BF16),
            jax.ShapeDtypeStruct((bsz, conv_k - 1, cch), F32),
            jax.ShapeDtypeStruct((bsz, heads, hp, sn), F32),
        ],
        scratch_shapes=[pltpu.VMEM((lt + 8, cch), F32), pltpu.VMEM((heads, hp, sn), F32),
                        pltpu.VMEM((lt, xw), F32)],
        compiler_params=_cparams(("parallel", "arbitrary")),
        name="ssd",
    )(proj, proj, proj, dt_rows, a_rows, conv_prev, s0, conv_w, conv_b, dvec, norm_w)


def _outproj_body(foxp_ref, foxs_ref, gm_ref, ssdp_ref, ssds_ref, x_ref, nw_ref, w_ref, o_ref, *, fw, gw, rows_a):
    i = pl.program_id(0)
    last_i = pl.num_programs(0) - 1

    def run(fox, ssd):
        o = _dot(fox, w_ref[0:fw, :])
        o = o + _dot(gm_ref[...], w_ref[fw:fw + gw, :])
        o = o + _dot(ssd, w_ref[fw + gw:, :])
        o_ref[...] = x_ref[...] + _rmsnorm(o, nw_ref[...])

    pl.when(i < last_i)(lambda: run(foxp_ref[...], ssdp_ref[...]))
    pl.when(i == last_i)(lambda: run(jnp.concatenate([foxp_ref[0:rows_a, :], foxs_ref[...]], axis=0),
                                     jnp.concatenate([ssdp_ref[0:rows_a, :], ssds_ref[...]], axis=0)))


def _outproj(fox_p, fox_s, gm_o, ssd_p, ssd_s, x, norm_w3, w_out, l, tm):
    m, d = x.shape
    n_a, n_b = fox_p.shape[0], fox_s.shape[0]
    fw, gw, sw = fox_p.shape[1], gm_o.shape[1], ssd_p.shape[1]
    rows_a = n_a % tm
    assert m % tm == 0 and rows_a + n_b == tm and rows_a % 16 == 0
    return pl.pallas_call(
        functools.partial(_outproj_body, fw=fw, gw=gw, rows_a=rows_a),
        grid=(m // tm,),
        in_specs=[
            pl.BlockSpec((tm, fw), lambda i: (i, 0)),
            pl.BlockSpec((n_b, fw), lambda i: (0, 0)),
            pl.BlockSpec((tm, gw), lambda i: (i, 0)),
            pl.BlockSpec((tm, sw), lambda i: (i, 0)),
            pl.BlockSpec((n_b, sw), lambda i: (0, 0)),
            pl.BlockSpec((tm, d), lambda i: (i, 0)),
            pl.BlockSpec((None, 1, d), lambda i: (l * 6 + 3, 0, 0)),
            pl.BlockSpec((None, fw + gw + sw, d), lambda i: (l, 0, 0)),
        ],
        out_specs=pl.BlockSpec((tm, d), lambda i: (i, 0)),
        out_shape=jax.ShapeDtypeStruct((m, d), F32),
        compiler_params=_cparams(("parallel",)),
        name=f"outproj_{l}",
    )(fox_p, fox_s, gm_o, ssd_p, ssd_s, x, norm_w3, w_out)


def _head_rows(vals, bsz, seq):
    return jnp.transpose(vals.reshape(bsz, seq, vals.shape[-1]), (0, 2, 1))


def _pad_lanes(rows, width):
    return jnp.pad(rows, ((0, 0), (0, 0), (0, width - rows.shape[-1])))


def kernel(x_prompt, x_sample, cache_fox_k, cache_fox_v, cache_fox_logf, state_ssd_conv, state_ssd, norm_w,
           w_ffn_gate, w_ffn_up, w_ffn_down, w_in, fox_fb, gm_ln_w, gm_ln_b, gm_ws, gm_bs, ssd_conv_w,
           ssd_conv_b, ssd_dt_bias, ssd_a_log, ssd_d, ssd_norm_w, w_out):
    bsz, seq, d = x_prompt.shape
    dbsz, dseq, _ = x_sample.shape
    depth, _, past, heads, dh = cache_fox_k.shape
    fox_w = heads * dh
    groups, gchunk = gm_ws.shape[1], gm_ws.shape[2]
    gm_w = gm_ln_w.shape[1]
    gch = gm_w // groups
    _, _, sheads, hp, sn = state_ssd.shape
    ssd_w = sheads * hp
    cch = ssd_conv_w.shape[2]
    ngroups = (cch - ssd_w) // (2 * sn)
    conv_k = ssd_conv_w.shape[1]
    n_p, n_s = bsz * seq, dbsz * dseq
    m = n_p + n_s
    assert gchunk == GM_CHUNK and heads == 8 and sheads == 8 and dh == LANES
    assert seq % GM_CHUNK == 0 and GM_CHUNK % dseq == 0 and dseq < SSD_CHUNK and past % LANES == 0

    wg, wu, wd = w_ffn_gate, w_ffn_up.astype(BF16), w_ffn_down
    wo = w_out.astype(BF16)
    sizes = [fox_w, fox_w, fox_w, heads, gm_w, gm_w, ssd_w, cch, sheads]
    offs = [0]
    for s in sizes:
        offs.append(offs[-1] + s)
    w_t = jnp.transpose(w_in, (0, 2, 1))
    wq, wk, wv, wf, wgu, wgv, wz, wxbc, wdt = (w_t[:, offs[i]:offs[i + 1], :] for i in range(9))
    tn = 1024
    used = 3 * fox_w + 2 * gm_w + cch + ssd_w + MISC_W
    npad = -(-used // tn) * tn
    zeros_misc = jnp.zeros((depth, MISC_W - heads - 2 * sheads, d), F32)
    zeros_tail = jnp.zeros((depth, npad - used, d), F32)
    w_r = jnp.concatenate([wq, wk, wv, wgu, wgv, wxbc, wz, wf, wdt, wdt, zeros_misc, zeros_tail],
                          axis=1).astype(BF16)
    col_q, col_k, col_v = 0, fox_w // dh, 2 * fox_w // dh
    col_u, col_gv = 3 * fox_w // gm_w, 3 * fox_w // gm_w + 1
    off_xbc = 3 * fox_w + 2 * gm_w
    off_z = off_xbc + cch
    off_misc = off_z + ssd_w
    assert off_xbc % cch == 0 and off_z % ssd_w == 0 and off_misc % MISC_W == 0
    misc_in_tile = off_misc % tn
    assert off_misc // tn == npad // tn - 1 and misc_in_tile + MISC_W <= tn

    lane_pad = jnp.zeros((depth, MISC_W - heads - 2 * sheads), F32)
    bias_vec = jnp.concatenate([fox_fb, ssd_dt_bias, ssd_dt_bias, lane_pad], axis=-1)[:, None, :]
    alog_vec = jnp.concatenate([jnp.zeros((depth, heads + sheads), F32), ssd_a_log, lane_pad], axis=-1)[:, None, :]
    norm_w3 = norm_w.reshape(depth * 6, 1, d)
    lnw3, lnb3 = gm_ln_w[:, None, :], gm_ln_b[:, None, :]
    reps = GM_CHUNK // dseq
    eye = jnp.eye(reps, dtype=F32)
    ws_s = jnp.einsum('ab,lgts->lgatbs', eye, gm_ws[:, :, :dseq, :dseq]).reshape(depth, groups, GM_CHUNK, GM_CHUNK)
    bs_p = jnp.transpose(gm_bs, (0, 2, 1))
    bs_s = jnp.tile(jnp.transpose(gm_bs[:, :, :dseq], (0, 2, 1)), (1, reps, 1))
    conv_b3 = ssd_conv_b[:, None, :]
    dvec = jnp.repeat(ssd_d, hp, axis=-1)[:, None, :]
    snw3 = ssd_norm_w[:, None, :]
    cache_k = cache_fox_k.reshape(depth, dbsz, past * heads, dh)
    cache_v = cache_fox_v.reshape(depth, dbsz, past * heads, dh)
    conv_prev_p = jnp.zeros((bsz, 8, cch), F32)
    conv_prev_s = jnp.pad(state_ssd_conv, ((0, 0), (0, 0), (8 - (conv_k - 1), 0), (0, 0)))
    s0_p = jnp.zeros((bsz, sheads, hp, sn), F32)

    x = (x_prompt.reshape(n_p, d), x_sample.reshape(n_s, d))
    tm_big = m // 8 if m % 64 == 0 and (m // 8) % 16 == 0 else 256
    tm_ffn = m // 12 if m % 12 == 0 and (m // 12) % 16 == 0 else 256
    tm_out = m // 16 if m % 16 == 0 and (m // 16) % 16 == 0 else 256
    lf_len = 2 * past
    assert past + dseq <= lf_len

    outs = {k: [] for k in ("plf", "pconv", "pst", "slf", "sgm", "sconv", "sst")}
    kv_p = kv_s = None
    for l in range(depth):
        x = _ffn(x, norm_w3, wg, wu, wd, l, 0, tm_ffn, 512)
        proj = _inproj(x, norm_w3, w_r, bias_vec, alog_vec, l, tm_big, tn, misc_in_tile)
        misc = proj[:, off_misc:off_misc + heads + 2 * sheads]
        logf_p, logf_s = misc[:n_p, :heads], misc[n_p:, :heads]
        lf_rows_p = _head_rows(logf_p, bsz, seq).reshape(bsz, heads, seq // LANES, LANES)
        lf_all = jnp.concatenate([cache_fox_logf[l], logf_s.reshape(dbsz, dseq, heads),
                                  jnp.zeros((dbsz, lf_len - past - dseq, heads), F32)], axis=1)
        lf_rows_s = jnp.transpose(lf_all, (0, 2, 1)).reshape(dbsz, heads, lf_len // LANES, LANES)
        dt_rows_p = _head_rows(misc[:n_p, heads:heads + sheads], bsz, seq)
        a_rows_p = _head_rows(misc[:n_p, heads + sheads:], bsz, seq)
        dt_rows_s = _pad_lanes(_head_rows(misc[n_p:, heads:heads + sheads], dbsz, dseq), LANES)
        a_rows_s = _pad_lanes(_head_rows(misc[n_p:, heads + sheads:], dbsz, dseq), LANES)

        fox_p, *kv_p = _fox_prompt(proj, lf_rows_p, kv_p, l, depth, bsz, seq, heads, dh, col_q, col_k, col_v, 256, 2)
        fox_s, *kv_s = _fox_sample(proj, lf_rows_s, cache_k, cache_v, kv_s, l, n_p, dbsz, dseq, heads, dh, 0, 1, 2)
        gm_o, gm_v_s = _gmlp(proj, lnw3, lnb3, gm_ws, ws_s, bs_p, bs_s, l, n_p, groups, gch, col_u, col_gv, 256)
        ssd_p, conv_p, st_p = _ssd(proj, dt_rows_p, a_rows_p, conv_prev_p, s0_p, ssd_conv_w, conv_b3, dvec, snw3,
                                   l, 0, bsz, seq, 256, SSD_CHUNK, sheads, hp, ngroups, sn,
                                   off_xbc // cch, off_z // ssd_w, off_misc // MISC_W)
        ssd_s, conv_s, st_s = _ssd(proj, dt_rows_s, a_rows_s, conv_prev_s[l], state_ssd[l], ssd_conv_w, conv_b3,
                                   dvec, snw3, l, n_p, dbsz, dseq, dseq, dseq, sheads, hp, ngroups, sn,
                                   off_xbc // cch, off_z // ssd_w, off_misc // MISC_W)
        x = _outproj(fox_p, fox_s, gm_o, ssd_p, ssd_s, x, norm_w3, wo, l, tm_out)
        x = _ffn(x, norm_w3, wg, wu, wd, l, 1, tm_ffn, 512,
                 split_out_rows=(n_p, n_s) if l == depth - 1 else None)

        outs["plf"].append(logf_p.reshape(bsz, seq, heads))
        outs["pconv"].append(conv_p)
        outs["pst"].append(st_p)
        outs["slf"].append(logf_s.reshape(dbsz, dseq, heads))
        outs["sgm"].append(gm_v_s.reshape(dbsz, dseq, gm_w))
        outs["sconv"].append(conv_s)
        outs["sst"].append(st_s)

    st = {k: jnp.stack(v) for k, v in outs.items()}
    y_prompt = x[0].reshape(bsz, seq, d)
    y_sample = x[1].reshape(dbsz, dseq, d)
    p_k, p_v = (a.reshape(depth, bsz, seq, heads, dh) for a in kv_p)
    s_k, s_v = (a.reshape(depth, dbsz, dseq, heads, dh) for a in kv_s)
    return (y_prompt, y_sample, p_k, p_v, st["plf"], st["pconv"], st["pst"],
            s_k, s_v, st["slf"], st["sgm"], st["sconv"], st["sst"])
```

```python
import functools

import jax
import jax.numpy as jnp
from jax import lax
from jax.experimental import pallas as pl
from jax.experimental.pallas import tpu as pltpu

F32 = jnp.float32
BF16 = jnp.bfloat16
EPS = 1e-6
LANES = 128
SSD_CHUNK = 64
GM_CHUNK = 128
VMEM_LIMIT = 56 * 1024 * 1024
FFN_VMEM_LIMIT = 60 * 1024 * 1024

MISC_W = LANES


def _cparams(sem, vmem_limit=VMEM_LIMIT):
    return pltpu.CompilerParams(dimension_semantics=sem, vmem_limit_bytes=vmem_limit)


def _rmsnorm(x, w):
    return x * lax.rsqrt(jnp.mean(x * x, axis=-1, keepdims=True) + EPS) * w


def _dot(a, b):
    return jnp.dot(a, b, preferred_element_type=F32)


def _dot_nt(a, b):
    return lax.dot_general(a, b, (((1,), (1,)), ((), ())), preferred_element_type=F32)


def _dot_tn(a, b):
    return lax.dot_general(a, b, (((0,), (0,)), ((), ())), preferred_element_type=F32)


def _row_to_col(row):
    n = row.shape[1]
    eye = lax.broadcasted_iota(jnp.int32, (n, n), 0) == lax.broadcasted_iota(jnp.int32, (n, n), 1)
    return jnp.sum(jnp.where(eye, jnp.broadcast_to(row, (n, n)), 0.0), axis=1, keepdims=True)


def _lane_cumsum(x, seg):
    lane = lax.broadcasted_iota(jnp.int32, x.shape, 1) & (seg - 1)
    k = 1
    while k < seg:
        x = x + jnp.where(lane >= k, pltpu.roll(x, k, axis=1), 0.0)
        k *= 2
    return x


def _flat_cumsum(x):
    x = _lane_cumsum(x, LANES)
    tot = x[:, LANES - 1:LANES]
    rows = []
    carry = jnp.zeros((1, 1), F32)
    for r in range(x.shape[0]):
        rows.append(x[r:r + 1, :] + carry)
        carry = carry + tot[r:r + 1, :]
    return jnp.concatenate(rows, axis=0)


def _ffn_body(*refs, split_in, split_out, rows_a):
    refs = list(refs)
    x_ref = refs.pop(0)
    xs_ref = refs.pop(0) if split_in else None
    nwa_ref, nwb_ref, wg_ref, wu_ref, wd_ref, o_ref = refs[:6]
    os_ref = refs[6] if split_out else None
    h_ref = refs[-1]
    i, f = pl.program_id(0), pl.program_id(1)
    last_i, last_f = pl.num_programs(0) - 1, pl.num_programs(1) - 1

    tm = o_ref.shape[0]
    whole = [(slice(0, tm), x_ref, slice(0, tm))]
    pieces_last = [(slice(0, rows_a), x_ref, slice(0, rows_a)), (slice(rows_a, tm), xs_ref, slice(None))]

    def per_piece(fn):
        if not split_in:
            fn(whole)
            return
        pl.when(i < last_i)(lambda: fn(whole))
        pl.when(i == last_i)(lambda: fn(pieces_last))

    def start(pieces):
        for rows, ref, src in pieces:
            h_ref[rows, :] = _rmsnorm(ref[src, :], nwa_ref[...]).astype(BF16)
        o_ref[...] = jnp.zeros_like(o_ref)

    pl.when(f == 0)(lambda: per_piece(start))

    h = h_ref[...]
    a = _dot(h, wg_ref[...].astype(BF16))
    g = _dot(h, wu_ref[...])
    act = (a * jax.nn.sigmoid(a) * g).astype(BF16)
    o_ref[...] += _dot(act, wd_ref[...].astype(BF16))

    def finish(pieces):
        for rows, ref, src in pieces:
            o_ref[rows, :] = ref[src, :] + _rmsnorm(o_ref[rows, :], 0.5 * nwb_ref[...])
        if split_out:
            @pl.when(i == last_i)
            def _():
                os_ref[...] = o_ref[rows_a:, :]

    pl.when(f == last_f)(lambda: per_piece(finish))


def _ffn(x, norm_w3, wg, wu, wd, l, j, tm, tf, split_out_rows=None):
    split_in = isinstance(x, tuple)
    split_out = split_out_rows is not None
    xs = list(x) if split_in else [x]
    n_a, n_b = (xs[0].shape[0], xs[1].shape[0]) if split_in else (split_out_rows or (x.shape[0], 0))
    m, d = n_a + n_b, xs[0].shape[1]
    dff = wg.shape[-1]
    rows_a = n_a % tm
    assert m % tm == 0 and (not (split_in or split_out) or (rows_a + n_b == tm and rows_a % 8 == 0))
    row_spec = pl.BlockSpec((tm, d), lambda i, f: (i, 0))
    tail_spec = pl.BlockSpec((n_b, d), lambda i, f: (0, 0), pipeline_mode=pl.Buffered(1))
    out_shape = jax.ShapeDtypeStruct((m, d), F32)
    if split_out:
        out_shape = [jax.ShapeDtypeStruct((n_a, d), F32), jax.ShapeDtypeStruct((n_b, d), F32)]
    return pl.pallas_call(
        functools.partial(_ffn_body, split_in=split_in, split_out=split_out, rows_a=rows_a),
        grid=(m // tm, dff // tf),
        in_specs=[row_spec] + ([tail_spec] if split_in else []) + [
            pl.BlockSpec((None, 1, d), lambda i, f: (l * 6 + 3 * j + j, 0, 0)),
            pl.BlockSpec((None, 1, d), lambda i, f: (l * 6 + 3 * j + j + 1, 0, 0)),
            pl.BlockSpec((None, None, d, tf), lambda i, f: (l, j, 0, f)),
            pl.BlockSpec((None, None, d, tf), lambda i, f: (l, j, 0, f)),
            pl.BlockSpec((None, None, tf, d), lambda i, f: (l, j, f, 0)),
        ],
        out_specs=[row_spec, tail_spec] if split_out else row_spec,
        out_shape=out_shape,
        scratch_shapes=[pltpu.VMEM((tm, d), BF16)],
        compiler_params=_cparams(("parallel", "arbitrary"), FFN_VMEM_LIMIT),
        name=f"ffn_{l}_{j}",
    )(*xs, norm_w3, norm_w3, wg, wu, wd)


def _inproj_body(x_ref, nw_ref, w_ref, bias_ref, alog_ref, o_ref, h_ref, *, misc_off):
    j = pl.program_id(1)

    @pl.when(j == 0)
    def _():
        h_ref[...] = _rmsnorm(x_ref[...], nw_ref[...]).astype(BF16)

    acc = _dot_nt(h_ref[...], w_ref[...])
    o_ref[...] = acc

    @pl.when(j == pl.num_programs(1) - 1)
    def _():
        raw = acc[:, misc_off:misc_off + MISC_W] + bias_ref[...]
        lane = lax.broadcasted_iota(jnp.int32, raw.shape, 1)
        tail = jnp.log1p(jnp.exp(-jnp.abs(raw)))
        sp = jnp.maximum(raw, 0.0) + tail
        logf = jnp.minimum(raw, 0.0) - tail
        neg_a = -jnp.exp(alog_ref[...])
        val = jnp.where(lane < 8, logf, jnp.where(lane < 16, sp, jnp.where(lane < 24, sp * neg_a, 0.0)))
        o_ref[:, misc_off:misc_off + MISC_W] = val


def _inproj(x, norm_w3, w_r, bias_vec, alog_vec, l, tm, tn, misc_off):
    m, d = x.shape
    npad = w_r.shape[1]
    return pl.pallas_call(
        functools.partial(_inproj_body, misc_off=misc_off),
        grid=(m // tm, npad // tn),
        in_specs=[
            pl.BlockSpec((tm, d), lambda i, j: (i, 0)),
            pl.BlockSpec((None, 1, d), lambda i, j: (l * 6 + 2, 0, 0)),
            pl.BlockSpec((None, tn, d), lambda i, j: (l, j, 0)),
            pl.BlockSpec((None, 1, MISC_W), lambda i, j: (l, 0, 0)),
            pl.BlockSpec((None, 1, MISC_W), lambda i, j: (l, 0, 0)),
        ],
        out_specs=pl.BlockSpec((tm, tn), lambda i, j: (i, j)),
        out_shape=jax.ShapeDtypeStruct((m, npad), F32),
        scratch_shapes=[pltpu.VMEM((tm, d), BF16)],
        compiler_params=_cparams(("parallel", "arbitrary")),
        name=f"inproj_{l}",
    )(x, norm_w3, w_r, bias_vec, alog_vec)


def _rows_as_lane_vector(c_ref, start, count):
    rows = c_ref[pl.ds(start, count), :]
    return jnp.concatenate([rows[r:r + 1, :] for r in range(count)], axis=1)


def _store_layer_rows(out_ref, val, l, aliased):
    if aliased:
        out_ref[...] = val
    else:
        for slot in range(out_ref.shape[0]):
            out_ref[slot] = val if slot == l else jnp.zeros_like(val)


def _split3(x):
    hi = x.astype(BF16).astype(F32)
    r = x - hi
    mid = r.astype(BF16).astype(F32)
    return hi, mid, r - mid


def _fox_head(hh, lf_ref, q_ref, k_ref, v_ref, *rest, tq, dh, scale, l, aliased):
    o_ref, ko_ref, vo_ref = rest[2 * aliased:2 * aliased + 3]
    c_scr, qx_scr, kx_scr, vb_scr, s_scr, p_scr = (scr.at[hh] for scr in rest[2 * aliased + 3:])
    seq = q_ref.shape[0]
    cols = slice(hh * dh, (hh + 1) * dh)
    nq = seq // tq
    sub = tq // LANES
    log2e = 1.4426950408889634
    c_scr[...] = _flat_cumsum(lf_ref[hh]) * log2e
    lane = lax.broadcasted_iota(jnp.int32, (tq, LANES), 1)

    def prepare(i):
        rows = slice(i * tq, (i + 1) * tq)
        hi, mid, lo = _split3(_row_to_col(_rows_as_lane_vector(c_scr, i * sub, sub)))
        q_ext = jnp.where(lane == 0, hi, jnp.where(lane == 1, mid, jnp.where(lane == 2, lo,
                          jnp.where(lane < 6, 1.0, 0.0))))
        k_ext = jnp.where(lane < 3, 1.0, jnp.where(lane == 3, -hi, jnp.where(lane == 4, -mid,
                          jnp.where(lane == 5, -lo, 0.0))))
        k_rows, v_rows = k_ref[rows, cols], v_ref[rows, cols]
        qx_scr[rows, 0:dh] = (q_ref[rows, cols] * (scale * log2e)).astype(BF16)
        qx_scr[rows, dh:dh + LANES] = q_ext.astype(BF16)
        kx_scr[0:dh, rows] = k_rows.T.astype(BF16)
        kx_scr[dh:dh + LANES, rows] = k_ext.T.astype(BF16)
        vb_scr[rows, :] = v_rows.astype(BF16)
        if aliased:
            ko_ref[rows, cols] = k_rows
            vo_ref[rows, cols] = v_rows
        else:
            for slot in range(ko_ref.shape[0]):
                ko_ref[slot, rows, cols] = k_rows if slot == l else jnp.zeros_like(k_rows)
                vo_ref[slot, rows, cols] = v_rows if slot == l else jnp.zeros_like(v_rows)

    causal = lax.broadcasted_iota(jnp.int32, (tq, tq), 1) <= lax.broadcasted_iota(jnp.int32, (tq, tq), 0)
    slab = 64
    n_slabs = tq // slab

    def scores(i, j):
        s = _dot(qx_scr[i * tq:(i + 1) * tq, :], kx_scr[:, j * tq:(j + 1) * tq])
        if j == i:
            s = jnp.where(causal, s, -jnp.inf)
        s_scr[i % 2, :, j * tq:(j + 1) * tq] = s

    def softmax_slab(i, r):
        nk = (i + 1) * tq
        rs = slice(r * slab, (r + 1) * slab)
        m_acc = s_scr[i % 2, rs, 0:LANES]
        for u in range(1, nk // LANES):
            m_acc = jnp.maximum(m_acc, s_scr[i % 2, rs, u * LANES:(u + 1) * LANES])
        m = jnp.broadcast_to(jnp.max(m_acc, axis=1, keepdims=True), (slab, LANES))
        l_acc = jnp.zeros((slab, LANES), F32)
        for u in range(nk // LANES):
            p = jnp.exp2(s_scr[i % 2, rs, u * LANES:(u + 1) * LANES] - m)
            l_acc = l_acc + p
            p_scr[i % 2, rs, u * LANES:(u + 1) * LANES] = p.astype(BF16)
        return jnp.sum(l_acc, axis=1, keepdims=True)

    def weighted_values(i, denom):
        nk = (i + 1) * tq
        acc = _dot(p_scr[i % 2, :, 0:nk], vb_scr[0:nk, :])
        o_ref[i * tq:(i + 1) * tq, cols] = (acc / jnp.concatenate(denom, axis=0)).astype(o_ref.dtype)

    prepare(0)
    scores(0, 0)
    denom_prev = None
    for i in range(nq):
        next_keys = list(range(i + 2)) if i + 1 < nq else []
        if next_keys:
            prepare(i + 1)
        denom = []
        for r in range(n_slabs):
            for j in next_keys[r::n_slabs]:
                scores(i + 1, j)
            if r == 1 and denom_prev is not None:
                weighted_values(i - 1, denom_prev)
            denom.append(softmax_slab(i, r))
        denom_prev = denom
    weighted_values(nq - 1, denom_prev)


def _fox_prompt_body(*refs, dh, **kw):
    for hh in range(refs[1].shape[1] // dh):
        _fox_head(hh, *refs, dh=dh, **kw)


def _fox_prompt(proj, lf_rows, kv_prev, l, depth, bsz, seq, heads, dh, col_q, col_k, col_v, tq, hps):
    n = seq // LANES
    aliased = kv_prev is not None
    kv_shape = jax.ShapeDtypeStruct((depth, bsz * seq, heads * dh), F32)
    gw = hps * dh
    assert heads % hps == 0 and col_q % hps == 0 and col_k % hps == 0 and col_v % hps == 0
    kv_spec = (pl.BlockSpec((None, seq, gw), lambda b, h: (l, b, h)) if aliased else
               pl.BlockSpec((depth, seq, gw), lambda b, h: (0, b, h)))
    in_specs = [
        pl.BlockSpec((None, hps, n, LANES), lambda b, h: (b, h, 0, 0)),
        pl.BlockSpec((seq, gw), lambda b, h: (b, col_q // hps + h)),
        pl.BlockSpec((seq, gw), lambda b, h: (b, col_k // hps + h)),
        pl.BlockSpec((seq, gw), lambda b, h: (b, col_v // hps + h)),
    ]
    args = [lf_rows, proj, proj, proj]
    if aliased:
        in_specs += [pl.BlockSpec(memory_space=pl.ANY)] * 2
        args += list(kv_prev)
    return pl.pallas_call(
        functools.partial(_fox_prompt_body, tq=tq, dh=dh, scale=dh ** -0.5, l=l, aliased=aliased),
        grid=(bsz, heads // hps),
        in_specs=in_specs,
        out_specs=[pl.BlockSpec((seq, gw), lambda b, h: (b, h)), kv_spec, kv_spec],
        out_shape=[jax.ShapeDtypeStruct((bsz * seq, heads * dh), BF16), kv_shape, kv_shape],
        scratch_shapes=[pltpu.VMEM((hps, n, LANES), F32), pltpu.VMEM((hps, seq, dh + LANES), BF16),
                        pltpu.VMEM((hps, dh + LANES, seq), BF16), pltpu.VMEM((hps, seq, dh), BF16),
                        pltpu.VMEM((hps, 2, tq, seq), F32), pltpu.VMEM((hps, 2, tq, seq), BF16)],
        input_output_aliases={4: 1, 5: 2} if aliased else {},
        compiler_params=_cparams(("parallel", "parallel")),
        name="fox_prompt",
    )(*args)


def _fox_sample_body(lf_ref, q_ref, k_ref, v_ref, ck_ref, cv_ref, *rest, past, new, heads, dh, scale, l,
                     aliased):
    o_ref, ko_ref, vo_ref = rest[2 * aliased:]
    _store_layer_rows(ko_ref, k_ref[...], l, aliased)
    _store_layer_rows(vo_ref, v_ref[...], l, aliased)
    n_past = past // LANES
    causal = lax.broadcasted_iota(jnp.int32, (new, new), 1) <= lax.broadcasted_iota(jnp.int32, (new, new), 0)
    for h in range(heads):
        cols = slice(h * dh, (h + 1) * dh)
        c = _flat_cumsum(lf_ref[h])
        c_past = jnp.concatenate([c[r:r + 1, :] for r in range(n_past)], axis=1)
        c_new = c[n_past:n_past + 1, 0:new]
        cq_col = _row_to_col(c_new)
        q = q_ref[:, cols].astype(BF16)
        head_rows = pl.ds(h, past, stride=heads)
        s_past = _dot_nt(q, ck_ref[head_rows, :].astype(BF16)) * scale + cq_col - c_past
        s_new = _dot_nt(q, k_ref[:, cols].astype(BF16)) * scale + cq_col - c_new
        s_new = jnp.where(causal, s_new, -jnp.inf)
        m = jnp.maximum(jnp.max(s_past, axis=1, keepdims=True), jnp.max(s_new, axis=1, keepdims=True))
        p_past = jnp.exp(s_past - m)
        p_new = jnp.exp(s_new - m)
        denom = jnp.sum(p_past, axis=1, keepdims=True) + jnp.sum(p_new, axis=1, keepdims=True)
        acc = _dot(p_past.astype(BF16), cv_ref[head_rows, :].astype(BF16)) + _dot(
            p_new.astype(BF16), v_ref[:, cols].astype(BF16))
        o_ref[:, cols] = (acc / denom).astype(o_ref.dtype)


def _fox_sample(proj, lf_rows, cache_k, cache_v, kv_prev, l, row0, bsz, new, heads, dh, col_q, col_k, col_v):
    depth = cache_k.shape[0]
    past = cache_k.shape[2] // heads
    fw = heads * dh
    n = lf_rows.shape[2]
    rb0 = row0 // new
    aliased = kv_prev is not None
    kv_shape = jax.ShapeDtypeStruct((depth, bsz * new, fw), F32)
    kv_spec = (pl.BlockSpec((None, new, fw), lambda b: (l, b, 0)) if aliased else
               pl.BlockSpec((depth, new, fw), lambda b: (0, b, 0)))
    in_specs = [
        pl.BlockSpec((None, heads, n, LANES), lambda b: (b, 0, 0, 0)),
        pl.BlockSpec((new, fw), lambda b: (rb0 + b, col_q)),
        pl.BlockSpec((new, fw), lambda b: (rb0 + b, col_k)),
        pl.BlockSpec((new, fw), lambda b: (rb0 + b, col_v)),
        pl.BlockSpec((None, None, past * heads, dh), lambda b: (l, b, 0, 0)),
        pl.BlockSpec((None, None, past * heads, dh), lambda b: (l, b, 0, 0)),
    ]
    args = [lf_rows, proj, proj, proj, cache_k, cache_v]
    if aliased:
        in_specs += [pl.BlockSpec(memory_space=pl.ANY)] * 2
        args += list(kv_prev)
    return pl.pallas_call(
        functools.partial(_fox_sample_body, past=past, new=new, heads=heads, dh=dh, scale=dh ** -0.5,
                          l=l, aliased=aliased),
        grid=(bsz,),
        in_specs=in_specs,
        out_specs=[pl.BlockSpec((new, fw), lambda b: (b, 0)), kv_spec, kv_spec],
        out_shape=[jax.ShapeDtypeStruct((bsz * new, fw), BF16), kv_shape, kv_shape],
        input_output_aliases={6: 1, 7: 2} if aliased else {},
        compiler_params=_cparams(("parallel",)),
        name="fox_sample",
    )(*args)


def _gmlp_body(gu_ref, gv_ref, lnw_ref, lnb_ref, wp_ref, ws_ref, bp_ref, bs_ref, o_ref, v_ref, *,
               n_prompt_tiles, groups, gch):
    i = pl.program_id(0)
    is_sample = i >= n_prompt_tiles
    u = jax.nn.gelu(gu_ref[...])
    gv = jax.nn.gelu(gv_ref[...])
    mu = jnp.mean(gv, axis=-1, keepdims=True)
    var = jnp.mean(jnp.square(gv - mu), axis=-1, keepdims=True)
    v = (gv - mu) * lax.rsqrt(var + EPS) * lnw_ref[...] + lnb_ref[...]

    @pl.when(is_sample)
    def _():
        v_ref[...] = v

    tm = u.shape[0]
    tril = lax.broadcasted_iota(jnp.int32, (GM_CHUNK, GM_CHUNK), 1) <= lax.broadcasted_iota(
        jnp.int32, (GM_CHUNK, GM_CHUNK), 0)
    bias_all = jnp.where(is_sample, bs_ref[...], bp_ref[...])
    for g in range(groups):
        wm = jnp.where(is_sample, ws_ref[g], wp_ref[g])
        wm = jnp.where(tril, wm, 0.0).astype(BF16)
        bias = bias_all[:, g:g + 1]
        for c in range(tm // GM_CHUNK):
            rows = slice(c * GM_CHUNK, (c + 1) * GM_CHUNK)
            cols = slice(g * gch, (g + 1) * gch)
            sp = _dot(wm, v[rows, cols].astype(BF16)) + bias
            o_ref[rows, cols] = (u[rows, cols] * sp).astype(o_ref.dtype)


def _gmlp(proj, lnw, lnb, w_prompt, w_sample, b_prompt, b_sample, l, n_prompt_rows, groups, gch,
          col_u, col_v, tm):
    m = proj.shape[0]
    gw = groups * gch
    npt = n_prompt_rows // tm
    return pl.pallas_call(
        functools.partial(_gmlp_body, n_prompt_tiles=npt, groups=groups, gch=gch),
        grid=(m // tm,),
        in_specs=[
            pl.BlockSpec((tm, gw), lambda i: (i, col_u)),
            pl.BlockSpec((tm, gw), lambda i: (i, col_v)),
            pl.BlockSpec((None, 1, gw), lambda i: (l, 0, 0)),
            pl.BlockSpec((None, 1, gw), lambda i: (l, 0, 0)),
            pl.BlockSpec((None, groups, GM_CHUNK, GM_CHUNK), lambda i: (l, 0, 0, 0)),
            pl.BlockSpec((None, groups, GM_CHUNK, GM_CHUNK), lambda i: (l, 0, 0, 0)),
            pl.BlockSpec((None, GM_CHUNK, groups), lambda i: (l, 0, 0)),
            pl.BlockSpec((None, GM_CHUNK, groups), lambda i: (l, 0, 0)),
        ],
        out_specs=[
            pl.BlockSpec((tm, gw), lambda i: (i, 0)),
            pl.BlockSpec((tm, gw), lambda i: (jnp.maximum(i - npt, 0), 0)),
        ],
        out_shape=[jax.ShapeDtypeStruct((m, gw), BF16), jax.ShapeDtypeStruct((m - n_prompt_rows, gw), F32)],
        compiler_params=_cparams(("arbitrary",)),
        name="gmlp",
    )(proj, proj, lnw, lnb, w_prompt, w_sample, b_prompt, b_sample)


def _ssd_body(xbc_ref, z_ref, misc_ref, dtrow_ref, arow_ref, prev_ref, s0_ref, cw_ref, cb_ref, dvec_ref,
              nw_ref, o_ref, conv_ref, st_ref, xp_scr, s_scr, y_scr, *, lt, q, heads, hp, ngroups, sn, conv_k):
    t = pl.program_id(1)
    nt = pl.num_programs(1)
    xw = heads * hp
    pad = 8

    @pl.when(t == 0)
    def _():
        xp_scr[0:pad, :] = prev_ref[...]
        s_scr[...] = s0_ref[...]

    xbc = xbc_ref[...]
    xp_scr[pad:pad + lt, :] = xbc
    acc = cb_ref[...]
    for i in range(conv_k):
        off = pad - (conv_k - 1) + i
        acc = acc + xp_scr[off:off + lt, :] * cw_ref[i:i + 1, :]
    xp_scr[0:pad, :] = xbc[lt - pad:lt, :]
    act = acc * jax.nn.sigmoid(acc)

    @pl.when(t == nt - 1)
    def _():
        conv_ref[...] = xbc[lt - (conv_k - 1):lt, :]

    dt_rows = dtrow_ref[...]
    ltp = dt_rows.shape[1]
    a_rows = jnp.concatenate(
        [_lane_cumsum(arow_ref[:, j * LANES:(j + 1) * LANES], q) for j in range(ltp // LANES)], axis=1)
    row = lax.broadcasted_iota(jnp.int32, (lt, lt), 0)
    col = lax.broadcasted_iota(jnp.int32, (lt, lt), 1)
    tri_all = jnp.where((col <= row) & (col >= (row & -q)), 1.0, 0.0).astype(BF16)
    a_cols = sum(_dot(tri_all, part.astype(BF16)) for part in _split3(misc_ref[...]))
    x_all = act[:, 0:xw]
    if lt < LANES:
        x_all = jnp.concatenate([x_all, jnp.zeros((LANES - lt, xw), F32)], axis=0)
    x_t = x_all.T
    tril = lax.broadcasted_iota(jnp.int32, (q, q), 1) <= lax.broadcasted_iota(jnp.int32, (q, q), 0)
    rep = heads // ngroups
    for c in range(lt // q):
        r0 = c * q
        bmat = [act[r0:r0 + q, xw + g * sn: xw + (g + 1) * sn].astype(BF16) for g in range(ngroups)]
        cmat = [act[r0:r0 + q, xw + (ngroups + g) * sn: xw + (ngroups + g + 1) * sn].astype(BF16)
                for g in range(ngroups)]
        gmat = [_dot_nt(cmat[g], bmat[g]) for g in range(ngroups)]
        for h in range(heads):
            g = h // rep
            a_row = a_rows[h:h + 1, r0:r0 + q]
            dt_row = dt_rows[h:h + 1, r0:r0 + q]
            a_col = a_cols[r0:r0 + q, 16 + h:17 + h]
            a_last = a_row[:, q - 1:q]
            lmat = jnp.exp(jnp.where(tril, a_col - a_row, -jnp.inf))
            scores = gmat[g] * lmat * dt_row
            xh = act[r0:r0 + q, h * hp:(h + 1) * hp]
            y = _dot(scores.astype(BF16), xh.astype(BF16))
            decay_end = jnp.exp(a_last - a_row) * dt_row
            xd_t = (x_t[h * hp:(h + 1) * hp, r0:r0 + q] * decay_end).astype(BF16)
            chunk_state = _dot(xd_t, bmat[g])
            s_in = s_scr[h]
            y = y + _dot_nt(cmat[g], s_in.astype(BF16)) * jnp.exp(a_col)
            s_scr[h] = s_in * jnp.exp(a_last) + chunk_state
            y_scr[r0:r0 + q, h * hp:(h + 1) * hp] = y + dvec_ref[:, h * hp:(h + 1) * hp] * xh

    zz = z_ref[...]
    o_ref[...] = _rmsnorm(y_scr[...] * (zz * jax.nn.sigmoid(zz)), nw_ref[...]).astype(o_ref.dtype)

    @pl.when(t == nt - 1)
    def _():
        st_ref[...] = s_scr[...]


def _ssd(proj, dt_rows, a_rows, conv_prev, s0, conv_w, conv_b, dvec, norm_w, l, row0, bsz, seq, lt, q,
         heads, hp, ngroups, sn, col_xbc, col_z, col_misc):
    nt = seq // lt
    xw = heads * hp
    cch = xw + 2 * ngroups * sn
    conv_k = conv_w.shape[1]
    rb0 = row0 // lt
    ltp = max(lt, LANES)
    body = functools.partial(_ssd_body, lt=lt, q=q, heads=heads, hp=hp, ngroups=ngroups, sn=sn, conv_k=conv_k)
    return pl.pallas_call(
        body,
        grid=(bsz, nt),
        in_specs=[
            pl.BlockSpec((lt, cch), lambda b, t: (rb0 + b * nt + t, col_xbc)),
            pl.BlockSpec((lt, xw), lambda b, t: (rb0 + b * nt + t, col_z)),
            pl.BlockSpec((lt, MISC_W), lambda b, t: (rb0 + b * nt + t, col_misc)),
            pl.BlockSpec((None, heads, ltp), lambda b, t: (b, 0, t)),
            pl.BlockSpec((None, heads, ltp), lambda b, t: (b, 0, t)),
            pl.BlockSpec((None, 8, cch), lambda b, t: (b, 0, 0)),
            pl.BlockSpec((None, heads, hp, sn), lambda b, t: (b, 0, 0, 0)),
            pl.BlockSpec((None, conv_k, cch), lambda b, t: (l, 0, 0)),
            pl.BlockSpec((None, 1, cch), lambda b, t: (l, 0, 0)),
            pl.BlockSpec((None, 1, xw), lambda b, t: (l, 0, 0)),
            pl.BlockSpec((None, 1, xw), lambda b, t: (l, 0, 0)),
        ],
        out_specs=[
            pl.BlockSpec((lt, xw), lambda b, t: (b * nt + t, 0)),
            pl.BlockSpec((None, conv_k - 1, cch), lambda b, t: (b, 0, 0)),
            pl.BlockSpec((None, heads, hp, sn), lambda b, t: (b, 0, 0, 0)),
        ],
        out_shape=[
            jax.ShapeDtypeStruct((bsz * seq, xw), BF16),
            jax.ShapeDtypeStruct((bsz, conv_k - 1, cch), F32),
            jax.ShapeDtypeStruct((bsz, heads, hp, sn), F32),
        ],
        scratch_shapes=[pltpu.VMEM((lt + 8, cch), F32), pltpu.VMEM((heads, hp, sn), F32),
                        pltpu.VMEM((lt, xw), F32)],
        compiler_params=_cparams(("parallel", "arbitrary")),
        name="ssd",
    )(proj, proj, proj, dt_rows, a_rows, conv_prev, s0, conv_w, conv_b, dvec, norm_w)


def _outproj_body(foxp_ref, foxs_ref, gm_ref, ssdp_ref, ssds_ref, x_ref, nw_ref, w_ref, o_ref, *, fw, gw, rows_a):
    i = pl.program_id(0)
    last_i = pl.num_programs(0) - 1

    def run(fox, ssd):
        o = _dot(fox, w_ref[0:fw, :])
        o = o + _dot(gm_ref[...], w_ref[fw:fw + gw, :])
        o = o + _dot(ssd, w_ref[fw + gw:, :])
        o_ref[...] = x_ref[...] + _rmsnorm(o, nw_ref[...])

    pl.when(i < last_i)(lambda: run(foxp_ref[...], ssdp_ref[...]))
    pl.when(i == last_i)(lambda: run(jnp.concatenate([foxp_ref[0:rows_a, :], foxs_ref[...]], axis=0),
                                     jnp.concatenate([ssdp_ref[0:rows_a, :], ssds_ref[...]], axis=0)))


def _outproj(fox_p, fox_s, gm_o, ssd_p, ssd_s, x, norm_w3, w_out, l, tm):
    m, d = x.shape
    n_a, n_b = fox_p.shape[0], fox_s.shape[0]
    fw, gw, sw = fox_p.shape[1], gm_o.shape[1], ssd_p.shape[1]
    rows_a = n_a % tm
    assert m % tm == 0 and rows_a + n_b == tm and rows_a % 16 == 0
    return pl.pallas_call(
        functools.partial(_outproj_body, fw=fw, gw=gw, rows_a=rows_a),
        grid=(m // tm,),
        in_specs=[
            pl.BlockSpec((tm, fw), lambda i: (i, 0)),
            pl.BlockSpec((n_b, fw), lambda i: (0, 0)),
            pl.BlockSpec((tm, gw), lambda i: (i, 0)),
            pl.BlockSpec((tm, sw), lambda i: (i, 0)),
            pl.BlockSpec((n_b, sw), lambda i: (0, 0)),
            pl.BlockSpec((tm, d), lambda i: (i, 0)),
            pl.BlockSpec((None, 1, d), lambda i: (l * 6 + 3, 0, 0)),
            pl.BlockSpec((None, fw + gw + sw, d), lambda i: (l, 0, 0)),
        ],
        out_specs=pl.BlockSpec((tm, d), lambda i: (i, 0)),
        out_shape=jax.ShapeDtypeStruct((m, d), F32),
        compiler_params=_cparams(("parallel",)),
        name=f"outproj_{l}",
    )(fox_p, fox_s, gm_o, ssd_p, ssd_s, x, norm_w3, w_out)


def _head_rows(vals, bsz, seq):
    return jnp.transpose(vals.reshape(bsz, seq, vals.shape[-1]), (0, 2, 1))


def _pad_lanes(rows, width):
    return jnp.pad(rows, ((0, 0), (0, 0), (0, width - rows.shape[-1])))


def kernel(x_prompt, x_sample, cache_fox_k, cache_fox_v, cache_fox_logf, state_ssd_conv, state_ssd, norm_w,
           w_ffn_gate, w_ffn_up, w_ffn_down, w_in, fox_fb, gm_ln_w, gm_ln_b, gm_ws, gm_bs, ssd_conv_w,
           ssd_conv_b, ssd_dt_bias, ssd_a_log, ssd_d, ssd_norm_w, w_out):
    bsz, seq, d = x_prompt.shape
    dbsz, dseq, _ = x_sample.shape
    depth, _, past, heads, dh = cache_fox_k.shape
    fox_w = heads * dh
    groups, gchunk = gm_ws.shape[1], gm_ws.shape[2]
    gm_w = gm_ln_w.shape[1]
    gch = gm_w // groups
    _, _, sheads, hp, sn = state_ssd.shape
    ssd_w = sheads * hp
    cch = ssd_conv_w.shape[2]
    ngroups = (cch - ssd_w) // (2 * sn)
    conv_k = ssd_conv_w.shape[1]
    n_p, n_s = bsz * seq, dbsz * dseq
    m = n_p + n_s
    assert gchunk == GM_CHUNK and heads == 8 and sheads == 8 and dh == LANES
    assert seq % GM_CHUNK == 0 and GM_CHUNK % dseq == 0 and dseq < SSD_CHUNK and past % LANES == 0

    wg, wu, wd = w_ffn_gate, w_ffn_up.astype(BF16), w_ffn_down
    wo = w_out.astype(BF16)
    sizes = [fox_w, fox_w, fox_w, heads, gm_w, gm_w, ssd_w, cch, sheads]
    offs = [0]
    for s in sizes:
        offs.append(offs[-1] + s)
    w_t = jnp.transpose(w_in, (0, 2, 1))
    wq, wk, wv, wf, wgu, wgv, wz, wxbc, wdt = (w_t[:, offs[i]:offs[i + 1], :] for i in range(9))
    tn = 1024
    used = 3 * fox_w + 2 * gm_w + cch + ssd_w + MISC_W
    npad = -(-used // tn) * tn
    zeros_misc = jnp.zeros((depth, MISC_W - heads - 2 * sheads, d), F32)
    zeros_tail = jnp.zeros((depth, npad - used, d), F32)
    w_r = jnp.concatenate([wq, wk, wv, wgu, wgv, wxbc, wz, wf, wdt, wdt, zeros_misc, zeros_tail],
                          axis=1).astype(BF16)
    col_q, col_k, col_v = 0, fox_w // dh, 2 * fox_w // dh
    col_u, col_gv = 3 * fox_w // gm_w, 3 * fox_w // gm_w + 1
    off_xbc = 3 * fox_w + 2 * gm_w
    off_z = off_xbc + cch
    off_misc = off_z + ssd_w
    assert off_xbc % cch == 0 and off_z % ssd_w == 0 and off_misc % MISC_W == 0
    misc_in_tile = off_misc % tn
    assert off_misc // tn == npad // tn - 1 and misc_in_tile + MISC_W <= tn

    lane_pad = jnp.zeros((depth, MISC_W - heads - 2 * sheads), F32)
    bias_vec = jnp.concatenate([fox_fb, ssd_dt_bias, ssd_dt_bias, lane_pad], axis=-1)[:, None, :]
    alog_vec = jnp.concatenate([jnp.zeros((depth, heads + sheads), F32), ssd_a_log, lane_pad], axis=-1)[:, None, :]
    norm_w3 = norm_w.reshape(depth * 6, 1, d)
    lnw3, lnb3 = gm_ln_w[:, None, :], gm_ln_b[:, None, :]
    reps = GM_CHUNK // dseq
    eye = jnp.eye(reps, dtype=F32)
    ws_s = jnp.einsum('ab,lgts->lgatbs', eye, gm_ws[:, :, :dseq, :dseq]).reshape(depth, groups, GM_CHUNK, GM_CHUNK)
    bs_p = jnp.transpose(gm_bs, (0, 2, 1))
    bs_s = jnp.tile(jnp.transpose(gm_bs[:, :, :dseq], (0, 2, 1)), (1, reps, 1))
    conv_b3 = ssd_conv_b[:, None, :]
    dvec = jnp.repeat(ssd_d, hp, axis=-1)[:, None, :]
    snw3 = ssd_norm_w[:, None, :]
    cache_k = cache_fox_k.reshape(depth, dbsz, past * heads, dh)
    cache_v = cache_fox_v.reshape(depth, dbsz, past * heads, dh)
    conv_prev_p = jnp.zeros((bsz, 8, cch), F32)
    conv_prev_s = jnp.pad(state_ssd_conv, ((0, 0), (0, 0), (8 - (conv_k - 1), 0), (0, 0)))
    s0_p = jnp.zeros((bsz, sheads, hp, sn), F32)

    x = (x_prompt.reshape(n_p, d), x_sample.reshape(n_s, d))
    tm_big = m // 8 if m % 64 == 0 and (m // 8) % 16 == 0 else 256
    tm_ffn = m // 12 if m % 12 == 0 and (m // 12) % 16 == 0 else 256
    tm_out = m // 16 if m % 16 == 0 and (m // 16) % 16 == 0 else 256
    lf_len = 2 * past
    assert past + dseq <= lf_len

    outs = {k: [] for k in ("plf", "pconv", "pst", "slf", "sgm", "sconv", "sst")}
    kv_p = kv_s = None
    for l in range(depth):
        x = _ffn(x, norm_w3, wg, wu, wd, l, 0, tm_ffn, 512)
        proj = _inproj(x, norm_w3, w_r, bias_vec, alog_vec, l, tm_big, tn, misc_in_tile)
        misc = proj[:, off_misc:off_misc + heads + 2 * sheads]
        logf_p, logf_s = misc[:n_p, :heads], misc[n_p:, :heads]
        lf_rows_p = _head_rows(logf_p, bsz, seq).reshape(bsz, heads, seq // LANES, LANES)
        lf_all = jnp.concatenate([cache_fox_logf[l], logf_s.reshape(dbsz, dseq, heads),
                                  jnp.zeros((dbsz, lf_len - past - dseq, heads), F32)], axis=1)
        lf_rows_s = jnp.transpose(lf_all, (0, 2, 1)).reshape(dbsz, heads, lf_len // LANES, LANES)
        dt_rows_p = _head_rows(misc[:n_p, heads:heads + sheads], bsz, seq)
        a_rows_p = _head_rows(misc[:n_p, heads + sheads:], bsz, seq)
        dt_rows_s = _pad_lanes(_head_rows(misc[n_p:, heads:heads + sheads], dbsz, dseq), LANES)
        a_rows_s = _pad_lanes(_head_rows(misc[n_p:, heads + sheads:], dbsz, dseq), LANES)

        fox_p, *kv_p = _fox_prompt(proj, lf_rows_p, kv_p, l, depth, bsz, seq, heads, dh, col_q, col_k, col_v, 256, 2)
        fox_s, *kv_s = _fox_sample(proj, lf_rows_s, cache_k, cache_v, kv_s, l, n_p, dbsz, dseq, heads, dh, 0, 1, 2)
        gm_o, gm_v_s = _gmlp(proj, lnw3, lnb3, gm_ws, ws_s, bs_p, bs_s, l, n_p, groups, gch, col_u, col_gv, 256)
        ssd_p, conv_p, st_p = _ssd(proj, dt_rows_p, a_rows_p, conv_prev_p, s0_p, ssd_conv_w, conv_b3, dvec, snw3,
                                   l, 0, bsz, seq, 256, SSD_CHUNK, sheads, hp, ngroups, sn,
                                   off_xbc // cch, off_z // ssd_w, off_misc // MISC_W)
        ssd_s, conv_s, st_s = _ssd(proj, dt_rows_s, a_rows_s, conv_prev_s[l], state_ssd[l], ssd_conv_w, conv_b3,
                                   dvec, snw3, l, n_p, dbsz, dseq, dseq, dseq, sheads, hp, ngroups, sn,
                                   off_xbc // cch, off_z // ssd_w, off_misc // MISC_W)
        x = _outproj(fox_p, fox_s, gm_o, ssd_p, ssd_s, x, norm_w3, wo, l, tm_out)
        x = _ffn(x, norm_w3, wg, wu, wd, l, 1, tm_ffn, 512,
                 split_out_rows=(n_p, n_s) if l == depth - 1 else None)

        outs["plf"].append(logf_p.reshape(bsz, seq, heads))
        outs["pconv"].append(conv_p)
        outs["pst"].append(st_p)
        outs["slf"].append(logf_s.reshape(dbsz, dseq, heads))
        outs["sgm"].append(gm_v_s.reshape(dbsz, dseq, gm_w))
        outs["sconv"].append(conv_s)
        outs["sst"].append(st_s)

    st = {k: jnp.stack(v) for k, v in outs.items()}
    y_prompt = x[0].reshape(bsz, seq, d)
    y_sample = x[1].reshape(dbsz, dseq, d)
    p_k, p_v = (a.reshape(depth, bsz, seq, heads, dh) for a in kv_p)
    s_k, s_v = (a.reshape(depth, dbsz, dseq, heads, dh) for a in kv_s)
    return (y_prompt, y_sample, p_k, p_v, st["plf"], st["pconv"], st["pst"],
            s_k, s_v, st["slf"], st["sgm"], st["sconv"], st["sst"])
```

```python
import functools

import jax
import jax.numpy as jnp
from jax import lax
from jax.experimental import pallas as pl
from jax.experimental.pallas import tpu as pltpu

F32 = jnp.float32
BF16 = jnp.bfloat16
EPS = 1e-6
LANES = 128
SSD_CHUNK = 64
GM_CHUNK = 128
VMEM_LIMIT = 56 * 1024 * 1024
FFN_VMEM_LIMIT = 60 * 1024 * 1024

MISC_W = LANES


def _cparams(sem, vmem_limit=VMEM_LIMIT):
    return pltpu.CompilerParams(dimension_semantics=sem, vmem_limit_bytes=vmem_limit)


def _rmsnorm(x, w):
    return x * lax.rsqrt(jnp.mean(x * x, axis=-1, keepdims=True) + EPS) * w


def _dot(a, b):
    return jnp.dot(a, b, preferred_element_type=F32)


def _dot_nt(a, b):
    return lax.dot_general(a, b, (((1,), (1,)), ((), ())), preferred_element_type=F32)


def _dot_tn(a, b):
    return lax.dot_general(a, b, (((0,), (0,)), ((), ())), preferred_element_type=F32)


def _row_to_col(row):
    n = row.shape[1]
    eye = lax.broadcasted_iota(jnp.int32, (n, n), 0) == lax.broadcasted_iota(jnp.int32, (n, n), 1)
    return jnp.sum(jnp.where(eye, jnp.broadcast_to(row, (n, n)), 0.0), axis=1, keepdims=True)


def _lane_cumsum(x, seg):
    lane = lax.broadcasted_iota(jnp.int32, x.shape, 1) & (seg - 1)
    k = 1
    while k < seg:
        x = x + jnp.where(lane >= k, pltpu.roll(x, k, axis=1), 0.0)
        k *= 2
    return x


def _flat_cumsum(x):
    x = _lane_cumsum(x, LANES)
    tot = x[:, LANES - 1:LANES]
    rows = []
    carry = jnp.zeros((1, 1), F32)
    for r in range(x.shape[0]):
        rows.append(x[r:r + 1, :] + carry)
        carry = carry + tot[r:r + 1, :]
    return jnp.concatenate(rows, axis=0)


def _ffn_body(*refs, split_in, split_out, rows_a):
    refs = list(refs)
    x_ref = refs.pop(0)
    xs_ref = refs.pop(0) if split_in else None
    nwa_ref, nwb_ref, wg_ref, wu_ref, wd_ref, o_ref = refs[:6]
    os_ref = refs[6] if split_out else None
    h_ref = refs[-1]
    i, f = pl.program_id(0), pl.program_id(1)
    last_i, last_f = pl.num_programs(0) - 1, pl.num_programs(1) - 1

    tm = o_ref.shape[0]
    whole = [(slice(0, tm), x_ref, slice(0, tm))]
    pieces_last = [(slice(0, rows_a), x_ref, slice(0, rows_a)), (slice(rows_a, tm), xs_ref, slice(None))]

    def per_piece(fn):
        if not split_in:
            fn(whole)
            return
        pl.when(i < last_i)(lambda: fn(whole))
        pl.when(i == last_i)(lambda: fn(pieces_last))

    def start(pieces):
        for rows, ref, src in pieces:
            h_ref[rows, :] = _rmsnorm(ref[src, :], nwa_ref[...]).astype(BF16)
        o_ref[...] = jnp.zeros_like(o_ref)

    pl.when(f == 0)(lambda: per_piece(start))

    h = h_ref[...]
    a = _dot(h, wg_ref[...].astype(BF16))
    g = _dot(h, wu_ref[...].astype(BF16))
    act = (a * jax.nn.sigmoid(a) * g).astype(BF16)
    o_ref[...] += _dot(act, wd_ref[...].astype(BF16))

    def finish(pieces):
        for rows, ref, src in pieces:
            o_ref[rows, :] = ref[src, :] + _rmsnorm(o_ref[rows, :], 0.5 * nwb_ref[...])
        if split_out:
            @pl.when(i == last_i)
            def _():
                os_ref[...] = o_ref[rows_a:, :]

    pl.when(f == last_f)(lambda: per_piece(finish))


def _ffn(x, norm_w3, wg, wu, wd, l, j, tm, tf, split_out_rows=None):
    split_in = isinstance(x, tuple)
    split_out = split_out_rows is not None
    xs = list(x) if split_in else [x]
    n_a, n_b = (xs[0].shape[0], xs[1].shape[0]) if split_in else (split_out_rows or (x.shape[0], 0))
    m, d = n_a + n_b, xs[0].shape[1]
    dff = wg.shape[-1]
    rows_a = n_a % tm
    assert m % tm == 0 and (not (split_in or split_out) or (rows_a + n_b == tm and rows_a % 8 == 0))
    row_spec = pl.BlockSpec((tm, d), lambda i, f: (i, 0))
    tail_spec = pl.BlockSpec((n_b, d), lambda i, f: (0, 0), pipeline_mode=pl.Buffered(1))
    out_shape = jax.ShapeDtypeStruct((m, d), F32)
    if split_out:
        out_shape = [jax.ShapeDtypeStruct((n_a, d), F32), jax.ShapeDtypeStruct((n_b, d), F32)]
    return pl.pallas_call(
        functools.partial(_ffn_body, split_in=split_in, split_out=split_out, rows_a=rows_a),
        grid=(m // tm, dff // tf),
        in_specs=[row_spec] + ([tail_spec] if split_in else []) + [
            pl.BlockSpec((None, 1, d), lambda i, f: (l * 6 + 3 * j + j, 0, 0)),
            pl.BlockSpec((None, 1, d), lambda i, f: (l * 6 + 3 * j + j + 1, 0, 0)),
            pl.BlockSpec((None, None, d, tf), lambda i, f: (l, j, 0, f)),
            (pl.BlockSpec((d, tf), lambda i, f: (0, f)) if wu.ndim == 2 else
             pl.BlockSpec((None, None, d, tf), lambda i, f: (l, j, 0, f))),
            pl.BlockSpec((None, None, tf, d), lambda i, f: (l, j, f, 0)),
        ],
        out_specs=[row_spec, tail_spec] if split_out else row_spec,
        out_shape=out_shape,
        scratch_shapes=[pltpu.VMEM((tm, d), BF16)],
        compiler_params=_cparams(("parallel", "arbitrary"), FFN_VMEM_LIMIT),
        name=f"ffn_{l}_{j}",
    )(*xs, norm_w3, norm_w3, wg, wu, wd)


def _inproj_body(x_ref, nw_ref, w_ref, bias_ref, alog_ref, o_ref, h_ref, *, misc_off):
    j = pl.program_id(1)

    @pl.when(j == 0)
    def _():
        h_ref[...] = _rmsnorm(x_ref[...], nw_ref[...]).astype(BF16)

    acc = _dot_nt(h_ref[...], w_ref[...])
    o_ref[...] = acc

    @pl.when(j == pl.num_programs(1) - 1)
    def _():
        raw = acc[:, misc_off:misc_off + MISC_W] + bias_ref[...]
        lane = lax.broadcasted_iota(jnp.int32, raw.shape, 1)
        tail = jnp.log1p(jnp.exp(-jnp.abs(raw)))
        sp = jnp.maximum(raw, 0.0) + tail
        logf = jnp.minimum(raw, 0.0) - tail
        neg_a = -jnp.exp(alog_ref[...])
        val = jnp.where(lane < 8, logf, jnp.where(lane < 16, sp, jnp.where(lane < 24, sp * neg_a, 0.0)))
        o_ref[:, misc_off:misc_off + MISC_W] = val


def _inproj(x, norm_w3, w_r, bias_vec, alog_vec, l, tm, tn, misc_off):
    m, d = x.shape
    npad = w_r.shape[1]
    return pl.pallas_call(
        functools.partial(_inproj_body, misc_off=misc_off),
        grid=(m // tm, npad // tn),
        in_specs=[
            pl.BlockSpec((tm, d), lambda i, j: (i, 0)),
            pl.BlockSpec((None, 1, d), lambda i, j: (l * 6 + 2, 0, 0)),
            pl.BlockSpec((None, tn, d), lambda i, j: (l, j, 0)),
            pl.BlockSpec((None, 1, MISC_W), lambda i, j: (l, 0, 0)),
            pl.BlockSpec((None, 1, MISC_W), lambda i, j: (l, 0, 0)),
        ],
        out_specs=pl.BlockSpec((tm, tn), lambda i, j: (i, j)),
        out_shape=jax.ShapeDtypeStruct((m, npad), F32),
        scratch_shapes=[pltpu.VMEM((tm, d), BF16)],
        compiler_params=_cparams(("parallel", "arbitrary")),
        name=f"inproj_{l}",
    )(x, norm_w3, w_r, bias_vec, alog_vec)


def _rows_as_lane_vector(c_ref, start, count):
    rows = c_ref[pl.ds(start, count), :]
    return jnp.concatenate([rows[r:r + 1, :] for r in range(count)], axis=1)


def _store_layer_rows(out_ref, val, l, aliased):
    if aliased:
        out_ref[...] = val
    else:
        for slot in range(out_ref.shape[0]):
            out_ref[slot] = val if slot == l else jnp.zeros_like(val)


def _split3(x):
    hi = x.astype(BF16).astype(F32)
    r = x - hi
    mid = r.astype(BF16).astype(F32)
    return hi, mid, r - mid


def _fox_head(hh, lf_ref, q_ref, k_ref, v_ref, *rest, tq, dh, scale, l, aliased):
    o_ref, ko_ref, vo_ref = rest[2 * aliased:2 * aliased + 3]
    c_scr, qx_scr, kx_scr, vb_scr, s_scr, p_scr = (scr.at[hh] for scr in rest[2 * aliased + 3:])
    seq = q_ref.shape[0]
    cols = slice(hh * dh, (hh + 1) * dh)
    nq = seq // tq
    sub = tq // LANES
    log2e = 1.4426950408889634
    c_scr[...] = _flat_cumsum(lf_ref[hh]) * log2e
    lane = lax.broadcasted_iota(jnp.int32, (tq, LANES), 1)

    def prepare(i):
        rows = slice(i * tq, (i + 1) * tq)
        hi, mid, lo = _split3(_row_to_col(_rows_as_lane_vector(c_scr, i * sub, sub)))
        q_ext = jnp.where(lane == 0, hi, jnp.where(lane == 1, mid, jnp.where(lane == 2, lo,
                          jnp.where(lane < 6, 1.0, 0.0))))
        k_ext = jnp.where(lane < 3, 1.0, jnp.where(lane == 3, -hi, jnp.where(lane == 4, -mid,
                          jnp.where(lane == 5, -lo, 0.0))))
        k_rows, v_rows = k_ref[rows, cols], v_ref[rows, cols]
        qx_scr[rows, 0:dh] = (q_ref[rows, cols] * (scale * log2e)).astype(BF16)
        qx_scr[rows, dh:dh + LANES] = q_ext.astype(BF16)
        kx_scr[0:dh, rows] = k_rows.T.astype(BF16)
        kx_scr[dh:dh + LANES, rows] = k_ext.T.astype(BF16)
        vb_scr[rows, :] = v_rows.astype(BF16)
        if aliased:
            ko_ref[rows, cols] = k_rows
            vo_ref[rows, cols] = v_rows
        else:
            for slot in range(ko_ref.shape[0]):
                ko_ref[slot, rows, cols] = k_rows if slot == l else jnp.zeros_like(k_rows)
                vo_ref[slot, rows, cols] = v_rows if slot == l else jnp.zeros_like(v_rows)

    causal = lax.broadcasted_iota(jnp.int32, (tq, tq), 1) <= lax.broadcasted_iota(jnp.int32, (tq, tq), 0)
    slab = 64
    n_slabs = tq // slab

    def scores(i, j):
        s = _dot(qx_scr[i * tq:(i + 1) * tq, :], kx_scr[:, j * tq:(j + 1) * tq])
        if j == i:
            s = jnp.where(causal, s, -jnp.inf)
        s_scr[i % 2, :, j * tq:(j + 1) * tq] = s

    def softmax_slab(i, r):
        nk = (i + 1) * tq
        rs = slice(r * slab, (r + 1) * slab)
        m_acc = s_scr[i % 2, rs, 0:LANES]
        for u in range(1, nk // LANES):
            m_acc = jnp.maximum(m_acc, s_scr[i % 2, rs, u * LANES:(u + 1) * LANES])
        m = jnp.broadcast_to(jnp.max(m_acc, axis=1, keepdims=True), (slab, LANES))
        l_acc = jnp.zeros((slab, LANES), F32)
        for u in range(nk // LANES):
            p = jnp.exp2(s_scr[i % 2, rs, u * LANES:(u + 1) * LANES] - m)
            l_acc = l_acc + p
            p_scr[i % 2, rs, u * LANES:(u + 1) * LANES] = p.astype(BF16)
        return jnp.sum(l_acc, axis=1, keepdims=True)

    def weighted_values(i, denom):
        nk = (i + 1) * tq
        acc = _dot(p_scr[i % 2, :, 0:nk], vb_scr[0:nk, :])
        o_ref[i * tq:(i + 1) * tq, cols] = (acc / jnp.concatenate(denom, axis=0)).astype(o_ref.dtype)

    prepare(0)
    scores(0, 0)
    denom_prev = None
    for i in range(nq):
        next_keys = list(range(i + 2)) if i + 1 < nq else []
        if next_keys:
            prepare(i + 1)
        denom = []
        for r in range(n_slabs):
            for j in next_keys[r::n_slabs]:
                scores(i + 1, j)
            if r == 1 and denom_prev is not None:
                weighted_values(i - 1, denom_prev)
            denom.append(softmax_slab(i, r))
        denom_prev = denom
    weighted_values(nq - 1, denom_prev)


def _fox_prompt_body(*refs, dh, **kw):
    for hh in range(refs[1].shape[1] // dh):
        _fox_head(hh, *refs, dh=dh, **kw)


def _fox_prompt(proj, lf_rows, kv_prev, l, depth, bsz, seq, heads, dh, col_q, col_k, col_v, tq, hps):
    n = seq // LANES
    aliased = kv_prev is not None
    kv_shape = jax.ShapeDtypeStruct((depth, bsz * seq, heads * dh), F32)
    gw = hps * dh
    assert heads % hps == 0 and col_q % hps == 0 and col_k % hps == 0 and col_v % hps == 0
    kv_spec = (pl.BlockSpec((None, seq, gw), lambda b, h: (l, b, h)) if aliased else
               pl.BlockSpec((depth, seq, gw), lambda b, h: (0, b, h)))
    in_specs = [
        pl.BlockSpec((None, hps, n, LANES), lambda b, h: (b, h, 0, 0)),
        pl.BlockSpec((seq, gw), lambda b, h: (b, col_q // hps + h)),
        pl.BlockSpec((seq, gw), lambda b, h: (b, col_k // hps + h)),
        pl.BlockSpec((seq, gw), lambda b, h: (b, col_v // hps + h)),
    ]
    args = [lf_rows, proj, proj, proj]
    if aliased:
        in_specs += [pl.BlockSpec(memory_space=pl.ANY)] * 2
        args += list(kv_prev)
    return pl.pallas_call(
        functools.partial(_fox_prompt_body, tq=tq, dh=dh, scale=dh ** -0.5, l=l, aliased=aliased),
        grid=(bsz, heads // hps),
        in_specs=in_specs,
        out_specs=[pl.BlockSpec((seq, gw), lambda b, h: (b, h)), kv_spec, kv_spec],
        out_shape=[jax.ShapeDtypeStruct((bsz * seq, heads * dh), BF16), kv_shape, kv_shape],
        scratch_shapes=[pltpu.VMEM((hps, n, LANES), F32), pltpu.VMEM((hps, seq, dh + LANES), BF16),
                        pltpu.VMEM((hps, dh + LANES, seq), BF16), pltpu.VMEM((hps, seq, dh), BF16),
                        pltpu.VMEM((hps, 2, tq, seq), F32), pltpu.VMEM((hps, 2, tq, seq), BF16)],
        input_output_aliases={4: 1, 5: 2} if aliased else {},
        compiler_params=_cparams(("parallel", "parallel")),
        name="fox_prompt",
    )(*args)


def _fox_sample_body(lf_ref, q_ref, k_ref, v_ref, ck_ref, cv_ref, *rest, past, new, heads, dh, scale, l,
                     aliased):
    o_ref, ko_ref, vo_ref = rest[2 * aliased:]
    _store_layer_rows(ko_ref, k_ref[...], l, aliased)
    _store_layer_rows(vo_ref, v_ref[...], l, aliased)
    n_past = past // LANES
    causal = lax.broadcasted_iota(jnp.int32, (new, new), 1) <= lax.broadcasted_iota(jnp.int32, (new, new), 0)
    for h in range(heads):
        cols = slice(h * dh, (h + 1) * dh)
        c = _flat_cumsum(lf_ref[h])
        c_past = jnp.concatenate([c[r:r + 1, :] for r in range(n_past)], axis=1)
        c_new = c[n_past:n_past + 1, 0:new]
        cq_col = _row_to_col(c_new)
        q = q_ref[:, cols].astype(BF16)
        head_rows = pl.ds(h, past, stride=heads)
        s_past = _dot_nt(q, ck_ref[head_rows, :].astype(BF16)) * scale + cq_col - c_past
        s_new = _dot_nt(q, k_ref[:, cols].astype(BF16)) * scale + cq_col - c_new
        s_new = jnp.where(causal, s_new, -jnp.inf)
        m = jnp.maximum(jnp.max(s_past, axis=1, keepdims=True), jnp.max(s_new, axis=1, keepdims=True))
        p_past = jnp.exp(s_past - m)
        p_new = jnp.exp(s_new - m)
        denom = jnp.sum(p_past, axis=1, keepdims=True) + jnp.sum(p_new, axis=1, keepdims=True)
        acc = _dot(p_past.astype(BF16), cv_ref[head_rows, :].astype(BF16)) + _dot(
            p_new.astype(BF16), v_ref[:, cols].astype(BF16))
        o_ref[:, cols] = (acc / denom).astype(o_ref.dtype)


def _fox_sample(proj, lf_rows, cache_k, cache_v, kv_prev, l, row0, bsz, new, heads, dh, col_q, col_k, col_v):
    depth = cache_k.shape[0]
    past = cache_k.shape[2] // heads
    fw = heads * dh
    n = lf_rows.shape[2]
    rb0 = row0 // new
    aliased = kv_prev is not None
    kv_shape = jax.ShapeDtypeStruct((depth, bsz * new, fw), F32)
    kv_spec = (pl.BlockSpec((None, new, fw), lambda b: (l, b, 0)) if aliased else
               pl.BlockSpec((depth, new, fw), lambda b: (0, b, 0)))
    in_specs = [
        pl.BlockSpec((None, heads, n, LANES), lambda b: (b, 0, 0, 0)),
        pl.BlockSpec((new, fw), lambda b: (rb0 + b, col_q)),
        pl.BlockSpec((new, fw), lambda b: (rb0 + b, col_k)),
        pl.BlockSpec((new, fw), lambda b: (rb0 + b, col_v)),
        pl.BlockSpec((None, None, past * heads, dh), lambda b: (l, b, 0, 0)),
        pl.BlockSpec((None, None, past * heads, dh), lambda b: (l, b, 0, 0)),
    ]
    args = [lf_rows, proj, proj, proj, cache_k, cache_v]
    if aliased:
        in_specs += [pl.BlockSpec(memory_space=pl.ANY)] * 2
        args += list(kv_prev)
    return pl.pallas_call(
        functools.partial(_fox_sample_body, past=past, new=new, heads=heads, dh=dh, scale=dh ** -0.5,
                          l=l, aliased=aliased),
        grid=(bsz,),
        in_specs=in_specs,
        out_specs=[pl.BlockSpec((new, fw), lambda b: (b, 0)), kv_spec, kv_spec],
        out_shape=[jax.ShapeDtypeStruct((bsz * new, fw), BF16), kv_shape, kv_shape],
        input_output_aliases={6: 1, 7: 2} if aliased else {},
        compiler_params=_cparams(("parallel",)),
        name="fox_sample",
    )(*args)


def _gmlp_body(gu_ref, gv_ref, lnw_ref, lnb_ref, wp_ref, ws_ref, bp_ref, bs_ref, o_ref, v_ref, *,
               n_prompt_tiles, groups, gch):
    i = pl.program_id(0)
    is_sample = i >= n_prompt_tiles
    u = jax.nn.gelu(gu_ref[...])
    gv = jax.nn.gelu(gv_ref[...])
    mu = jnp.mean(gv, axis=-1, keepdims=True)
    var = jnp.mean(jnp.square(gv - mu), axis=-1, keepdims=True)
    v = (gv - mu) * lax.rsqrt(var + EPS) * lnw_ref[...] + lnb_ref[...]

    @pl.when(is_sample)
    def _():
        v_ref[...] = v

    tm = u.shape[0]
    tril = lax.broadcasted_iota(jnp.int32, (GM_CHUNK, GM_CHUNK), 1) <= lax.broadcasted_iota(
        jnp.int32, (GM_CHUNK, GM_CHUNK), 0)
    bias_all = jnp.where(is_sample, bs_ref[...], bp_ref[...])
    for g in range(groups):
        wm = jnp.where(is_sample, ws_ref[g], wp_ref[g])
        wm = jnp.where(tril, wm, 0.0).astype(BF16)
        bias = bias_all[:, g:g + 1]
        for c in range(tm // GM_CHUNK):
            rows = slice(c * GM_CHUNK, (c + 1) * GM_CHUNK)
            cols = slice(g * gch, (g + 1) * gch)
            sp = _dot(wm, v[rows, cols].astype(BF16)) + bias
            o_ref[rows, cols] = (u[rows, cols] * sp).astype(o_ref.dtype)


def _gmlp(proj, lnw, lnb, w_prompt, w_sample, b_prompt, b_sample, l, n_prompt_rows, groups, gch,
          col_u, col_v, tm):
    m = proj.shape[0]
    gw = groups * gch
    npt = n_prompt_rows // tm
    return pl.pallas_call(
        functools.partial(_gmlp_body, n_prompt_tiles=npt, groups=groups, gch=gch),
        grid=(m // tm,),
        in_specs=[
            pl.BlockSpec((tm, gw), lambda i: (i, col_u)),
            pl.BlockSpec((tm, gw), lambda i: (i, col_v)),
            pl.BlockSpec((None, 1, gw), lambda i: (l, 0, 0)),
            pl.BlockSpec((None, 1, gw), lambda i: (l, 0, 0)),
            pl.BlockSpec((None, groups, GM_CHUNK, GM_CHUNK), lambda i: (l, 0, 0, 0)),
            pl.BlockSpec((None, groups, GM_CHUNK, GM_CHUNK), lambda i: (l, 0, 0, 0)),
            pl.BlockSpec((None, GM_CHUNK, groups), lambda i: (l, 0, 0)),
            pl.BlockSpec((None, GM_CHUNK, groups), lambda i: (l, 0, 0)),
        ],
        out_specs=[
            pl.BlockSpec((tm, gw), lambda i: (i, 0)),
            pl.BlockSpec((tm, gw), lambda i: (jnp.maximum(i - npt, 0), 0)),
        ],
        out_shape=[jax.ShapeDtypeStruct((m, gw), BF16), jax.ShapeDtypeStruct((m - n_prompt_rows, gw), F32)],
        compiler_params=_cparams(("arbitrary",)),
        name="gmlp",
    )(proj, proj, lnw, lnb, w_prompt, w_sample, b_prompt, b_sample)


def _ssd_body(xbc_ref, z_ref, misc_ref, dtrow_ref, arow_ref, prev_ref, s0_ref, cw_ref, cb_ref, dvec_ref,
              nw_ref, o_ref, conv_ref, st_ref, xp_scr, s_scr, y_scr, *, lt, q, heads, hp, ngroups, sn, conv_k):
    t = pl.program_id(1)
    nt = pl.num_programs(1)
    xw = heads * hp
    pad = 8

    @pl.when(t == 0)
    def _():
        xp_scr[0:pad, :] = prev_ref[...]
        s_scr[...] = s0_ref[...]

    xbc = xbc_ref[...]
    xp_scr[pad:pad + lt, :] = xbc
    acc = cb_ref[...]
    for i in range(conv_k):
        off = pad - (conv_k - 1) + i
        acc = acc + xp_scr[off:off + lt, :] * cw_ref[i:i + 1, :]
    xp_scr[0:pad, :] = xbc[lt - pad:lt, :]
    act = acc * jax.nn.sigmoid(acc)

    @pl.when(t == nt - 1)
    def _():
        conv_ref[...] = xbc[lt - (conv_k - 1):lt, :]

    dt_rows = dtrow_ref[...]
    ltp = dt_rows.shape[1]
    a_rows = jnp.concatenate(
        [_lane_cumsum(arow_ref[:, j * LANES:(j + 1) * LANES], q) for j in range(ltp // LANES)], axis=1)
    row = lax.broadcasted_iota(jnp.int32, (lt, lt), 0)
    col = lax.broadcasted_iota(jnp.int32, (lt, lt), 1)
    tri_all = jnp.where((col <= row) & (col >= (row & -q)), 1.0, 0.0).astype(BF16)
    a_cols = sum(_dot(tri_all, part.astype(BF16)) for part in _split3(misc_ref[...]))
    x_all = act[:, 0:xw]
    if lt < LANES:
        x_all = jnp.concatenate([x_all, jnp.zeros((LANES - lt, xw), F32)], axis=0)
    x_t = x_all.T
    tril = lax.broadcasted_iota(jnp.int32, (q, q), 1) <= lax.broadcasted_iota(jnp.int32, (q, q), 0)
    rep = heads // ngroups
    for c in range(lt // q):
        r0 = c * q
        bmat = [act[r0:r0 + q, xw + g * sn: xw + (g + 1) * sn].astype(BF16) for g in range(ngroups)]
        cmat = [act[r0:r0 + q, xw + (ngroups + g) * sn: xw + (ngroups + g + 1) * sn].astype(BF16)
                for g in range(ngroups)]
        gmat = [_dot_nt(cmat[g], bmat[g]) for g in range(ngroups)]
        for h in range(heads):
            g = h // rep
            a_row = a_rows[h:h + 1, r0:r0 + q]
            dt_row = dt_rows[h:h + 1, r0:r0 + q]
            a_col = a_cols[r0:r0 + q, 16 + h:17 + h]
            a_last = a_row[:, q - 1:q]
            lmat = jnp.exp(jnp.where(tril, a_col - a_row, -jnp.inf))
            scores = gmat[g] * lmat * dt_row
            xh = act[r0:r0 + q, h * hp:(h + 1) * hp]
            y = _dot(scores.astype(BF16), xh.astype(BF16))
            decay_end = jnp.exp(a_last - a_row) * dt_row
            xd_t = (x_t[h * hp:(h + 1) * hp, r0:r0 + q] * decay_end).astype(BF16)
            chunk_state = _dot(xd_t, bmat[g])
            s_in = s_scr[h]
            y = y + _dot_nt(cmat[g], s_in.astype(BF16)) * jnp.exp(a_col)
            s_scr[h] = s_in * jnp.exp(a_last) + chunk_state
            y_scr[r0:r0 + q, h * hp:(h + 1) * hp] = y + dvec_ref[:, h * hp:(h + 1) * hp] * xh

    zz = z_ref[...]
    o_ref[...] = _rmsnorm(y_scr[...] * (zz * jax.nn.sigmoid(zz)), nw_ref[...]).astype(o_ref.dtype)

    @pl.when(t == nt - 1)
    def _():
        st_ref[...] = s_scr[...]


def _ssd(proj, dt_rows, a_rows, conv_prev, s0, conv_w, conv_b, dvec, norm_w, l, row0, bsz, seq, lt, q,
         heads, hp, ngroups, sn, col_xbc, col_z, col_misc):
    nt = seq // lt
    xw = heads * hp
    cch = xw + 2 * ngroups * sn
    conv_k = conv_w.shape[1]
    rb0 = row0 // lt
    ltp = max(lt, LANES)
    body = functools.partial(_ssd_body, lt=lt, q=q, heads=heads, hp=hp, ngroups=ngroups, sn=sn, conv_k=conv_k)
    return pl.pallas_call(
        body,
        grid=(bsz, nt),
        in_specs=[
            pl.BlockSpec((lt, cch), lambda b, t: (rb0 + b * nt + t, col_xbc)),
            pl.BlockSpec((lt, xw), lambda b, t: (rb0 + b * nt + t, col_z)),
            pl.BlockSpec((lt, MISC_W), lambda b, t: (rb0 + b * nt + t, col_misc)),
            pl.BlockSpec((None, heads, ltp), lambda b, t: (b, 0, t)),
            pl.BlockSpec((None, heads, ltp), lambda b, t: (b, 0, t)),
            pl.BlockSpec((None, 8, cch), lambda b, t: (b, 0, 0)),
            pl.BlockSpec((None, heads, hp, sn), lambda b, t: (b, 0, 0, 0)),
            pl.BlockSpec((None, conv_k, cch), lambda b, t: (l, 0, 0)),
            pl.BlockSpec((None, 1, cch), lambda b, t: (l, 0, 0)),
            pl.BlockSpec((None, 1, xw), lambda b, t: (l, 0, 0)),
            pl.BlockSpec((None, 1, xw), lambda b, t: (l, 0, 0)),
        ],
        out_specs=[
            pl.BlockSpec((lt, xw), lambda b, t: (b * nt + t, 0)),
            pl.BlockSpec((None, conv_k - 1, cch), lambda b, t: (b, 0, 0)),
            pl.BlockSpec((None, heads, hp, sn), lambda b, t: (b, 0, 0, 0)),
        ],
        out_shape=[
            jax.ShapeDtypeStruct((bsz * seq, xw), BF16),
            jax.ShapeDtypeStruct((bsz, conv_k - 1, cch), F32),
            jax.ShapeDtypeStruct((bsz, heads, hp, sn), F32),
        ],
        scratch_shapes=[pltpu.VMEM((lt + 8, cch), F32), pltpu.VMEM((heads, hp, sn), F32),
                        pltpu.VMEM((lt, xw), F32)],
        compiler_params=_cparams(("parallel", "arbitrary")),
        name="ssd",
    )(proj, proj, proj, dt_rows, a_rows, conv_prev, s0, conv_w, conv_b, dvec, norm_w)


def _outproj_body(foxp_ref, foxs_ref, gm_ref, ssdp_ref, ssds_ref, x_ref, nw_ref, w_ref, o_ref, *, fw, gw, rows_a):
    i = pl.program_id(0)
    last_i = pl.num_programs(0) - 1

    def run(fox, ssd):
        o = _dot(fox, w_ref[0:fw, :])
        o = o + _dot(gm_ref[...], w_ref[fw:fw + gw, :])
        o = o + _dot(ssd, w_ref[fw + gw:, :])
        o_ref[...] = x_ref[...] + _rmsnorm(o, nw_ref[...])

    pl.when(i < last_i)(lambda: run(foxp_ref[...], ssdp_ref[...]))
    pl.when(i == last_i)(lambda: run(jnp.concatenate([foxp_ref[0:rows_a, :], foxs_ref[...]], axis=0),
                                     jnp.concatenate([ssdp_ref[0:rows_a, :], ssds_ref[...]], axis=0)))


def _outproj(fox_p, fox_s, gm_o, ssd_p, ssd_s, x, norm_w3, w_out, l, tm):
    m, d = x.shape
    n_a, n_b = fox_p.shape[0], fox_s.shape[0]
    fw, gw, sw = fox_p.shape[1], gm_o.shape[1], ssd_p.shape[1]
    rows_a = n_a % tm
    assert m % tm == 0 and rows_a + n_b == tm and rows_a % 16 == 0
    return pl.pallas_call(
        functools.partial(_outproj_body, fw=fw, gw=gw, rows_a=rows_a),
        grid=(m // tm,),
        in_specs=[
            pl.BlockSpec((tm, fw), lambda i: (i, 0)),
            pl.BlockSpec((n_b, fw), lambda i: (0, 0)),
            pl.BlockSpec((tm, gw), lambda i: (i, 0)),
            pl.BlockSpec((tm, sw), lambda i: (i, 0)),
            pl.BlockSpec((n_b, sw), lambda i: (0, 0)),
            pl.BlockSpec((tm, d), lambda i: (i, 0)),
            pl.BlockSpec((None, 1, d), lambda i: (l * 6 + 3, 0, 0)),
            pl.BlockSpec((None, fw + gw + sw, d), lambda i: (l, 0, 0)),
        ],
        out_specs=pl.BlockSpec((tm, d), lambda i: (i, 0)),
        out_shape=jax.ShapeDtypeStruct((m, d), F32),
        compiler_params=_cparams(("parallel",)),
        name=f"outproj_{l}",
    )(fox_p, fox_s, gm_o, ssd_p, ssd_s, x, norm_w3, w_out)


def _head_rows(vals, bsz, seq):
    return jnp.transpose(vals.reshape(bsz, seq, vals.shape[-1]), (0, 2, 1))


def _pad_lanes(rows, width):
    return jnp.pad(rows, ((0, 0), (0, 0), (0, width - rows.shape[-1])))


def kernel(x_prompt, x_sample, cache_fox_k, cache_fox_v, cache_fox_logf, state_ssd_conv, state_ssd, norm_w,
           w_ffn_gate, w_ffn_up, w_ffn_down, w_in, fox_fb, gm_ln_w, gm_ln_b, gm_ws, gm_bs, ssd_conv_w,
           ssd_conv_b, ssd_dt_bias, ssd_a_log, ssd_d, ssd_norm_w, w_out):
    bsz, seq, d = x_prompt.shape
    dbsz, dseq, _ = x_sample.shape
    depth, _, past, heads, dh = cache_fox_k.shape
    fox_w = heads * dh
    groups, gchunk = gm_ws.shape[1], gm_ws.shape[2]
    gm_w = gm_ln_w.shape[1]
    gch = gm_w // groups
    _, _, sheads, hp, sn = state_ssd.shape
    ssd_w = sheads * hp
    cch = ssd_conv_w.shape[2]
    ngroups = (cch - ssd_w) // (2 * sn)
    conv_k = ssd_conv_w.shape[1]
    n_p, n_s = bsz * seq, dbsz * dseq
    m = n_p + n_s
    assert gchunk == GM_CHUNK and heads == 8 and sheads == 8 and dh == LANES
    assert seq % GM_CHUNK == 0 and GM_CHUNK % dseq == 0 and dseq < SSD_CHUNK and past % LANES == 0

    wg, wd = w_ffn_gate, w_ffn_down
    wu_first, wu_last = w_ffn_up[0, 0].astype(BF16), w_ffn_up[depth - 1, 1].astype(BF16)
    wo = w_out.astype(BF16)
    sizes = [fox_w, fox_w, fox_w, heads, gm_w, gm_w, ssd_w, cch, sheads]
    offs = [0]
    for s in sizes:
        offs.append(offs[-1] + s)
    w_t = jnp.transpose(w_in, (0, 2, 1))
    wq, wk, wv, wf, wgu, wgv, wz, wxbc, wdt = (w_t[:, offs[i]:offs[i + 1], :] for i in range(9))
    tn = 1024
    used = 3 * fox_w + 2 * gm_w + cch + ssd_w + MISC_W
    npad = -(-used // tn) * tn
    zeros_misc = jnp.zeros((depth, MISC_W - heads - 2 * sheads, d), F32)
    zeros_tail = jnp.zeros((depth, npad - used, d), F32)
    w_r = jnp.concatenate([wq, wk, wv, wgu, wgv, wxbc, wz, wf, wdt, wdt, zeros_misc, zeros_tail],
                          axis=1).astype(BF16)
    col_q, col_k, col_v = 0, fox_w // dh, 2 * fox_w // dh
    col_u, col_gv = 3 * fox_w // gm_w, 3 * fox_w // gm_w + 1
    off_xbc = 3 * fox_w + 2 * gm_w
    off_z = off_xbc + cch
    off_misc = off_z + ssd_w
    assert off_xbc % cch == 0 and off_z % ssd_w == 0 and off_misc % MISC_W == 0
    misc_in_tile = off_misc % tn
    assert off_misc // tn == npad // tn - 1 and misc_in_tile + MISC_W <= tn

    lane_pad = jnp.zeros((depth, MISC_W - heads - 2 * sheads), F32)
    bias_vec = jnp.concatenate([fox_fb, ssd_dt_bias, ssd_dt_bias, lane_pad], axis=-1)[:, None, :]
    alog_vec = jnp.concatenate([jnp.zeros((depth, heads + sheads), F32), ssd_a_log, lane_pad], axis=-1)[:, None, :]
    norm_w3 = norm_w.reshape(depth * 6, 1, d)
    lnw3, lnb3 = gm_ln_w[:, None, :], gm_ln_b[:, None, :]
    reps = GM_CHUNK // dseq
    eye = jnp.eye(reps, dtype=F32)
    ws_s = jnp.einsum('ab,lgts->lgatbs', eye, gm_ws[:, :, :dseq, :dseq]).reshape(depth, groups, GM_CHUNK, GM_CHUNK)
    bs_p = jnp.transpose(gm_bs, (0, 2, 1))
    bs_s = jnp.tile(jnp.transpose(gm_bs[:, :, :dseq], (0, 2, 1)), (1, reps, 1))
    conv_b3 = ssd_conv_b[:, None, :]
    dvec = jnp.repeat(ssd_d, hp, axis=-1)[:, None, :]
    snw3 = ssd_norm_w[:, None, :]
    cache_k = cache_fox_k.reshape(depth, dbsz, past * heads, dh)
    cache_v = cache_fox_v.reshape(depth, dbsz, past * heads, dh)
    conv_prev_p = jnp.zeros((bsz, 8, cch), F32)
    conv_prev_s = jnp.pad(state_ssd_conv, ((0, 0), (0, 0), (8 - (conv_k - 1), 0), (0, 0)))
    s0_p = jnp.zeros((bsz, sheads, hp, sn), F32)

    x = (x_prompt.reshape(n_p, d), x_sample.reshape(n_s, d))
    tm_big = m // 8 if m % 64 == 0 and (m // 8) % 16 == 0 else 256
    tm_ffn = m // 12 if m % 12 == 0 and (m // 12) % 16 == 0 else 256
    tm_out = m // 16 if m % 16 == 0 and (m // 16) % 16 == 0 else 256
    lf_len = 2 * past
    assert past + dseq <= lf_len

    outs = {k: [] for k in ("plf", "pconv", "pst", "slf", "sgm", "sconv", "sst")}
    kv_p = kv_s = None
    for l in range(depth):
        x = _ffn(x, norm_w3, wg, wu_first if l == 0 else w_ffn_up, wd, l, 0, tm_ffn, 512)
        proj = _inproj(x, norm_w3, w_r, bias_vec, alog_vec, l, tm_big, tn, misc_in_tile)
        misc = proj[:, off_misc:off_misc + heads + 2 * sheads]
        logf_p, logf_s = misc[:n_p, :heads], misc[n_p:, :heads]
        lf_rows_p = _head_rows(logf_p, bsz, seq).reshape(bsz, heads, seq // LANES, LANES)
        lf_all = jnp.concatenate([cache_fox_logf[l], logf_s.reshape(dbsz, dseq, heads),
                                  jnp.zeros((dbsz, lf_len - past - dseq, heads), F32)], axis=1)
        lf_rows_s = jnp.transpose(lf_all, (0, 2, 1)).reshape(dbsz, heads, lf_len // LANES, LANES)
        dt_rows_p = _head_rows(misc[:n_p, heads:heads + sheads], bsz, seq)
        a_rows_p = _head_rows(misc[:n_p, heads + sheads:], bsz, seq)
        dt_rows_s = _pad_lanes(_head_rows(misc[n_p:, heads:heads + sheads], dbsz, dseq), LANES)
        a_rows_s = _pad_lanes(_head_rows(misc[n_p:, heads + sheads:], dbsz, dseq), LANES)

        fox_p, *kv_p = _fox_prompt(proj, lf_rows_p, kv_p, l, depth, bsz, seq, heads, dh, col_q, col_k, col_v, 256, 2)
        fox_s, *kv_s = _fox_sample(proj, lf_rows_s, cache_k, cache_v, kv_s, l, n_p, dbsz, dseq, heads, dh, 0, 1, 2)
        gm_o, gm_v_s = _gmlp(proj, lnw3, lnb3, gm_ws, ws_s, bs_p, bs_s, l, n_p, groups, gch, col_u, col_gv, 256)
        ssd_p, conv_p, st_p = _ssd(proj, dt_rows_p, a_rows_p, conv_prev_p, s0_p, ssd_conv_w, conv_b3, dvec, snw3,
                                   l, 0, bsz, seq, 256, SSD_CHUNK, sheads, hp, ngroups, sn,
                                   off_xbc // cch, off_z // ssd_w, off_misc // MISC_W)
        ssd_s, conv_s, st_s = _ssd(proj, dt_rows_s, a_rows_s, conv_prev_s[l], state_ssd[l], ssd_conv_w, conv_b3,
                                   dvec, snw3, l, n_p, dbsz, dseq, dseq, dseq, sheads, hp, ngroups, sn,
                                   off_xbc // cch, off_z // ssd_w, off_misc // MISC_W)
        x = _outproj(fox_p, fox_s, gm_o, ssd_p, ssd_s, x, norm_w3, wo, l, tm_out)
        x = _ffn(x, norm_w3, wg, wu_last if l == depth - 1 else w_ffn_up, wd, l, 1, tm_ffn, 512,
                 split_out_rows=(n_p, n_s) if l == depth - 1 else None)

        outs["plf"].append(logf_p.reshape(bsz, seq, heads))
        outs["pconv"].append(conv_p)
        outs["pst"].append(st_p)
        outs["slf"].append(logf_s.reshape(dbsz, dseq, heads))
        outs["sgm"].append(gm_v_s.reshape(dbsz, dseq, gm_w))
        outs["sconv"].append(conv_s)
        outs["sst"].append(st_s)

    st = {k: jnp.stack(v) for k, v in outs.items()}
    y_prompt = x[0].reshape(bsz, seq, d)
    y_sample = x[1].reshape(dbsz, dseq, d)
    p_k, p_v = (a.reshape(depth, bsz, seq, heads, dh) for a in kv_p)
    s_k, s_v = (a.reshape(depth, dbsz, dseq, heads, dh) for a in kv_s)
    return (y_prompt, y_sample, p_k, p_v, st["plf"], st["pconv"], st["pst"],
            s_k, s_v, st["slf"], st["sgm"], st["sconv"], st["sst"])
```

```python
import functools

import jax
import jax.numpy as jnp
from jax import lax
from jax.experimental import pallas as pl
from jax.experimental.pallas import tpu as pltpu

F32 = jnp.float32
BF16 = jnp.bfloat16
EPS = 1e-6
LANES = 128
SSD_CHUNK = 64
GM_CHUNK = 128
VMEM_LIMIT = 56 * 1024 * 1024
FFN_VMEM_LIMIT = 60 * 1024 * 1024

MISC_W = LANES


def _cparams(sem, vmem_limit=VMEM_LIMIT):
    return pltpu.CompilerParams(dimension_semantics=sem, vmem_limit_bytes=vmem_limit)


def _rmsnorm(x, w):
    return x * lax.rsqrt(jnp.mean(x * x, axis=-1, keepdims=True) + EPS) * w


def _dot(a, b):
    return jnp.dot(a, b, preferred_element_type=F32)


def _dot_nt(a, b):
    return lax.dot_general(a, b, (((1,), (1,)), ((), ())), preferred_element_type=F32)


def _dot_tn(a, b):
    return lax.dot_general(a, b, (((0,), (0,)), ((), ())), preferred_element_type=F32)


def _row_to_col(row):
    n = row.shape[1]
    eye = lax.broadcasted_iota(jnp.int32, (n, n), 0) == lax.broadcasted_iota(jnp.int32, (n, n), 1)
    return jnp.sum(jnp.where(eye, jnp.broadcast_to(row, (n, n)), 0.0), axis=1, keepdims=True)


def _lane_cumsum(x, seg):
    lane = lax.broadcasted_iota(jnp.int32, x.shape, 1) & (seg - 1)
    k = 1
    while k < seg:
        x = x + jnp.where(lane >= k, pltpu.roll(x, k, axis=1), 0.0)
        k *= 2
    return x


def _flat_cumsum(x):
    x = _lane_cumsum(x, LANES)
    tot = x[:, LANES - 1:LANES]
    rows = []
    carry = jnp.zeros((1, 1), F32)
    for r in range(x.shape[0]):
        rows.append(x[r:r + 1, :] + carry)
        carry = carry + tot[r:r + 1, :]
    return jnp.concatenate(rows, axis=0)


def _ffn_body(*refs, split_in, split_out, rows_a):
    refs = list(refs)
    x_ref = refs.pop(0)
    xs_ref = refs.pop(0) if split_in else None
    nwa_ref, nwb_ref, wg_ref, wu_ref, wd_ref, o_ref = refs[:6]
    os_ref = refs[6] if split_out else None
    h_ref = refs[-1]
    i, f = pl.program_id(0), pl.program_id(1)
    last_i, last_f = pl.num_programs(0) - 1, pl.num_programs(1) - 1

    tm = o_ref.shape[0]
    whole = [(slice(0, tm), x_ref, slice(0, tm))]
    pieces_last = [(slice(0, rows_a), x_ref, slice(0, rows_a)), (slice(rows_a, tm), xs_ref, slice(None))]

    def per_piece(fn):
        if not split_in:
            fn(whole)
            return
        pl.when(i < last_i)(lambda: fn(whole))
        pl.when(i == last_i)(lambda: fn(pieces_last))

    def start(pieces):
        for rows, ref, src in pieces:
            h_ref[rows, :] = _rmsnorm(ref[src, :], nwa_ref[...]).astype(BF16)
        o_ref[...] = jnp.zeros_like(o_ref)

    pl.when(f == 0)(lambda: per_piece(start))

    h = h_ref[...]
    a = _dot(h, wg_ref[...].astype(BF16))
    g = _dot(h, wu_ref[...])
    act = (a * jax.nn.sigmoid(a) * g).astype(BF16)
    o_ref[...] += _dot(act, wd_ref[...].astype(BF16))

    def finish(pieces):
        for rows, ref, src in pieces:
            o_ref[rows, :] = ref[src, :] + _rmsnorm(o_ref[rows, :], 0.5 * nwb_ref[...])
        if split_out:
            @pl.when(i == last_i)
            def _():
                os_ref[...] = o_ref[rows_a:, :]

    pl.when(f == last_f)(lambda: per_piece(finish))


def _ffn(x, norm_w3, wg, wu, wd, l, j, tm, tf, split_out_rows=None):
    split_in = isinstance(x, tuple)
    split_out = split_out_rows is not None
    xs = list(x) if split_in else [x]
    n_a, n_b = (xs[0].shape[0], xs[1].shape[0]) if split_in else (split_out_rows or (x.shape[0], 0))
    m, d = n_a + n_b, xs[0].shape[1]
    dff = wg.shape[-1]
    rows_a = n_a % tm
    assert m % tm == 0 and (not (split_in or split_out) or (rows_a + n_b == tm and rows_a % 8 == 0))
    row_spec = pl.BlockSpec((tm, d), lambda i, f: (i, 0))
    tail_spec = pl.BlockSpec((n_b, d), lambda i, f: (0, 0), pipeline_mode=pl.Buffered(1))
    out_shape = jax.ShapeDtypeStruct((m, d), F32)
    if split_out:
        out_shape = [jax.ShapeDtypeStruct((n_a, d), F32), jax.ShapeDtypeStruct((n_b, d), F32)]
    return pl.pallas_call(
        functools.partial(_ffn_body, split_in=split_in, split_out=split_out, rows_a=rows_a),
        grid=(m // tm, dff // tf),
        in_specs=[row_spec] + ([tail_spec] if split_in else []) + [
            pl.BlockSpec((None, 1, d), lambda i, f: (l * 6 + 3 * j + j, 0, 0)),
            pl.BlockSpec((None, 1, d), lambda i, f: (l * 6 + 3 * j + j + 1, 0, 0)),
            pl.BlockSpec((None, None, d, tf), lambda i, f: (l, j, 0, f)),
            pl.BlockSpec((None, None, d, tf), lambda i, f: (l, j, 0, f)),
            pl.BlockSpec((None, None, tf, d), lambda i, f: (l, j, f, 0)),
        ],
        out_specs=[row_spec, tail_spec] if split_out else row_spec,
        out_shape=out_shape,
        scratch_shapes=[pltpu.VMEM((tm, d), BF16)],
        compiler_params=_cparams(("parallel", "arbitrary"), FFN_VMEM_LIMIT),
        name=f"ffn_{l}_{j}",
    )(*xs, norm_w3, norm_w3, wg, wu, wd)


def _inproj_body(x_ref, nw_ref, w_ref, bias_ref, alog_ref, o_ref, m_ref, h_ref, *, misc_off):
    j = pl.program_id(1)

    @pl.when(j == 0)
    def _():
        h_ref[...] = _rmsnorm(x_ref[...], nw_ref[...]).astype(BF16)

    acc = _dot_nt(h_ref[...], w_ref[...])
    o_ref[...] = acc

    @pl.when(j == pl.num_programs(1) - 1)
    def _():
        raw = acc[:, misc_off:misc_off + MISC_W] + bias_ref[...]
        lane = lax.broadcasted_iota(jnp.int32, raw.shape, 1)
        tail = jnp.log1p(jnp.exp(-jnp.abs(raw)))
        sp = jnp.maximum(raw, 0.0) + tail
        logf = jnp.minimum(raw, 0.0) - tail
        neg_a = -jnp.exp(alog_ref[...])
        val = jnp.where(lane < 8, logf, jnp.where(lane < 16, sp, jnp.where(lane < 24, sp * neg_a, 0.0)))
        o_ref[:, misc_off:misc_off + MISC_W] = val
        m_ref[...] = val


def _inproj(x, norm_w3, w_r, bias_vec, alog_vec, l, tm, tn, misc_off):
    m, d = x.shape
    npad = w_r.shape[1]
    return pl.pallas_call(
        functools.partial(_inproj_body, misc_off=misc_off),
        grid=(m // tm, npad // tn),
        in_specs=[
            pl.BlockSpec((tm, d), lambda i, j: (i, 0)),
            pl.BlockSpec((None, 1, d), lambda i, j: (l * 6 + 2, 0, 0)),
            pl.BlockSpec((None, tn, d), lambda i, j: (l, j, 0)),
            pl.BlockSpec((None, 1, MISC_W), lambda i, j: (l, 0, 0)),
            pl.BlockSpec((None, 1, MISC_W), lambda i, j: (l, 0, 0)),
        ],
        out_specs=[pl.BlockSpec((tm, tn), lambda i, j: (i, j)), pl.BlockSpec((tm, MISC_W), lambda i, j: (i, 0))],
        out_shape=[jax.ShapeDtypeStruct((m, npad), F32), jax.ShapeDtypeStruct((m, MISC_W), F32)],
        scratch_shapes=[pltpu.VMEM((tm, d), BF16)],
        compiler_params=_cparams(("parallel", "arbitrary")),
        name=f"inproj_{l}",
    )(x, norm_w3, w_r, bias_vec, alog_vec)


def _rows_as_lane_vector(c_ref, start, count):
    rows = c_ref[pl.ds(start, count), :]
    return jnp.concatenate([rows[r:r + 1, :] for r in range(count)], axis=1)


def _store_layer_rows(out_ref, val, l, aliased):
    if aliased:
        out_ref[...] = val
    else:
        for slot in range(out_ref.shape[0]):
            out_ref[slot] = val if slot == l else jnp.zeros_like(val)


def _split3(x):
    hi = x.astype(BF16).astype(F32)
    r = x - hi
    mid = r.astype(BF16).astype(F32)
    return hi, mid, r - mid


def _fox_head(hh, lf_ref, q_ref, k_ref, v_ref, *rest, tq, dh, scale, l, aliased):
    o_ref, ko_ref, vo_ref = rest[2 * aliased:2 * aliased + 3]
    c_scr, qx_scr, kx_scr, vb_scr, s_scr, p_scr = (scr.at[hh] for scr in rest[2 * aliased + 3:])
    seq = q_ref.shape[0]
    cols = slice(hh * dh, (hh + 1) * dh)
    nq = seq // tq
    sub = tq // LANES
    log2e = 1.4426950408889634
    c_scr[...] = _flat_cumsum(lf_ref[hh]) * log2e
    lane = lax.broadcasted_iota(jnp.int32, (tq, LANES), 1)

    def prepare(i):
        rows = slice(i * tq, (i + 1) * tq)
        hi, mid, lo = _split3(_row_to_col(_rows_as_lane_vector(c_scr, i * sub, sub)))
        q_ext = jnp.where(lane == 0, hi, jnp.where(lane == 1, mid, jnp.where(lane == 2, lo,
                          jnp.where(lane < 6, 1.0, 0.0))))
        k_ext = jnp.where(lane < 3, 1.0, jnp.where(lane == 3, -hi, jnp.where(lane == 4, -mid,
                          jnp.where(lane == 5, -lo, 0.0))))
        k_rows, v_rows = k_ref[rows, cols], v_ref[rows, cols]
        qx_scr[rows, 0:dh] = (q_ref[rows, cols] * (scale * log2e)).astype(BF16)
        qx_scr[rows, dh:dh + LANES] = q_ext.astype(BF16)
        kx_scr[0:dh, rows] = k_rows.T.astype(BF16)
        kx_scr[dh:dh + LANES, rows] = k_ext.T.astype(BF16)
        vb_scr[rows, :] = v_rows.astype(BF16)
        if aliased:
            ko_ref[rows, cols] = k_rows
            vo_ref[rows, cols] = v_rows
        else:
            for slot in range(ko_ref.shape[0]):
                ko_ref[slot, rows, cols] = k_rows if slot == l else jnp.zeros_like(k_rows)
                vo_ref[slot, rows, cols] = v_rows if slot == l else jnp.zeros_like(v_rows)

    causal = lax.broadcasted_iota(jnp.int32, (tq, tq), 1) <= lax.broadcasted_iota(jnp.int32, (tq, tq), 0)
    slab = 64
    n_slabs = tq // slab

    def scores(i, j):
        s = _dot(qx_scr[i * tq:(i + 1) * tq, :], kx_scr[:, j * tq:(j + 1) * tq])
        if j == i:
            s = jnp.where(causal, s, -jnp.inf)
        s_scr[i % 2, :, j * tq:(j + 1) * tq] = s

    def softmax_slab(i, r):
        nk = (i + 1) * tq
        rs = slice(r * slab, (r + 1) * slab)
        m_acc = s_scr[i % 2, rs, 0:LANES]
        for u in range(1, nk // LANES):
            m_acc = jnp.maximum(m_acc, s_scr[i % 2, rs, u * LANES:(u + 1) * LANES])
        m = jnp.broadcast_to(jnp.max(m_acc, axis=1, keepdims=True), (slab, LANES))
        l_acc = jnp.zeros((slab, LANES), F32)
        for u in range(nk // LANES):
            p = jnp.exp2(s_scr[i % 2, rs, u * LANES:(u + 1) * LANES] - m)
            l_acc = l_acc + p
            p_scr[i % 2, rs, u * LANES:(u + 1) * LANES] = p.astype(BF16)
        return jnp.sum(l_acc, axis=1, keepdims=True)

    def weighted_values(i, denom):
        nk = (i + 1) * tq
        acc = _dot(p_scr[i % 2, :, 0:nk], vb_scr[0:nk, :])
        o_ref[i * tq:(i + 1) * tq, cols] = (acc / jnp.concatenate(denom, axis=0)).astype(o_ref.dtype)

    prepare(0)
    scores(0, 0)
    denom_prev = None
    for i in range(nq):
        next_keys = list(range(i + 2)) if i + 1 < nq else []
        if next_keys:
            prepare(i + 1)
        denom = []
        for r in range(n_slabs):
            for j in next_keys[r::n_slabs]:
                scores(i + 1, j)
            if r == 1 and denom_prev is not None:
                weighted_values(i - 1, denom_prev)
            denom.append(softmax_slab(i, r))
        denom_prev = denom
    weighted_values(nq - 1, denom_prev)


def _fox_prompt_body(*refs, dh, **kw):
    for hh in range(refs[1].shape[1] // dh):
        _fox_head(hh, *refs, dh=dh, **kw)


def _fox_prompt(proj, lf_rows, kv_prev, l, depth, bsz, seq, heads, dh, col_q, col_k, col_v, tq, hps):
    n = seq // LANES
    aliased = kv_prev is not None
    kv_shape = jax.ShapeDtypeStruct((depth, bsz * seq, heads * dh), F32)
    gw = hps * dh
    assert heads % hps == 0 and col_q % hps == 0 and col_k % hps == 0 and col_v % hps == 0
    kv_spec = (pl.BlockSpec((None, seq, gw), lambda b, h: (l, b, h)) if aliased else
               pl.BlockSpec((depth, seq, gw), lambda b, h: (0, b, h)))
    in_specs = [
        pl.BlockSpec((None, hps, n, LANES), lambda b, h: (b, h, 0, 0)),
        pl.BlockSpec((seq, gw), lambda b, h: (b, col_q // hps + h)),
        pl.BlockSpec((seq, gw), lambda b, h: (b, col_k // hps + h)),
        pl.BlockSpec((seq, gw), lambda b, h: (b, col_v // hps + h)),
    ]
    args = [lf_rows, proj, proj, proj]
    if aliased:
        in_specs += [pl.BlockSpec(memory_space=pl.ANY)] * 2
        args += list(kv_prev)
    return pl.pallas_call(
        functools.partial(_fox_prompt_body, tq=tq, dh=dh, scale=dh ** -0.5, l=l, aliased=aliased),
        grid=(bsz, heads // hps),
        in_specs=in_specs,
        out_specs=[pl.BlockSpec((seq, gw), lambda b, h: (b, h)), kv_spec, kv_spec],
        out_shape=[jax.ShapeDtypeStruct((bsz * seq, heads * dh), BF16), kv_shape, kv_shape],
        scratch_shapes=[pltpu.VMEM((hps, n, LANES), F32), pltpu.VMEM((hps, seq, dh + LANES), BF16),
                        pltpu.VMEM((hps, dh + LANES, seq), BF16), pltpu.VMEM((hps, seq, dh), BF16),
                        pltpu.VMEM((hps, 2, tq, seq), F32), pltpu.VMEM((hps, 2, tq, seq), BF16)],
        input_output_aliases={4: 1, 5: 2} if aliased else {},
        compiler_params=_cparams(("parallel", "parallel")),
        name="fox_prompt",
    )(*args)


def _fox_sample_body(lf_ref, q_ref, k_ref, v_ref, ck_ref, cv_ref, *rest, past, new, heads, dh, scale, l,
                     aliased):
    o_ref, ko_ref, vo_ref = rest[2 * aliased:]
    _store_layer_rows(ko_ref, k_ref[...], l, aliased)
    _store_layer_rows(vo_ref, v_ref[...], l, aliased)
    n_past = past // LANES
    causal = lax.broadcasted_iota(jnp.int32, (new, new), 1) <= lax.broadcasted_iota(jnp.int32, (new, new), 0)
    for h in range(heads):
        cols = slice(h * dh, (h + 1) * dh)
        c = _flat_cumsum(lf_ref[h])
        c_past = jnp.concatenate([c[r:r + 1, :] for r in range(n_past)], axis=1)
        c_new = c[n_past:n_past + 1, 0:new]
        cq_col = _row_to_col(c_new)
        q = q_ref[:, cols].astype(BF16)
        head_rows = pl.ds(h, past, stride=heads)
        s_past = _dot_nt(q, ck_ref[head_rows, :].astype(BF16)) * scale + cq_col - c_past
        s_new = _dot_nt(q, k_ref[:, cols].astype(BF16)) * scale + cq_col - c_new
        s_new = jnp.where(causal, s_new, -jnp.inf)
        m = jnp.maximum(jnp.max(s_past, axis=1, keepdims=True), jnp.max(s_new, axis=1, keepdims=True))
        p_past = jnp.exp(s_past - m)
        p_new = jnp.exp(s_new - m)
        denom = jnp.sum(p_past, axis=1, keepdims=True) + jnp.sum(p_new, axis=1, keepdims=True)
        acc = _dot(p_past.astype(BF16), cv_ref[head_rows, :].astype(BF16)) + _dot(
            p_new.astype(BF16), v_ref[:, cols].astype(BF16))
        o_ref[:, cols] = (acc / denom).astype(o_ref.dtype)


def _fox_sample(proj, lf_rows, cache_k, cache_v, kv_prev, l, row0, bsz, new, heads, dh, col_q, col_k, col_v):
    depth = cache_k.shape[0]
    past = cache_k.shape[2] // heads
    fw = heads * dh
    n = lf_rows.shape[2]
    rb0 = row0 // new
    aliased = kv_prev is not None
    kv_shape = jax.ShapeDtypeStruct((depth, bsz * new, fw), F32)
    kv_spec = (pl.BlockSpec((None, new, fw), lambda b: (l, b, 0)) if aliased else
               pl.BlockSpec((depth, new, fw), lambda b: (0, b, 0)))
    in_specs = [
        pl.BlockSpec((None, heads, n, LANES), lambda b: (b, 0, 0, 0)),
        pl.BlockSpec((new, fw), lambda b: (rb0 + b, col_q)),
        pl.BlockSpec((new, fw), lambda b: (rb0 + b, col_k)),
        pl.BlockSpec((new, fw), lambda b: (rb0 + b, col_v)),
        pl.BlockSpec((None, None, past * heads, dh), lambda b: (l, b, 0, 0)),
        pl.BlockSpec((None, None, past * heads, dh), lambda b: (l, b, 0, 0)),
    ]
    args = [lf_rows, proj, proj, proj, cache_k, cache_v]
    if aliased:
        in_specs += [pl.BlockSpec(memory_space=pl.ANY)] * 2
        args += list(kv_prev)
    return pl.pallas_call(
        functools.partial(_fox_sample_body, past=past, new=new, heads=heads, dh=dh, scale=dh ** -0.5,
                          l=l, aliased=aliased),
        grid=(bsz,),
        in_specs=in_specs,
        out_specs=[pl.BlockSpec((new, fw), lambda b: (b, 0)), kv_spec, kv_spec],
        out_shape=[jax.ShapeDtypeStruct((bsz * new, fw), BF16), kv_shape, kv_shape],
        input_output_aliases={6: 1, 7: 2} if aliased else {},
        compiler_params=_cparams(("parallel",)),
        name="fox_sample",
    )(*args)


def _gmlp_body(gu_ref, gv_ref, lnw_ref, lnb_ref, wp_ref, ws_ref, bp_ref, bs_ref, o_ref, v_ref, *,
               n_prompt_tiles, groups, gch):
    i = pl.program_id(0)
    is_sample = i >= n_prompt_tiles
    u = jax.nn.gelu(gu_ref[...])
    gv = jax.nn.gelu(gv_ref[...])
    mu = jnp.mean(gv, axis=-1, keepdims=True)
    var = jnp.mean(jnp.square(gv - mu), axis=-1, keepdims=True)
    v = (gv - mu) * lax.rsqrt(var + EPS) * lnw_ref[...] + lnb_ref[...]

    @pl.when(is_sample)
    def _():
        v_ref[...] = v

    tm = u.shape[0]
    tril = lax.broadcasted_iota(jnp.int32, (GM_CHUNK, GM_CHUNK), 1) <= lax.broadcasted_iota(
        jnp.int32, (GM_CHUNK, GM_CHUNK), 0)
    bias_all = jnp.where(is_sample, bs_ref[...], bp_ref[...])
    for g in range(groups):
        wm = jnp.where(is_sample, ws_ref[g], wp_ref[g])
        wm = jnp.where(tril, wm, 0.0).astype(BF16)
        bias = bias_all[:, g:g + 1]
        for c in range(tm // GM_CHUNK):
            rows = slice(c * GM_CHUNK, (c + 1) * GM_CHUNK)
            cols = slice(g * gch, (g + 1) * gch)
            sp = _dot(wm, v[rows, cols].astype(BF16)) + bias
            o_ref[rows, cols] = (u[rows, cols] * sp).astype(o_ref.dtype)


def _gmlp(proj, lnw, lnb, w_prompt, w_sample, b_prompt, b_sample, l, n_prompt_rows, groups, gch,
          col_u, col_v, tm):
    m = proj.shape[0]
    gw = groups * gch
    npt = n_prompt_rows // tm
    return pl.pallas_call(
        functools.partial(_gmlp_body, n_prompt_tiles=npt, groups=groups, gch=gch),
        grid=(m // tm,),
        in_specs=[
            pl.BlockSpec((tm, gw), lambda i: (i, col_u)),
            pl.BlockSpec((tm, gw), lambda i: (i, col_v)),
            pl.BlockSpec((None, 1, gw), lambda i: (l, 0, 0)),
            pl.BlockSpec((None, 1, gw), lambda i: (l, 0, 0)),
            pl.BlockSpec((None, groups, GM_CHUNK, GM_CHUNK), lambda i: (l, 0, 0, 0)),
            pl.BlockSpec((None, groups, GM_CHUNK, GM_CHUNK), lambda i: (l, 0, 0, 0)),
            pl.BlockSpec((None, GM_CHUNK, groups), lambda i: (l, 0, 0)),
            pl.BlockSpec((None, GM_CHUNK, groups), lambda i: (l, 0, 0)),
        ],
        out_specs=[
            pl.BlockSpec((tm, gw), lambda i: (i, 0)),
            pl.BlockSpec((tm, gw), lambda i: (jnp.maximum(i - npt, 0), 0)),
        ],
        out_shape=[jax.ShapeDtypeStruct((m, gw), BF16), jax.ShapeDtypeStruct((m - n_prompt_rows, gw), F32)],
        compiler_params=_cparams(("arbitrary",)),
        name="gmlp",
    )(proj, proj, lnw, lnb, w_prompt, w_sample, b_prompt, b_sample)


def _ssd_body(xbc_ref, z_ref, misc_ref, dtrow_ref, arow_ref, prev_ref, s0_ref, cw_ref, cb_ref, dvec_ref,
              nw_ref, o_ref, conv_ref, st_ref, xp_scr, s_scr, y_scr, *, lt, q, heads, hp, ngroups, sn, conv_k):
    t = pl.program_id(1)
    nt = pl.num_programs(1)
    xw = heads * hp
    pad = 8

    @pl.when(t == 0)
    def _():
        xp_scr[0:pad, :] = prev_ref[...]
        s_scr[...] = s0_ref[...]

    xbc = xbc_ref[...]
    xp_scr[pad:pad + lt, :] = xbc
    acc = cb_ref[...]
    for i in range(conv_k):
        off = pad - (conv_k - 1) + i
        acc = acc + xp_scr[off:off + lt, :] * cw_ref[i:i + 1, :]
    xp_scr[0:pad, :] = xbc[lt - pad:lt, :]
    act = acc * jax.nn.sigmoid(acc)

    @pl.when(t == nt - 1)
    def _():
        conv_ref[...] = xbc[lt - (conv_k - 1):lt, :]

    dt_rows = dtrow_ref[...]
    ltp = dt_rows.shape[1]
    a_rows = jnp.concatenate(
        [_lane_cumsum(arow_ref[:, j * LANES:(j + 1) * LANES], q) for j in range(ltp // LANES)], axis=1)
    row = lax.broadcasted_iota(jnp.int32, (lt, lt), 0)
    col = lax.broadcasted_iota(jnp.int32, (lt, lt), 1)
    tri_all = jnp.where((col <= row) & (col >= (row & -q)), 1.0, 0.0).astype(BF16)
    a_cols = sum(_dot(tri_all, part.astype(BF16)) for part in _split3(misc_ref[...]))
    x_all = act[:, 0:xw]
    if lt < LANES:
        x_all = jnp.concatenate([x_all, jnp.zeros((LANES - lt, xw), F32)], axis=0)
    x_t = x_all.T
    tril = lax.broadcasted_iota(jnp.int32, (q, q), 1) <= lax.broadcasted_iota(jnp.int32, (q, q), 0)
    rep = heads // ngroups
    for c in range(lt // q):
        r0 = c * q
        bmat = [act[r0:r0 + q, xw + g * sn: xw + (g + 1) * sn].astype(BF16) for g in range(ngroups)]
        cmat = [act[r0:r0 + q, xw + (ngroups + g) * sn: xw + (ngroups + g + 1) * sn].astype(BF16)
                for g in range(ngroups)]
        gmat = [_dot_nt(cmat[g], bmat[g]) for g in range(ngroups)]
        for h in range(heads):
            g = h // rep
            a_row = a_rows[h:h + 1, r0:r0 + q]
            dt_row = dt_rows[h:h + 1, r0:r0 + q]
            a_col = a_cols[r0:r0 + q, 16 + h:17 + h]
            a_last = a_row[:, q - 1:q]
            lmat = jnp.exp(jnp.where(tril, a_col - a_row, -jnp.inf))
            scores = gmat[g] * lmat * dt_row
            xh = act[r0:r0 + q, h * hp:(h + 1) * hp]
            y = _dot(scores.astype(BF16), xh.astype(BF16))
            decay_end = jnp.exp(a_last - a_row) * dt_row
            xd_t = (x_t[h * hp:(h + 1) * hp, r0:r0 + q] * decay_end).astype(BF16)
            chunk_state = _dot(xd_t, bmat[g])
            s_in = s_scr[h]
            y = y + _dot_nt(cmat[g], s_in.astype(BF16)) * jnp.exp(a_col)
            s_scr[h] = s_in * jnp.exp(a_last) + chunk_state
            y_scr[r0:r0 + q, h * hp:(h + 1) * hp] = y + dvec_ref[:, h * hp:(h + 1) * hp] * xh

    zz = z_ref[...]
    o_ref[...] = _rmsnorm(y_scr[...] * (zz * jax.nn.sigmoid(zz)), nw_ref[...]).astype(o_ref.dtype)

    @pl.when(t == nt - 1)
    def _():
        st_ref[...] = s_scr[...]


def _ssd(proj, dt_rows, a_rows, conv_prev, s0, conv_w, conv_b, dvec, norm_w, l, row0, bsz, seq, lt, q,
         heads, hp, ngroups, sn, col_xbc, col_z, col_misc):
    nt = seq // lt
    xw = heads * hp
    cch = xw + 2 * ngroups * sn
    conv_k = conv_w.shape[1]
    rb0 = row0 // lt
    ltp = max(lt, LANES)
    body = functools.partial(_ssd_body, lt=lt, q=q, heads=heads, hp=hp, ngroups=ngroups, sn=sn, conv_k=conv_k)
    return pl.pallas_call(
        body,
        grid=(bsz, nt),
        in_specs=[
            pl.BlockSpec((lt, cch), lambda b, t: (rb0 + b * nt + t, col_xbc)),
            pl.BlockSpec((lt, xw), lambda b, t: (rb0 + b * nt + t, col_z)),
            pl.BlockSpec((lt, MISC_W), lambda b, t: (rb0 + b * nt + t, col_misc)),
            pl.BlockSpec((None, heads, ltp), lambda b, t: (b, 0, t)),
            pl.BlockSpec((None, heads, ltp), lambda b, t: (b, 0, t)),
            pl.BlockSpec((None, 8, cch), lambda b, t: (b, 0, 0)),
            pl.BlockSpec((None, heads, hp, sn), lambda b, t: (b, 0, 0, 0)),
            pl.BlockSpec((None, conv_k, cch), lambda b, t: (l, 0, 0)),
            pl.BlockSpec((None, 1, cch), lambda b, t: (l, 0, 0)),
            pl.BlockSpec((None, 1, xw), lambda b, t: (l, 0, 0)),
            pl.BlockSpec((None, 1, xw), lambda b, t: (l, 0, 0)),
        ],
        out_specs=[
            pl.BlockSpec((lt, xw), lambda b, t: (b * nt + t, 0)),
            pl.BlockSpec((None, conv_k - 1, cch), lambda b, t: (b, 0, 0)),
            pl.BlockSpec((None, heads, hp, sn), lambda b, t: (b, 0, 0, 0)),
        ],
        out_shape=[
            jax.ShapeDtypeStruct((bsz * seq, xw), BF16),
            jax.ShapeDtypeStruct((bsz, conv_k - 1, cch), F32),
            jax.ShapeDtypeStruct((bsz, heads, hp, sn), F32),
        ],
        scratch_shapes=[pltpu.VMEM((lt + 8, cch), F32), pltpu.VMEM((heads, hp, sn), F32),
                        pltpu.VMEM((lt, xw), F32)],
        compiler_params=_cparams(("parallel", "arbitrary")),
        name="ssd",
    )(proj, proj, proj, dt_rows, a_rows, conv_prev, s0, conv_w, conv_b, dvec, norm_w)


def _outproj_body(foxp_ref, foxs_ref, gm_ref, ssdp_ref, ssds_ref, x_ref, nw_ref, w_ref, o_ref, *, fw, gw, rows_a):
    i = pl.program_id(0)
    last_i = pl.num_programs(0) - 1

    def run(fox, ssd):
        o = _dot(fox, w_ref[0:fw, :])
        o = o + _dot(gm_ref[...], w_ref[fw:fw + gw, :])
        o = o + _dot(ssd, w_ref[fw + gw:, :])
        o_ref[...] = x_ref[...] + _rmsnorm(o, nw_ref[...])

    pl.when(i < last_i)(lambda: run(foxp_ref[...], ssdp_ref[...]))
    pl.when(i == last_i)(lambda: run(jnp.concatenate([foxp_ref[0:rows_a, :], foxs_ref[...]], axis=0),
                                     jnp.concatenate([ssdp_ref[0:rows_a, :], ssds_ref[...]], axis=0)))


def _outproj(fox_p, fox_s, gm_o, ssd_p, ssd_s, x, norm_w3, w_out, l, tm):
    m, d = x.shape
    n_a, n_b = fox_p.shape[0], fox_s.shape[0]
    fw, gw, sw = fox_p.shape[1], gm_o.shape[1], ssd_p.shape[1]
    rows_a = n_a % tm
    assert m % tm == 0 and rows_a + n_b == tm and rows_a % 16 == 0
    return pl.pallas_call(
        functools.partial(_outproj_body, fw=fw, gw=gw, rows_a=rows_a),
        grid=(m // tm,),
        in_specs=[
            pl.BlockSpec((tm, fw), lambda i: (i, 0)),
            pl.BlockSpec((n_b, fw), lambda i: (0, 0)),
            pl.BlockSpec((tm, gw), lambda i: (i, 0)),
            pl.BlockSpec((tm, sw), lambda i: (i, 0)),
            pl.BlockSpec((n_b, sw), lambda i: (0, 0)),
            pl.BlockSpec((tm, d), lambda i: (i, 0)),
            pl.BlockSpec((None, 1, d), lambda i: (l * 6 + 3, 0, 0)),
            pl.BlockSpec((None, fw + gw + sw, d), lambda i: (l, 0, 0)),
        ],
        out_specs=pl.BlockSpec((tm, d), lambda i: (i, 0)),
        out_shape=jax.ShapeDtypeStruct((m, d), F32),
        compiler_params=_cparams(("parallel",)),
        name=f"outproj_{l}",
    )(fox_p, fox_s, gm_o, ssd_p, ssd_s, x, norm_w3, w_out)


def _head_rows(vals, bsz, seq):
    return jnp.transpose(vals.reshape(bsz, seq, vals.shape[-1]), (0, 2, 1))


def _pad_lanes(rows, width):
    return jnp.pad(rows, ((0, 0), (0, 0), (0, width - rows.shape[-1])))


def kernel(x_prompt, x_sample, cache_fox_k, cache_fox_v, cache_fox_logf, state_ssd_conv, state_ssd, norm_w,
           w_ffn_gate, w_ffn_up, w_ffn_down, w_in, fox_fb, gm_ln_w, gm_ln_b, gm_ws, gm_bs, ssd_conv_w,
           ssd_conv_b, ssd_dt_bias, ssd_a_log, ssd_d, ssd_norm_w, w_out):
    bsz, seq, d = x_prompt.shape
    dbsz, dseq, _ = x_sample.shape
    depth, _, past, heads, dh = cache_fox_k.shape
    fox_w = heads * dh
    groups, gchunk = gm_ws.shape[1], gm_ws.shape[2]
    gm_w = gm_ln_w.shape[1]
    gch = gm_w // groups
    _, _, sheads, hp, sn = state_ssd.shape
    ssd_w = sheads * hp
    cch = ssd_conv_w.shape[2]
    ngroups = (cch - ssd_w) // (2 * sn)
    conv_k = ssd_conv_w.shape[1]
    n_p, n_s = bsz * seq, dbsz * dseq
    m = n_p + n_s
    assert gchunk == GM_CHUNK and heads == 8 and sheads == 8 and dh == LANES
    assert seq % GM_CHUNK == 0 and GM_CHUNK % dseq == 0 and dseq < SSD_CHUNK and past % LANES == 0

    wg, wu, wd = w_ffn_gate, w_ffn_up.astype(BF16), w_ffn_down
    wo = w_out.astype(BF16)
    sizes = [fox_w, fox_w, fox_w, heads, gm_w, gm_w, ssd_w, cch, sheads]
    offs = [0]
    for s in sizes:
        offs.append(offs[-1] + s)
    w_t = jnp.transpose(w_in, (0, 2, 1))
    wq, wk, wv, wf, wgu, wgv, wz, wxbc, wdt = (w_t[:, offs[i]:offs[i + 1], :] for i in range(9))
    tn = 1024
    used = 3 * fox_w + 2 * gm_w + cch + ssd_w + MISC_W
    npad = -(-used // tn) * tn
    zeros_misc = jnp.zeros((depth, MISC_W - heads - 2 * sheads, d), F32)
    zeros_tail = jnp.zeros((depth, npad - used, d), F32)
    w_r = jnp.concatenate([wq, wk, wv, wgu, wgv, wxbc, wz, wf, wdt, wdt, zeros_misc, zeros_tail],
                          axis=1).astype(BF16)
    col_q, col_k, col_v = 0, fox_w // dh, 2 * fox_w // dh
    col_u, col_gv = 3 * fox_w // gm_w, 3 * fox_w // gm_w + 1
    off_xbc = 3 * fox_w + 2 * gm_w
    off_z = off_xbc + cch
    off_misc = off_z + ssd_w
    assert off_xbc % cch == 0 and off_z % ssd_w == 0 and off_misc % MISC_W == 0
    misc_in_tile = off_misc % tn
    assert off_misc // tn == npad // tn - 1 and misc_in_tile + MISC_W <= tn

    lane_pad = jnp.zeros((depth, MISC_W - heads - 2 * sheads), F32)
    bias_vec = jnp.concatenate([fox_fb, ssd_dt_bias, ssd_dt_bias, lane_pad], axis=-1)[:, None, :]
    alog_vec = jnp.concatenate([jnp.zeros((depth, heads + sheads), F32), ssd_a_log, lane_pad], axis=-1)[:, None, :]
    norm_w3 = norm_w.reshape(depth * 6, 1, d)
    lnw3, lnb3 = gm_ln_w[:, None, :], gm_ln_b[:, None, :]
    reps = GM_CHUNK // dseq
    eye = jnp.eye(reps, dtype=F32)
    ws_s = jnp.einsum('ab,lgts->lgatbs', eye, gm_ws[:, :, :dseq, :dseq]).reshape(depth, groups, GM_CHUNK, GM_CHUNK)
    bs_p = jnp.transpose(gm_bs, (0, 2, 1))
    bs_s = jnp.tile(jnp.transpose(gm_bs[:, :, :dseq], (0, 2, 1)), (1, reps, 1))
    conv_b3 = ssd_conv_b[:, None, :]
    dvec = jnp.repeat(ssd_d, hp, axis=-1)[:, None, :]
    snw3 = ssd_norm_w[:, None, :]
    cache_k = cache_fox_k.reshape(depth, dbsz, past * heads, dh)
    cache_v = cache_fox_v.reshape(depth, dbsz, past * heads, dh)
    conv_prev_p = jnp.zeros((bsz, 8, cch), F32)
    conv_prev_s = jnp.pad(state_ssd_conv, ((0, 0), (0, 0), (8 - (conv_k - 1), 0), (0, 0)))
    s0_p = jnp.zeros((bsz, sheads, hp, sn), F32)

    x = (x_prompt.reshape(n_p, d), x_sample.reshape(n_s, d))
    tm_big = m // 8 if m % 64 == 0 and (m // 8) % 16 == 0 else 256
    tm_ffn = m // 12 if m % 12 == 0 and (m // 12) % 16 == 0 else 256
    tm_out = m // 16 if m % 16 == 0 and (m // 16) % 16 == 0 else 256
    lf_len = 2 * past
    assert past + dseq <= lf_len

    outs = {k: [] for k in ("plf", "pconv", "pst", "slf", "sgm", "sconv", "sst")}
    kv_p = kv_s = None
    for l in range(depth):
        x = _ffn(x, norm_w3, wg, wu, wd, l, 0, tm_ffn, 512)
        proj, misc_wide = _inproj(x, norm_w3, w_r, bias_vec, alog_vec, l, tm_big, tn, misc_in_tile)
        misc = misc_wide[:, :heads + 2 * sheads]
        logf_p, logf_s = misc[:n_p, :heads], misc[n_p:, :heads]
        lf_rows_p = _head_rows(logf_p, bsz, seq).reshape(bsz, heads, seq // LANES, LANES)
        lf_all = jnp.concatenate([cache_fox_logf[l], logf_s.reshape(dbsz, dseq, heads),
                                  jnp.zeros((dbsz, lf_len - past - dseq, heads), F32)], axis=1)
        lf_rows_s = jnp.transpose(lf_all, (0, 2, 1)).reshape(dbsz, heads, lf_len // LANES, LANES)
        dt_rows_p = _head_rows(misc[:n_p, heads:heads + sheads], bsz, seq)
        a_rows_p = _head_rows(misc[:n_p, heads + sheads:], bsz, seq)
        dt_rows_s = _pad_lanes(_head_rows(misc[n_p:, heads:heads + sheads], dbsz, dseq), LANES)
        a_rows_s = _pad_lanes(_head_rows(misc[n_p:, heads + sheads:], dbsz, dseq), LANES)

        fox_p, *kv_p = _fox_prompt(proj, lf_rows_p, kv_p, l, depth, bsz, seq, heads, dh, col_q, col_k, col_v, 256, 2)
        fox_s, *kv_s = _fox_sample(proj, lf_rows_s, cache_k, cache_v, kv_s, l, n_p, dbsz, dseq, heads, dh, 0, 1, 2)
        gm_o, gm_v_s = _gmlp(proj, lnw3, lnb3, gm_ws, ws_s, bs_p, bs_s, l, n_p, groups, gch, col_u, col_gv, 256)
        ssd_p, conv_p, st_p = _ssd(proj, dt_rows_p, a_rows_p, conv_prev_p, s0_p, ssd_conv_w, conv_b3, dvec, snw3,
                                   l, 0, bsz, seq, 256, SSD_CHUNK, sheads, hp, ngroups, sn,
                                   off_xbc // cch, off_z // ssd_w, off_misc // MISC_W)
        ssd_s, conv_s, st_s = _ssd(proj, dt_rows_s, a_rows_s, conv_prev_s[l], state_ssd[l], ssd_conv_w, conv_b3,
                                   dvec, snw3, l, n_p, dbsz, dseq, dseq, dseq, sheads, hp, ngroups, sn,
                                   off_xbc // cch, off_z // ssd_w, off_misc // MISC_W)
        x = _outproj(fox_p, fox_s, gm_o, ssd_p, ssd_s, x, norm_w3, wo, l, tm_out)
        x = _ffn(x, norm_w3, wg, wu, wd, l, 1, tm_ffn, 512,
                 split_out_rows=(n_p, n_s) if l == depth - 1 else None)

        outs["plf"].append(logf_p.reshape(bsz, seq, heads))
        outs["pconv"].append(conv_p)
        outs["pst"].append(st_p)
        outs["slf"].append(logf_s.reshape(dbsz, dseq, heads))
        outs["sgm"].append(gm_v_s.reshape(dbsz, dseq, gm_w))
        outs["sconv"].append(conv_s)
        outs["sst"].append(st_s)

    st = {k: jnp.stack(v) for k, v in outs.items()}
    y_prompt = x[0].reshape(bsz, seq, d)
    y_sample = x[1].reshape(dbsz, dseq, d)
    p_k, p_v = (a.reshape(depth, bsz, seq, heads, dh) for a in kv_p)
    s_k, s_v = (a.reshape(depth, dbsz, dseq, heads, dh) for a in kv_s)
    return (y_prompt, y_sample, p_k, p_v, st["plf"], st["pconv"], st["pst"],
            s_k, s_v, st["slf"], st["sgm"], st["sconv"], st["sst"])
```
